```python
import jax, jax.numpy as jnp
from jax import lax
import numpy as np

D_MODEL = 1024
BATCH = 8
SEQ = 2048
DEPTH = 1
DEC_BATCH = 128
DEC_SEQ = 4
PAST_LEN = 8192
PAGE_SIZE = 128

N_META = 16
N_HEADS = 8
QK_NOPE = 64
QK_ROPE = 32
V_HEAD = 64
QK_HEAD = QK_NOPE + QK_ROPE
Q_LORA = 384
KV_LORA = 256
ATTN_WIDTH = N_HEADS * V_HEAD
CONV_CH = D_MODEL - ATTN_WIDTH
CONV_GROUPS = 8
CONV_W = 3
D_FF = 4 * D_MODEL
IN_WIDTH = Q_LORA + KV_LORA + QK_ROPE + 3 * CONV_CH
ROPE_THETA = 10000.0
EPS = 1e-6
Q_BLOCK = 128
ATTN_SCALE = QK_HEAD ** -0.5

kernel_name = "hymba_mla_shortconv_decode_step"


def _rmsnorm(x, g):
    xf = x.astype(jnp.float32)
    y = xf * lax.rsqrt(jnp.mean(xf * xf, axis=-1, keepdims=True) + EPS)
    return (y * g.astype(jnp.float32)).astype(x.dtype)


def _rope_tables(pos):
    inv_freq = ROPE_THETA ** (-(jnp.arange(0, QK_ROPE, 2, dtype=jnp.float32) / QK_ROPE))
    ang = pos.astype(jnp.float32)[:, None] * inv_freq[None, :]
    return jnp.cos(ang), jnp.sin(ang)


def _apply_rope(x, cos, sin):
    xf = x.astype(jnp.float32)
    x1, x2 = xf[..., : QK_ROPE // 2], xf[..., QK_ROPE // 2:]
    out = jnp.concatenate([x1 * cos - x2 * sin, x2 * cos + x1 * sin], axis=-1)
    return out.astype(x.dtype)


def _mixer_inputs(h, w_in, q_lora_g, kv_lora_g, w_uq, q_norm_g, cos, sin):
    B, T = h.shape[0], h.shape[1]
    z = h @ w_in
    o1 = Q_LORA
    o2 = o1 + KV_LORA
    o3 = o2 + QK_ROPE
    o4 = o3 + CONV_CH
    o5 = o4 + CONV_CH
    cq, ckv, kpe = z[..., :o1], z[..., o1:o2], z[..., o2:o3]
    gb, gc, hc = z[..., o3:o4], z[..., o4:o5], z[..., o5:]
    q = (_rmsnorm(cq, q_lora_g) @ w_uq).reshape(B, T, N_HEADS, QK_HEAD)
    q_pe = _apply_rope(q[..., QK_NOPE:], cos[:, None, :], sin[:, None, :])
    q = _rmsnorm(jnp.concatenate([q[..., :QK_NOPE], q_pe], axis=-1), q_norm_g)
    lat = _rmsnorm(ckv, kv_lora_g)
    k_pe = _apply_rope(kpe, cos, sin)
    u = gc * hc
    return q, lat, k_pe, gb, u


def _expand_kv(lat, k_pe, w_ukv, k_norm_g):
    kv = (lat @ w_ukv).reshape(lat.shape[:-1] + (N_HEADS, QK_NOPE + V_HEAD))
    k_nope, v = kv[..., :QK_NOPE], kv[..., QK_NOPE:]
    k_rot = jnp.broadcast_to(k_pe[..., None, :], k_nope.shape[:-1] + (QK_ROPE,))
    k = _rmsnorm(jnp.concatenate([k_nope, k_rot], axis=-1), k_norm_g)
    return k, v


def _prompt_attention(q, k, v):
    B, T = q.shape[0], q.shape[1]
    nb = -(-T // Q_BLOCK)
    Tp = nb * Q_BLOCK
    pad = ((0, 0), (0, Tp - T), (0, 0), (0, 0))
    q, k, v = jnp.pad(q, pad), jnp.pad(k, pad), jnp.pad(v, pad)
    qb = q.reshape(B, nb, Q_BLOCK, N_HEADS, QK_HEAD).transpose(1, 0, 2, 3, 4)
    kpos = jnp.arange(Tp)

    def block(args):
        qblk, i = args
        qpos = i * Q_BLOCK + jnp.arange(Q_BLOCK)
        s = jnp.einsum('bqhd,bkhd->bhqk', qblk, k).astype(jnp.float32) * ATTN_SCALE
        s = jnp.where(kpos[None, :] <= qpos[:, None], s, -jnp.inf)
        p = jax.nn.softmax(s, axis=-1).astype(v.dtype)
        return jnp.einsum('bhqk,bkhd->bqhd', p, v)

    out = lax.map(block, (qb, jnp.arange(nb)))
    out = out.transpose(1, 0, 2, 3, 4).reshape(B, Tp, N_HEADS, V_HEAD)
    return out[:, :T]


def _sample_attention(q, lat_new, kpe_new, page_table, cache_lat, cache_kpe, w_ukv, k_norm_g):
    S = q.shape[1]
    past = page_table.shape[1] * PAGE_SIZE
    L = past + S
    kidx = jnp.arange(L)
    qidx = past + jnp.arange(S)
    mask = kidx[None, :] <= qidx[:, None]

    def one(args):
        pages, qs, lat, kp = args
        lat_all = jnp.concatenate([cache_lat[pages].reshape(past, KV_LORA), lat], axis=0)
        kp_all = jnp.concatenate([cache_kpe[pages].reshape(past, QK_ROPE), kp], axis=0)
        k, v = _expand_kv(lat_all, kp_all, w_ukv, k_norm_g)
        s = jnp.einsum('qhd,khd->hqk', qs, k).astype(jnp.float32) * ATTN_SCALE
        s = jnp.where(mask[None], s, -jnp.inf)
        p = jax.nn.softmax(s, axis=-1).astype(v.dtype)
        return jnp.einsum('hqk,khd->qhd', p, v)

    return lax.map(one, (page_table, q, lat_new, kpe_new))


def _short_conv(u_padded, gb, conv_w, conv_b):
    T = gb.shape[1]
    y = conv_b
    for j in range(CONV_W):
        y = y + u_padded[:, j:j + T] * conv_w[j]
    return gb * y


def _merge_and_ffn(x, attn, conv, attn_out_g, conv_out_g, w_o, norm_ffn_g, w_up, w_down):
    B, T = x.shape[0], x.shape[1]
    mix = jnp.concatenate([_rmsnorm(attn.reshape(B, T, ATTN_WIDTH), attn_out_g),
                           _rmsnorm(conv, conv_out_g)], axis=-1)
    x = x + mix @ w_o
    hf = _rmsnorm(x, norm_ffn_g)
    return x + jnp.square(jax.nn.relu(hf @ w_up)) @ w_down


def setup_inputs(seed: int = 0) -> dict:
    key = jax.random.key(seed)
    ks = jax.random.split(key, 24)
    n_pages = PAST_LEN // PAGE_SIZE
    n_pool = (DEC_BATCH * n_pages * 5) // 4
    f32 = jnp.float32

    def nrm(k, shape, scale):
        return jax.random.normal(k, shape, f32) * scale

    def gain(k, shape):
        return 1.0 + 0.02 * jax.random.normal(k, shape, f32)

    page_table = jax.random.permutation(ks[5], n_pool)[: DEC_BATCH * n_pages]
    page_table = page_table.reshape(DEC_BATCH, n_pages).astype(jnp.int32)
    return {
        "x_prompt": nrm(ks[0], (BATCH, SEQ, D_MODEL), 1.0),
        "x_sample": nrm(ks[1], (DEC_BATCH, DEC_SEQ, D_MODEL), 1.0),
        "cache_kv_latent": nrm(ks[2], (DEPTH, n_pool, PAGE_SIZE, KV_LORA), 1.0),
        "cache_k_rope": nrm(ks[3], (DEPTH, n_pool, PAGE_SIZE, QK_ROPE), 1.0),
        "state_conv": nrm(ks[4], (DEPTH, DEC_BATCH, CONV_W - 1, CONV_CH), 1.0),
        "page_table": page_table,
        "meta_tokens": nrm(ks[6], (N_META, D_MODEL), 1.0),
        "norm_mix_g": gain(ks[7], (DEPTH, D_MODEL)),
        "w_in": nrm(ks[8], (DEPTH, D_MODEL, IN_WIDTH), D_MODEL ** -0.5),
        "q_lora_g": gain(ks[9], (DEPTH, Q_LORA)),
        "kv_lora_g": gain(ks[10], (DEPTH, KV_LORA)),
        "w_uq": nrm(ks[11], (DEPTH, Q_LORA, N_HEADS * QK_HEAD), Q_LORA ** -0.5),
        "w_ukv": nrm(ks[12], (DEPTH, KV_LORA, N_HEADS * (QK_NOPE + V_HEAD)), KV_LORA ** -0.5),
        "q_norm_g": gain(ks[13], (DEPTH, QK_HEAD)),
        "k_norm_g": gain(ks[14], (DEPTH, QK_HEAD)),
        "conv_w": nrm(ks[15], (DEPTH, CONV_W, CONV_CH), CONV_W ** -0.5),
        "conv_b": nrm(ks[16], (DEPTH, CONV_CH), 0.01),
        "attn_out_g": gain(ks[17], (DEPTH, ATTN_WIDTH)),
        "conv_out_g": gain(ks[18], (DEPTH, CONV_CH)),
        "w_o": nrm(ks[19], (DEPTH, D_MODEL, D_MODEL), D_MODEL ** -0.5),
        "norm_ffn_g": gain(ks[20], (DEPTH, D_MODEL)),
        "w_up": nrm(ks[21], (DEPTH, D_MODEL, D_FF), D_MODEL ** -0.5),
        "w_down": nrm(ks[22], (DEPTH, D_FF, D_MODEL), D_FF ** -0.5),
    }


def reference(x_prompt, x_sample, cache_kv_latent, cache_k_rope, state_conv, page_table, meta_tokens,
              norm_mix_g, w_in, q_lora_g, kv_lora_g, w_uq, w_ukv, q_norm_g, k_norm_g, conv_w, conv_b,
              attn_out_g, conv_out_g, w_o, norm_ffn_g, w_up, w_down):
    B = x_prompt.shape[0]
    meta = jnp.broadcast_to(meta_tokens[None].astype(x_prompt.dtype), (B, N_META, D_MODEL))
    xp = jnp.concatenate([meta, x_prompt], axis=1)
    xs = x_sample
    Tp = xp.shape[1]
    S = xs.shape[1]
    past = page_table.shape[1] * PAGE_SIZE
    cos_p, sin_p = _rope_tables(jnp.arange(Tp))
    cos_s, sin_s = _rope_tables(past + jnp.arange(S))

    lat_p_all, kpe_p_all, conv_p_all = [], [], []
    lat_s_all, kpe_s_all, conv_s_all = [], [], []
    for l in range(DEPTH):
        hp = _rmsnorm(xp, norm_mix_g[l])
        q, lat, kpe, gb, u = _mixer_inputs(hp, w_in[l], q_lora_g[l], kv_lora_g[l], w_uq[l],
                                           q_norm_g[l], cos_p, sin_p)
        k, v = _expand_kv(lat, kpe, w_ukv[l], k_norm_g[l])
        attn = _prompt_attention(q, k, v)
        up = jnp.concatenate([jnp.zeros((B, CONV_W - 1, CONV_CH), u.dtype), u], axis=1)
        conv = _short_conv(up, gb, conv_w[l], conv_b[l])
        xp = _merge_and_ffn(xp, attn, conv, attn_out_g[l], conv_out_g[l], w_o[l],
                            norm_ffn_g[l], w_up[l], w_down[l])
        lat_p_all.append(lat)
        kpe_p_all.append(kpe)
        conv_p_all.append(up[:, -(CONV_W - 1):])

        hs = _rmsnorm(xs, norm_mix_g[l])
        q, lat, kpe, gb, u = _mixer_inputs(hs, w_in[l], q_lora_g[l], kv_lora_g[l], w_uq[l],
                                           q_norm_g[l], cos_s, sin_s)
        attn = _sample_attention(q, lat, kpe, page_table, cache_kv_latent[l], cache_k_rope[l],
                                 w_ukv[l], k_norm_g[l])
        us = jnp.concatenate([state_conv[l].astype(u.dtype), u], axis=1)
        conv = _short_conv(us, gb, conv_w[l], conv_b[l])
        xs = _merge_and_ffn(xs, attn, conv, attn_out_g[l], conv_out_g[l], w_o[l],
                            norm_ffn_g[l], w_up[l], w_down[l])
        lat_s_all.append(lat)
        kpe_s_all.append(kpe)
        conv_s_all.append(us[:, -(CONV_W - 1):])

    y_prompt = xp[:, N_META:]
    y_sample = xs
    return (y_prompt, y_sample,
            jnp.stack(lat_p_all), jnp.stack(kpe_p_all), jnp.stack(conv_p_all),
            jnp.stack(lat_s_all), jnp.stack(kpe_s_all), jnp.stack(conv_s_all))
```

```python
import functools

import jax
import jax.numpy as jnp
from jax import lax
from jax.experimental import pallas as pl
from jax.experimental.pallas import tpu as pltpu

D_MODEL = 1024
N_META = 16
N_HEADS = 8
QK_NOPE = 64
QK_ROPE = 32
V_HEAD = 64
QK_HEAD = QK_NOPE + QK_ROPE
Q_LORA = 384
KV_LORA = 256
ATTN_WIDTH = N_HEADS * V_HEAD
CONV_CH = D_MODEL - ATTN_WIDTH
CONV_W = 3
D_FF = 4 * D_MODEL
ROPE_THETA = 10000.0
EPS = 1e-6
PAGE_SIZE = 128
ATTN_SCALE = QK_HEAD ** -0.5

LANES = 128
SUBLANES = 8
SLOT = LANES
HEADS_W = N_HEADS * SLOT
HALF_ROPE = QK_ROPE // 2

C_CQ = 0
C_CKV = C_CQ + Q_LORA
C_GB = C_CKV + KV_LORA
C_GC = C_GB + CONV_CH
C_HC = C_GC + CONV_CH
C_KPE = C_HC + CONV_CH
C_KPR = C_KPE + SLOT
IN_W = C_KPR + SLOT

ROW_TILE = 512
ATT_TQ = 512
ATT_TK = 512
FF_CHUNK = 1024
PAGES_PER_STEP = 16
TAIL_KEYS = PAGE_SIZE
VMEM_LIMIT = 52 * 1024 * 1024

BF16 = jnp.bfloat16
F32 = jnp.float32

_NT = (((1,), (1,)), ((), ()))


def _dot(a, b):
    return jnp.dot(a, b, preferred_element_type=F32)


def _dot_nt(a, b):
    return lax.dot_general(a, b, _NT, preferred_element_type=F32)


def _rms(x, g):
    return x * lax.rsqrt(jnp.mean(x * x, axis=-1, keepdims=True) + EPS) * g


def _const_spec(shape):
    nd = len(shape)
    return pl.BlockSpec(shape, lambda *_: (0,) * nd, pipeline_mode=pl.Buffered(1))


def _mixer_kernel(*refs, rows, sample_mode):
    it = iter(refs)
    x_ref, cos_ref, sin_ref = next(it), next(it), next(it)
    if sample_mode:
        m1_ref, m2_ref, s1_ref, s2_ref = next(it), next(it), next(it), next(it)
    else:
        tail_in_ref = next(it)
    (g_mix_ref, w_in_ref, g_ql_ref, w_uq_ref, w_uqr_ref, g_kvl_ref, w_uk_ref, w_uv_ref,
     gq_ref, gk_ref, cw_ref, cb_ref, g_conv_ref) = (next(it) for _ in range(13))
    if sample_mode:
        (qk_ref, qabs_ref, k_ref, v_ref, lat_ref, kpe_ref, convn_ref, u_ref) = (next(it) for _ in range(8))
    else:
        (q_ref, k_ref, v_ref, lat_ref, kpe_ref, convn_ref, utail_ref) = (next(it) for _ in range(7))
    ubuf = next(it)

    hn = _rms(x_ref[...], g_mix_ref[...]).astype(BF16)
    cos = cos_ref[...]
    sin = sin_ref[...]

    cq = _dot(hn, w_in_ref[:, C_CQ:C_CKV])
    cqn = _rms(cq, g_ql_ref[...]).astype(BF16)
    gq = gq_ref[...]
    gk = gk_ref[...]
    for h in range(N_HEADS):
        sl = slice(h * SLOT, (h + 1) * SLOT)
        qh = _dot(cqn, w_uq_ref[:, sl]) * cos + _dot(cqn, w_uqr_ref[:, sl]) * sin
        ss = jnp.sum(qh * qh, axis=-1, keepdims=True)
        qh = qh * lax.rsqrt(ss * (1.0 / QK_HEAD) + EPS) * gq
        if sample_mode:
            qkh = (qh * gk).astype(BF16)
            qk_ref[:, sl] = qkh
            qabs_ref[:, h * KV_LORA:(h + 1) * KV_LORA] = _dot_nt(qkh, w_uk_ref[:, sl]).astype(BF16)
        else:
            q_ref[:, sl] = qh.astype(BF16)

    ckv = _dot(hn, w_in_ref[:, C_CKV:C_GB])
    lat = _rms(ckv, g_kvl_ref[...])
    lat_ref[...] = lat
    lat_b = lat.astype(BF16)
    zk = _dot(hn, w_in_ref[:, C_KPE:IN_W])
    krot = zk[:, :SLOT] * cos + zk[:, SLOT:] * sin
    kpe_ref[...] = krot[:, QK_NOPE:QK_HEAD]
    ss_rot = jnp.sum(krot * krot, axis=-1, keepdims=True)
    for h in range(N_HEADS):
        sl = slice(h * SLOT, (h + 1) * SLOT)
        kn = _dot(lat_b, w_uk_ref[:, sl])
        ss = jnp.sum(kn * kn, axis=-1, keepdims=True) + ss_rot
        kh = (kn + krot) * lax.rsqrt(ss * (1.0 / QK_HEAD) + EPS) * gk
        k_ref[:, sl] = kh.astype(BF16)
    v_ref[...] = _dot(lat_b, w_uv_ref[...]).astype(BF16)

    gc = _dot(hn, w_in_ref[:, C_GC:C_HC])
    hc = _dot(hn, w_in_ref[:, C_HC:C_KPE])
    u = gc * hc
    if sample_mode:
        ubuf[0:SUBLANES, :] = jnp.zeros((SUBLANES, CONV_CH), F32)
    else:
        t = pl.program_id(1)

        @pl.when(t == 0)
        def _():
            ubuf[0:SUBLANES, :] = tail_in_ref[...]

        @pl.when(t != 0)
        def _():
            ubuf[0:SUBLANES, :] = ubuf[rows:rows + SUBLANES, :]
    ubuf[SUBLANES:SUBLANES + rows, :] = u
    u1 = ubuf[SUBLANES - 1:SUBLANES - 1 + rows, :]
    u2 = ubuf[SUBLANES - 2:SUBLANES - 2 + rows, :]
    if sample_mode:
        u1 = u1 * m1_ref[...] + s1_ref[...]
        u2 = u2 * m2_ref[...] + s2_ref[...]
        u_ref[...] = u
    else:
        utail_ref[...] = u[rows - SUBLANES:, :]
    y = cb_ref[...] + u2 * cw_ref[0:1, :] + u1 * cw_ref[1:2, :] + u * cw_ref[2:3, :]
    gb = _dot(hn, w_in_ref[:, C_GB:C_GC])
    convn_ref[...] = _rms(gb * y, g_conv_ref[...]).astype(BF16)


def _mixer_call(x, cos_t, sin_t, conv_in, weights, *, sample_mode):
    (g_mix, w_in_p, g_ql, w_uq_p, w_uq_r, g_kvl, w_uk_p, w_uv, gq, gk, cw, cb, g_conv) = weights
    wspecs = [_const_spec(w.shape) for w in weights]
    if sample_mode:
        rows = x.shape[0]
        grid = (1,)
        row = lambda w: pl.BlockSpec((rows, w), lambda i: (0, 0))
        m1, m2, s1, s2 = conv_in
        in_specs = [row(D_MODEL), row(SLOT), row(SLOT), row(1), row(1), row(CONV_CH), row(CONV_CH)]
        args = [x, cos_t, sin_t, m1, m2, s1, s2]
        widths = [(HEADS_W, BF16), (N_HEADS * KV_LORA, BF16), (HEADS_W, BF16), (ATTN_WIDTH, BF16),
                  (KV_LORA, F32), (QK_ROPE, F32), (CONV_CH, BF16), (CONV_CH, F32)]
        out_shape = [jax.ShapeDtypeStruct((rows, w), d) for w, d in widths]
        out_specs = [row(w) for w, _ in widths]
        sem = ("arbitrary",)
    else:
        nb, seq, _ = x.shape
        rows = ROW_TILE
        nt = seq // rows
        grid = (nb, nt)
        row3 = lambda w: pl.BlockSpec((None, rows, w), lambda b, t: (b, t, 0))
        tab = pl.BlockSpec((rows, SLOT), lambda b, t: (t, 0))
        in_specs = [row3(D_MODEL), tab, tab, _const_spec(conv_in.shape)]
        args = [x, cos_t, sin_t, conv_in]
        widths = [(HEADS_W, BF16), (HEADS_W, BF16), (ATTN_WIDTH, BF16), (KV_LORA, F32),
                  (QK_ROPE, F32), (CONV_CH, BF16)]
        out_shape = [jax.ShapeDtypeStruct((nb, seq, w), d) for w, d in widths]
        out_shape.append(jax.ShapeDtypeStruct((nb, nt, SUBLANES, CONV_CH), F32))
        out_specs = [row3(w) for w, _ in widths]
        out_specs.append(pl.BlockSpec((None, None, SUBLANES, CONV_CH), lambda b, t: (b, t, 0, 0)))
        sem = ("arbitrary", "arbitrary")
    return pl.pallas_call(
        functools.partial(_mixer_kernel, rows=rows, sample_mode=sample_mode),
        grid=grid,
        in_specs=in_specs + wspecs,
        out_specs=out_specs,
        out_shape=out_shape,
        scratch_shapes=[pltpu.VMEM((rows + 2 * SUBLANES, CONV_CH), F32)],
        compiler_params=pltpu.CompilerParams(dimension_semantics=sem, vmem_limit_bytes=VMEM_LIMIT),
        name="mixer_sample" if sample_mode else "mixer_prompt",
    )(*args, *weights)


def _prompt_attn_kernel(q_ref, k_ref, v_ref, km_ref, vm_ref, o_ref, m_sc, l_sc, acc_sc):
    qi = pl.program_id(2)
    lane = lax.broadcasted_iota(jnp.int32, (ATT_TQ, SLOT), 1)
    causal = (lax.broadcasted_iota(jnp.int32, (ATT_TQ, ATT_TK), 1)
              <= lax.broadcasted_iota(jnp.int32, (ATT_TQ, ATT_TK), 0))
    outs = []
    for hh in range(2):
        sl = slice(hh * SLOT, (hh + 1) * SLOT)
        q = q_ref[:, sl]

        s = _dot_nt(q, km_ref[:, sl])
        m0 = jnp.max(s, axis=-1, keepdims=True)
        p = jnp.exp(s - m0)
        m_sc[...] = m0
        l_sc[...] = jnp.sum(p, axis=-1, keepdims=True)
        acc_sc[...] = _dot(p.astype(BF16), vm_ref[...])

        def update(ki, masked):
            ks = pl.multiple_of(ki * ATT_TK, ATT_TK)
            s = _dot_nt(q, k_ref[pl.ds(ks, ATT_TK), sl])
            if masked:
                s = jnp.where(causal, s, -jnp.inf)
            m_old = m_sc[...]
            m_new = jnp.maximum(m_old, jnp.max(s, axis=-1, keepdims=True))
            alpha = jnp.exp(m_old - m_new)
            p = jnp.exp(s - m_new)
            l_sc[...] = alpha * l_sc[...] + jnp.sum(p, axis=-1, keepdims=True)
            acc_sc[...] = alpha * acc_sc[...] + _dot(p.astype(BF16), v_ref[pl.ds(ks, ATT_TK), :])
            m_sc[...] = m_new

        def body(ki, carry):
            update(ki, False)
            return carry

        lax.fori_loop(0, qi, body, 0)
        update(qi, True)
        outs.append(acc_sc[...] / l_sc[...])
    o_ref[...] = jnp.where(lane < V_HEAD, outs[0], outs[1]).astype(o_ref.dtype)


def _prompt_attn_call(q, k, v, k_meta, v_meta):
    nb, seq, _ = q.shape
    grid = (nb, N_HEADS // 2, seq // ATT_TQ)
    return pl.pallas_call(
        _prompt_attn_kernel,
        grid=grid,
        in_specs=[
            pl.BlockSpec((None, ATT_TQ, 2 * SLOT), lambda b, p, i: (b, i, p)),
            pl.BlockSpec((None, seq, 2 * SLOT), lambda b, p, i: (b, 0, p)),
            pl.BlockSpec((None, seq, SLOT), lambda b, p, i: (b, 0, p)),
            pl.BlockSpec((N_META, 2 * SLOT), lambda b, p, i: (0, p)),
            pl.BlockSpec((N_META, SLOT), lambda b, p, i: (0, p)),
        ],
        out_specs=pl.BlockSpec((None, ATT_TQ, SLOT), lambda b, p, i: (b, i, p)),
        out_shape=jax.ShapeDtypeStruct((nb, seq, ATTN_WIDTH), BF16),
        scratch_shapes=[pltpu.VMEM((ATT_TQ, 1), F32), pltpu.VMEM((ATT_TQ, 1), F32),
                        pltpu.VMEM((ATT_TQ, SLOT), F32)],
        compiler_params=pltpu.CompilerParams(
            dimension_semantics=("arbitrary", "arbitrary", "arbitrary"),
            vmem_limit_bytes=VMEM_LIMIT),
        name="prompt_attn",
    )(q, k, v, k_meta, v_meta)


def _sample_attn_kernel(pt_ref, qabs_ref, qpe_ref, wukt_ref, wuv_ref, tlat_ref, tkpe_ref, *rest):
    del pt_ref
    npg = PAGES_PER_STEP
    lat_refs = rest[:npg]
    kpe_refs = rest[npg:2 * npg]
    o_ref = rest[2 * npg]
    m_sc, l_sc, acc_sc, latb_sc = rest[2 * npg + 1:]
    c = pl.program_id(1)
    nq = qabs_ref.shape[0]
    reps = nq // N_HEADS
    qabs = qabs_ref[...]
    qpe = qpe_ref[...]
    ones_r = jnp.ones((SUBLANES, QK_ROPE), BF16)

    def scores(lat, kpe):
        nk = lat.shape[0]
        lat_b = lat.astype(BF16)
        knt = _dot_nt(wukt_ref[...], lat_b)
        nsum = jnp.sum((knt * knt).reshape(N_HEADS, QK_NOPE, nk), axis=1)
        rsum = _dot_nt(ones_r, (kpe * kpe).astype(BF16))[0:1, :]
        r = lax.rsqrt((nsum + rsum) * (1.0 / QK_HEAD) + EPS)
        s = _dot_nt(qabs, lat_b) + _dot_nt(qpe, kpe.astype(BF16))
        return s * jnp.concatenate([r] * reps, axis=0), lat_b

    def update(s, pv):
        m_old = m_sc[...]
        m_new = jnp.maximum(m_old, jnp.max(s, axis=-1, keepdims=True))
        alpha = jnp.exp(m_old - m_new)
        p = jnp.exp(s - m_new)
        l_sc[...] = alpha * l_sc[...] + jnp.sum(p, axis=-1, keepdims=True)
        acc_sc[...] = alpha * acc_sc[...] + pv(p.astype(BF16))
        m_sc[...] = m_new

    @pl.when(c == 0)
    def _():
        m_sc[...] = jnp.full(m_sc.shape, -jnp.inf, F32)
        l_sc[...] = jnp.zeros(l_sc.shape, F32)
        acc_sc[...] = jnp.zeros(acc_sc.shape, F32)
        s, lat_b = scores(tlat_ref[...], tkpe_ref[...])
        key = lax.broadcasted_iota(jnp.int32, s.shape, 1)
        qt = lax.broadcasted_iota(jnp.int32, s.shape, 0) // N_HEADS
        update(jnp.where(key <= qt, s, -jnp.inf), lambda p: _dot(p, lat_b))

    parts = []
    for j in range(npg):
        s, lat_b = scores(lat_refs[j][...], kpe_refs[j][...])
        latb_sc[j * PAGE_SIZE:(j + 1) * PAGE_SIZE, :] = lat_b
        parts.append(s)
    update(jnp.concatenate(parts, axis=1), lambda p: _dot(p, latb_sc[...]))

    @pl.when(c == pl.num_programs(1) - 1)
    def _():
        o_lat = (acc_sc[...] / l_sc[...]).astype(BF16)
        full = _dot(o_lat, wuv_ref[...])
        row_h = lax.broadcasted_iota(jnp.int32, full.shape, 0) % N_HEADS
        col_h = lax.broadcasted_iota(jnp.int32, full.shape, 1) // V_HEAD
        own = jnp.where(row_h == col_h, full, 0.0)
        o_ref[...] = jnp.sum(own.reshape(reps, N_HEADS, ATTN_WIDTH), axis=1)


def _sample_attn_call(page_table, qabs, qpe, w_ukt, w_uv, tail_lat, tail_kpe, cache_lat, cache_kpe):
    nseq, n_pages = page_table.shape
    nq = qabs.shape[1]
    npg = PAGES_PER_STEP
    grid = (nseq, n_pages // npg)

    def page_spec(width, j):
        return pl.BlockSpec((None, PAGE_SIZE, width), lambda b, c, pt: (pt[b, c * npg + j], 0, 0))

    per_seq = lambda rows, width: pl.BlockSpec((None, rows, width), lambda b, c, pt: (b, 0, 0))
    whole = lambda shape: pl.BlockSpec(shape, lambda b, c, pt: (0,) * len(shape),
                                       pipeline_mode=pl.Buffered(1))
    in_specs = [per_seq(nq, KV_LORA), per_seq(nq, QK_ROPE), whole(w_ukt.shape), whole(w_uv.shape),
                per_seq(TAIL_KEYS, KV_LORA), per_seq(TAIL_KEYS, QK_ROPE)]
    in_specs += [page_spec(KV_LORA, j) for j in range(npg)]
    in_specs += [page_spec(QK_ROPE, j) for j in range(npg)]
    reps = nq // N_HEADS
    return pl.pallas_call(
        _sample_attn_kernel,
        grid_spec=pltpu.PrefetchScalarGridSpec(
            num_scalar_prefetch=1,
            grid=grid,
            in_specs=in_specs,
            out_specs=pl.BlockSpec((None, reps, ATTN_WIDTH), lambda b, c, pt: (b, 0, 0)),
            scratch_shapes=[pltpu.VMEM((nq, 1), F32), pltpu.VMEM((nq, 1), F32),
                            pltpu.VMEM((nq, KV_LORA), F32),
                            pltpu.VMEM((npg * PAGE_SIZE, KV_LORA), BF16)],
        ),
        out_shape=jax.ShapeDtypeStruct((nseq, reps, ATTN_WIDTH), F32),
        compiler_params=pltpu.CompilerParams(dimension_semantics=("arbitrary", "arbitrary"),
                                             vmem_limit_bytes=VMEM_LIMIT),
        name="sample_attn",
    )(page_table, qabs, qpe, w_ukt, w_uv, tail_lat, tail_kpe,
      *([cache_lat] * npg), *([cache_kpe] * npg))


def _merge_ffn_kernel(x_ref, attn_ref, convn_ref, g_attn_ref, w_o_ref, g_ffn_ref, w_up_ref,
                      w_down_ref, y_ref):
    an = _rms(attn_ref[...].astype(F32), g_attn_ref[...]).astype(BF16)
    x1 = x_ref[...] + (_dot(an, w_o_ref[0:ATTN_WIDTH, :]) + _dot(convn_ref[...], w_o_ref[ATTN_WIDTH:, :]))
    hf = _rms(x1, g_ffn_ref[...]).astype(BF16)
    ffn = None
    for c in range(D_FF // FF_CHUNK):
        cs = slice(c * FF_CHUNK, (c + 1) * FF_CHUNK)
        up = jnp.maximum(_dot(hf, w_up_ref[:, cs]), 0.0)
        part = _dot((up * up).astype(BF16), w_down_ref[cs, :])
        ffn = part if ffn is None else ffn + part
    y_ref[...] = x1 + ffn


def _merge_ffn_call(x, attn, convn, weights, rows):
    n = x.shape[0]
    row = lambda w: pl.BlockSpec((rows, w), lambda i: (i, 0))
    return pl.pallas_call(
        _merge_ffn_kernel,
        grid=(n // rows,),
        in_specs=[row(D_MODEL), row(ATTN_WIDTH), row(CONV_CH)] + [_const_spec(w.shape) for w in weights],
        out_specs=row(D_MODEL),
        out_shape=jax.ShapeDtypeStruct((n, D_MODEL), F32),
        compiler_params=pltpu.CompilerParams(dimension_semantics=("arbitrary",),
                                             vmem_limit_bytes=VMEM_LIMIT),
        name="merge_ffn",
    )(x, attn, convn, *weights)


def _slots(w, width):
    k = w.shape[0]
    w = w.reshape(k, N_HEADS, width)
    return jnp.pad(w, ((0, 0), (0, 0), (0, SLOT - width))).reshape(k, HEADS_W)


def _rot_partner(w):
    return jnp.concatenate([-w[..., HALF_ROPE:], w[..., :HALF_ROPE]], axis=-1)


def _rope_slot(w):
    return jnp.pad(w, ((0, 0), (QK_NOPE, SLOT - QK_HEAD)))


def _layer_weights(w_in, q_lora_g, kv_lora_g, w_uq, w_ukv, q_norm_g, k_norm_g, conv_w, conv_b,
                   conv_out_g, norm_mix_g):
    o1 = Q_LORA
    o2 = o1 + KV_LORA
    o3 = o2 + QK_ROPE
    w_kpe = w_in[:, o2:o3]
    w_in_p = jnp.concatenate(
        [w_in[:, :o2], w_in[:, o3:], _rope_slot(w_kpe), _rope_slot(_rot_partner(w_kpe))],
        axis=1).astype(BF16)
    uq = w_uq.reshape(Q_LORA, N_HEADS, QK_HEAD)
    w_uq_p = _slots(w_uq, QK_HEAD).astype(BF16)
    uq_rot = jnp.pad(_rot_partner(uq[..., QK_NOPE:]), ((0, 0), (0, 0), (QK_NOPE, SLOT - QK_HEAD)))
    w_uq_r = uq_rot.reshape(Q_LORA, HEADS_W).astype(BF16)
    ukv = w_ukv.reshape(KV_LORA, N_HEADS, QK_NOPE + V_HEAD)
    w_uk_p = _slots(ukv[..., :QK_NOPE].reshape(KV_LORA, N_HEADS * QK_NOPE), QK_NOPE).astype(BF16)
    w_uv = ukv[..., QK_NOPE:].reshape(KV_LORA, ATTN_WIDTH).astype(BF16)
    w_ukt = ukv[..., :QK_NOPE].reshape(KV_LORA, N_HEADS * QK_NOPE).T.astype(BF16)
    pad_g = lambda g: jnp.pad(g, (0, SLOT - QK_HEAD))[None, :]
    gq = pad_g(q_norm_g) * ATTN_SCALE
    gk = pad_g(k_norm_g)
    mixer = (norm_mix_g[None, :], w_in_p, q_lora_g[None, :], w_uq_p, w_uq_r, kv_lora_g[None, :],
             w_uk_p, w_uv, gq, gk, conv_w, conv_b[None, :], conv_out_g[None, :])
    return mixer, w_ukt, w_uv


def _rope_slot_tables(pos):
    inv_freq = ROPE_THETA ** (-(jnp.arange(0, QK_ROPE, 2, dtype=F32) / QK_ROPE))
    ang = pos.astype(F32)[:, None] * inv_freq[None, :]
    n = pos.shape[0]
    cos2 = jnp.concatenate([jnp.cos(ang)] * 2, axis=1)
    sin2 = jnp.concatenate([jnp.sin(ang)] * 2, axis=1)
    cos_t = jnp.concatenate([jnp.ones((n, QK_NOPE), F32), cos2, jnp.zeros((n, SLOT - QK_HEAD), F32)], axis=1)
    sin_t = jnp.pad(sin2, ((0, 0), (QK_NOPE, SLOT - QK_HEAD)))
    return cos_t, sin_t


def kernel(x_prompt, x_sample, cache_kv_latent, cache_k_rope, state_conv, page_table, meta_tokens,
           norm_mix_g, w_in, q_lora_g, kv_lora_g, w_uq, w_ukv, q_norm_g, k_norm_g, conv_w, conv_b,
           attn_out_g, conv_out_g, w_o, norm_ffn_g, w_up, w_down):
    depth = w_in.shape[0]
    assert depth == 1, "the prompt and sample streams are chained for a single layer"
    nb, seq, _ = x_prompt.shape
    nseq, dec, _ = x_sample.shape
    past = page_table.shape[1] * PAGE_SIZE
    l = 0

    mixer_w, w_ukt, w_uv = _layer_weights(w_in[l], q_lora_g[l], kv_lora_g[l], w_uq[l], w_ukv[l],
                                          q_norm_g[l], k_norm_g[l], conv_w[l], conv_b[l],
                                          conv_out_g[l], norm_mix_g[l])
    ffn_w = (attn_out_g[l][None, :], w_o[l].astype(BF16), norm_ffn_g[l][None, :],
             w_up[l].astype(BF16), w_down[l].astype(BF16))

    ns = nseq * dec
    xs_rows = jnp.concatenate([x_sample.reshape(ns, D_MODEL), meta_tokens.astype(F32)], axis=0)
    pos_s = jnp.concatenate([jnp.tile(past + jnp.arange(dec), nseq), jnp.arange(N_META)])
    cos_s, sin_s = _rope_slot_tables(pos_s)
    t_in_seq = jnp.concatenate([jnp.tile(jnp.arange(dec), nseq), jnp.arange(N_META)])
    m1 = (t_in_seq >= 1).astype(F32)[:, None]
    m2 = (t_in_seq >= 2).astype(F32)[:, None]
    st = state_conv[l].astype(F32)
    zrow = jnp.zeros((nseq, 1, CONV_CH), F32)
    s1 = jnp.concatenate([st[:, 1:2], zrow, zrow, zrow], axis=1)[:, :dec]
    s2 = jnp.concatenate([st[:, 0:1], st[:, 1:2], zrow, zrow], axis=1)[:, :dec]
    zmeta = jnp.zeros((N_META, CONV_CH), F32)
    s1 = jnp.concatenate([s1.reshape(ns, CONV_CH), zmeta], axis=0)
    s2 = jnp.concatenate([s2.reshape(ns, CONV_CH), zmeta], axis=0)
    (qk_s, qabs_s, k_s, v_s, lat_s, kpe_s, convn_s, u_s) = _mixer_call(
        xs_rows, cos_s, sin_s, (m1, m2, s1, s2), mixer_w, sample_mode=True)

    k_meta, v_meta = k_s[ns:], v_s[ns:]
    lat_meta, kpe_meta, u_meta = lat_s[ns:], kpe_s[ns:], u_s[ns:]

    cos_p, sin_p = _rope_slot_tables(N_META + jnp.arange(seq))
    q_p, k_p, v_p, lat_p, kpe_p, convn_p, utail_p = _mixer_call(
        x_prompt, cos_p, sin_p, u_meta[N_META - SUBLANES:], mixer_w, sample_mode=False)
    attn_p = _prompt_attn_call(q_p, k_p, v_p, k_meta, v_meta)
    y_prompt = _merge_ffn_call(x_prompt.reshape(nb * seq, D_MODEL), attn_p.reshape(nb * seq, ATTN_WIDTH),
                               convn_p.reshape(nb * seq, CONV_CH), ffn_w, ROW_TILE)
    y_prompt = y_prompt.reshape(nb, seq, D_MODEL)
    bcast = lambda a: jnp.broadcast_to(a[None], (nb,) + a.shape)
    new_lat_prompt = jnp.concatenate([bcast(lat_meta), lat_p], axis=1)[None]
    new_kpe_prompt = jnp.concatenate([bcast(kpe_meta), kpe_p], axis=1)[None]
    new_conv_prompt = utail_p[:, -1, SUBLANES - (CONV_W - 1):][None]

    nq = dec * N_HEADS
    qabs = qabs_s[:ns].reshape(nseq, nq, KV_LORA)
    qpe = qk_s[:ns].reshape(ns, N_HEADS, SLOT)[:, :, QK_NOPE:QK_HEAD].reshape(nseq, nq, QK_ROPE)
    pad_tail = lambda a: jnp.pad(a[:ns].reshape(nseq, dec, -1), ((0, 0), (0, TAIL_KEYS - dec), (0, 0)))
    attn_s = _sample_attn_call(page_table, qabs, qpe, w_ukt, w_uv, pad_tail(lat_s), pad_tail(kpe_s),
                               cache_kv_latent[l], cache_k_rope[l])
    y_sample = _merge_ffn_call(x_sample.reshape(ns, D_MODEL), attn_s.reshape(ns, ATTN_WIDTH),
                               convn_s[:ns], ffn_w, ns)
    y_sample = y_sample.reshape(nseq, dec, D_MODEL)
    new_lat_sample = lat_s[:ns].reshape(nseq, dec, KV_LORA)[None]
    new_kpe_sample = kpe_s[:ns].reshape(nseq, dec, QK_ROPE)[None]
    us = jnp.concatenate([st, u_s[:ns].reshape(nseq, dec, CONV_CH)], axis=1)
    new_conv_sample = us[:, -(CONV_W - 1):][None]

    return (y_prompt, y_sample, new_lat_prompt, new_kpe_prompt, new_conv_prompt,
            new_lat_sample, new_kpe_sample, new_conv_sample)
```

```python
import functools

import jax
import jax.numpy as jnp
from jax import lax
from jax.experimental import pallas as pl
from jax.experimental.pallas import tpu as pltpu

D_MODEL = 1024
N_META = 16
N_HEADS = 8
QK_NOPE = 64
QK_ROPE = 32
V_HEAD = 64
QK_HEAD = QK_NOPE + QK_ROPE
Q_LORA = 384
KV_LORA = 256
ATTN_WIDTH = N_HEADS * V_HEAD
CONV_CH = D_MODEL - ATTN_WIDTH
CONV_W = 3
D_FF = 4 * D_MODEL
ROPE_THETA = 10000.0
EPS = 1e-6
PAGE_SIZE = 128
ATTN_SCALE = QK_HEAD ** -0.5

LANES = 128
SUBLANES = 8
SLOT = LANES
HEADS_W = N_HEADS * SLOT
HALF_ROPE = QK_ROPE // 2

C_CQ = 0
C_CKV = C_CQ + Q_LORA
C_GB = C_CKV + KV_LORA
C_GC = C_GB + CONV_CH
C_HC = C_GC + CONV_CH
C_KPE = C_HC + CONV_CH
C_KPR = C_KPE + SLOT
IN_W = C_KPR + SLOT

ROW_TILE = 512
ATT_TQ = 512
ATT_TK = 512
FF_CHUNK = 1024
PAGES_PER_STEP = 16
TAIL_KEYS = PAGE_SIZE
VMEM_LIMIT = 52 * 1024 * 1024

BF16 = jnp.bfloat16
F32 = jnp.float32

_NT = (((1,), (1,)), ((), ()))


def _dot(a, b):
    return jnp.dot(a, b, preferred_element_type=F32)


def _dot_nt(a, b):
    return lax.dot_general(a, b, _NT, preferred_element_type=F32)


def _rms(x, g):
    return x * lax.rsqrt(jnp.mean(x * x, axis=-1, keepdims=True) + EPS) * g


def _const_spec(shape):
    nd = len(shape)
    return pl.BlockSpec(shape, lambda *_: (0,) * nd, pipeline_mode=pl.Buffered(1))


def _mixer_kernel(*refs, rows, sample_mode):
    it = iter(refs)
    x_ref, cos_ref, sin_ref = next(it), next(it), next(it)
    if sample_mode:
        m1_ref, m2_ref, s1_ref, s2_ref = next(it), next(it), next(it), next(it)
    else:
        tail_in_ref = next(it)
    (g_mix_ref, w_in_ref, g_ql_ref, w_uq_ref, w_uqr_ref, g_kvl_ref, w_uk_ref, w_uv_ref,
     gq_ref, gk_ref, cw_ref, cb_ref, g_conv_ref) = (next(it) for _ in range(13))
    if sample_mode:
        (qk_ref, qabs_ref, k_ref, v_ref, lat_ref, kpe_ref, convn_ref, u_ref) = (next(it) for _ in range(8))
    else:
        (q_ref, k_ref, v_ref, lat_ref, kpe_ref, convn_ref, utail_ref) = (next(it) for _ in range(7))
    ubuf = next(it)

    hn = _rms(x_ref[...], g_mix_ref[...]).astype(BF16)
    cos = cos_ref[...]
    sin = sin_ref[...]

    cq = _dot(hn, w_in_ref[:, C_CQ:C_CKV])
    cqn = _rms(cq, g_ql_ref[...]).astype(BF16)
    gq = gq_ref[...]
    gk = gk_ref[...]
    for h in range(N_HEADS):
        sl = slice(h * SLOT, (h + 1) * SLOT)
        qh = _dot(cqn, w_uq_ref[:, sl]) * cos + _dot(cqn, w_uqr_ref[:, sl]) * sin
        ss = jnp.sum(qh * qh, axis=-1, keepdims=True)
        qh = qh * lax.rsqrt(ss * (1.0 / QK_HEAD) + EPS) * gq
        if sample_mode:
            qkh = (qh * gk).astype(BF16)
            qk_ref[:, sl] = qkh
            qabs_ref[:, h * KV_LORA:(h + 1) * KV_LORA] = _dot_nt(qkh, w_uk_ref[:, sl]).astype(BF16)
        else:
            q_ref[:, sl] = qh.astype(BF16)

    ckv = _dot(hn, w_in_ref[:, C_CKV:C_GB])
    lat = _rms(ckv, g_kvl_ref[...])
    lat_ref[...] = lat
    lat_b = lat.astype(BF16)
    zk = _dot(hn, w_in_ref[:, C_KPE:IN_W])
    krot = zk[:, :SLOT] * cos + zk[:, SLOT:] * sin
    kpe_ref[...] = krot[:, QK_NOPE:QK_HEAD]
    ss_rot = jnp.sum(krot * krot, axis=-1, keepdims=True)
    for h in range(N_HEADS):
        sl = slice(h * SLOT, (h + 1) * SLOT)
        kn = _dot(lat_b, w_uk_ref[:, sl])
        ss = jnp.sum(kn * kn, axis=-1, keepdims=True) + ss_rot
        kh = (kn + krot) * lax.rsqrt(ss * (1.0 / QK_HEAD) + EPS) * gk
        k_ref[:, sl] = kh.astype(BF16)
    v_ref[...] = _dot(lat_b, w_uv_ref[...]).astype(BF16)

    gc = _dot(hn, w_in_ref[:, C_GC:C_HC])
    hc = _dot(hn, w_in_ref[:, C_HC:C_KPE])
    u = gc * hc
    if sample_mode:
        ubuf[0:SUBLANES, :] = jnp.zeros((SUBLANES, CONV_CH), F32)
    else:
        t = pl.program_id(1)

        @pl.when(t == 0)
        def _():
            ubuf[0:SUBLANES, :] = tail_in_ref[...]

        @pl.when(t != 0)
        def _():
            ubuf[0:SUBLANES, :] = ubuf[rows:rows + SUBLANES, :]
    ubuf[SUBLANES:SUBLANES + rows, :] = u
    u1 = ubuf[SUBLANES - 1:SUBLANES - 1 + rows, :]
    u2 = ubuf[SUBLANES - 2:SUBLANES - 2 + rows, :]
    if sample_mode:
        u1 = u1 * m1_ref[...] + s1_ref[...]
        u2 = u2 * m2_ref[...] + s2_ref[...]
        u_ref[...] = u
    else:
        utail_ref[...] = u[rows - SUBLANES:, :]
    y = cb_ref[...] + u2 * cw_ref[0:1, :] + u1 * cw_ref[1:2, :] + u * cw_ref[2:3, :]
    gb = _dot(hn, w_in_ref[:, C_GB:C_GC])
    convn_ref[...] = _rms(gb * y, g_conv_ref[...]).astype(BF16)


def _mixer_call(x, cos_t, sin_t, conv_in, weights, *, sample_mode):
    (g_mix, w_in_p, g_ql, w_uq_p, w_uq_r, g_kvl, w_uk_p, w_uv, gq, gk, cw, cb, g_conv) = weights
    wspecs = [_const_spec(w.shape) for w in weights]
    if sample_mode:
        rows = x.shape[0]
        grid = (1,)
        row = lambda w: pl.BlockSpec((rows, w), lambda i: (0, 0))
        m1, m2, s1, s2 = conv_in
        in_specs = [row(D_MODEL), row(SLOT), row(SLOT), row(1), row(1), row(CONV_CH), row(CONV_CH)]
        args = [x, cos_t, sin_t, m1, m2, s1, s2]
        widths = [(HEADS_W, BF16), (N_HEADS * KV_LORA, BF16), (HEADS_W, BF16), (ATTN_WIDTH, BF16),
                  (KV_LORA, F32), (QK_ROPE, F32), (CONV_CH, BF16), (CONV_CH, F32)]
        out_shape = [jax.ShapeDtypeStruct((rows, w), d) for w, d in widths]
        out_specs = [row(w) for w, _ in widths]
        sem = ("arbitrary",)
    else:
        nb, seq, _ = x.shape
        rows = ROW_TILE
        nt = seq // rows
        grid = (nb, nt)
        row3 = lambda w: pl.BlockSpec((None, rows, w), lambda b, t: (b, t, 0))
        tab = pl.BlockSpec((rows, SLOT), lambda b, t: (t, 0))
        in_specs = [row3(D_MODEL), tab, tab, _const_spec(conv_in.shape)]
        args = [x, cos_t, sin_t, conv_in]
        widths = [(HEADS_W, BF16), (HEADS_W, BF16), (ATTN_WIDTH, BF16), (KV_LORA, F32),
                  (QK_ROPE, F32), (CONV_CH, BF16)]
        out_shape = [jax.ShapeDtypeStruct((nb, seq, w), d) for w, d in widths]
        out_shape.append(jax.ShapeDtypeStruct((nb, nt, SUBLANES, CONV_CH), F32))
        out_specs = [row3(w) for w, _ in widths]
        out_specs.append(pl.BlockSpec((None, None, SUBLANES, CONV_CH), lambda b, t: (b, t, 0, 0)))
        sem = ("arbitrary", "arbitrary")
    return pl.pallas_call(
        functools.partial(_mixer_kernel, rows=rows, sample_mode=sample_mode),
        grid=grid,
        in_specs=in_specs + wspecs,
        out_specs=out_specs,
        out_shape=out_shape,
        scratch_shapes=[pltpu.VMEM((rows + 2 * SUBLANES, CONV_CH), F32)],
        compiler_params=pltpu.CompilerParams(dimension_semantics=sem, vmem_limit_bytes=VMEM_LIMIT),
        name="mixer_sample" if sample_mode else "mixer_prompt",
    )(*args, *weights)


def _softmax_update(s, m_ref, l_ref, acc_ref, pv):
    n = s.shape[1] // LANES
    m_old = m_ref[...]
    m_new = jnp.maximum(m_old, jnp.max(s, axis=-1, keepdims=True))
    alpha = jnp.exp(m_old - m_new)
    ps = [jnp.exp(s[:, c * LANES:(c + 1) * LANES] - m_new) for c in range(n)]
    l_ref[...] = alpha * l_ref[...] + functools.reduce(lambda a, b: a + b, ps)
    p = ps[0] if n == 1 else jnp.concatenate(ps, axis=1)
    w = acc_ref.shape[-1] // LANES
    alpha_w = alpha if w == 1 else jnp.concatenate([alpha] * w, axis=1)
    acc_ref[...] = alpha_w * acc_ref[...] + pv(p.astype(BF16))
    m_ref[...] = m_new


def _prompt_attn_kernel(q_ref, k_ref, v_ref, km_ref, vm_ref, o_ref, m_sc, l_sc, acc_sc):
    qi = pl.program_id(2)
    lane = lax.broadcasted_iota(jnp.int32, (ATT_TQ, SLOT), 1)
    m_sc[...] = jnp.full(m_sc.shape, -jnp.inf, F32)
    l_sc[...] = jnp.zeros(l_sc.shape, F32)
    acc_sc[...] = jnp.zeros(acc_sc.shape, F32)

    def step(hh, k_blk, v_blk, mask):
        sl = slice(hh * SLOT, (hh + 1) * SLOT)
        s = _dot_nt(q_ref[:, sl], k_blk[:, sl])
        if mask is not None:
            s = jnp.where(mask, s, -jnp.inf)
        _softmax_update(s, m_sc.at[hh], l_sc.at[hh], acc_sc.at[hh], lambda p: _dot(p, v_blk))

    meta = lane < N_META
    for hh in range(2):
        step(hh, km_ref[...], vm_ref[...], meta)

    def block(ki, mask):
        ks = pl.multiple_of(ki * ATT_TK, ATT_TK)
        k_blk = k_ref[pl.ds(ks, ATT_TK), :]
        v_blk = v_ref[pl.ds(ks, ATT_TK), :]
        for hh in range(2):
            step(hh, k_blk, v_blk, mask)

    def body(ki, carry):
        block(ki, None)
        return carry

    lax.fori_loop(0, qi, body, 0)
    block(qi, lax.broadcasted_iota(jnp.int32, (ATT_TQ, ATT_TK), 1)
          <= lax.broadcasted_iota(jnp.int32, (ATT_TQ, ATT_TK), 0))

    outs = [acc_sc[hh] / jnp.sum(l_sc[hh], axis=-1, keepdims=True) for hh in range(2)]
    o_ref[...] = jnp.where(lane < V_HEAD, outs[0], outs[1]).astype(o_ref.dtype)


def _prompt_attn_call(q, k, v, k_meta, v_meta):
    nb, seq, _ = q.shape
    grid = (nb, N_HEADS // 2, seq // ATT_TQ)
    return pl.pallas_call(
        _prompt_attn_kernel,
        grid=grid,
        in_specs=[
            pl.BlockSpec((None, ATT_TQ, 2 * SLOT), lambda b, p, i: (b, i, p)),
            pl.BlockSpec((None, seq, 2 * SLOT), lambda b, p, i: (b, 0, p)),
            pl.BlockSpec((None, seq, SLOT), lambda b, p, i: (b, 0, p)),
            pl.BlockSpec((LANES, 2 * SLOT), lambda b, p, i: (0, p)),
            pl.BlockSpec((LANES, SLOT), lambda b, p, i: (0, p)),
        ],
        out_specs=pl.BlockSpec((None, ATT_TQ, SLOT), lambda b, p, i: (b, i, p)),
        out_shape=jax.ShapeDtypeStruct((nb, seq, ATTN_WIDTH), BF16),
        scratch_shapes=[pltpu.VMEM((2, ATT_TQ, LANES), F32)] * 3,
        compiler_params=pltpu.CompilerParams(
            dimension_semantics=("arbitrary", "arbitrary", "arbitrary"),
            vmem_limit_bytes=VMEM_LIMIT),
        name="prompt_attn",
    )(q, k, v, k_meta, v_meta)


def _sample_attn_kernel(pt_ref, qabs_ref, qpe_ref, wukt_ref, wuv_ref, tlat_ref, tkpe_ref, *rest):
    del pt_ref
    npg = PAGES_PER_STEP
    lat_refs = rest[:npg]
    kpe_refs = rest[npg:2 * npg]
    o_ref = rest[2 * npg]
    m_sc, l_sc, acc_sc, latb_sc, lhs_sc = rest[2 * npg + 1:]
    b = pl.program_id(0)
    c = pl.program_id(1)
    nq = qabs_ref.shape[0]
    reps = nq // N_HEADS
    nkn = N_HEADS * QK_NOPE
    qpe = qpe_ref[...]

    @pl.when(jnp.logical_and(b == 0, c == 0))
    def _():
        lhs_sc[0:nkn, :] = wukt_ref[...]

    @pl.when(c == 0)
    def _():
        lhs_sc[nkn:nkn + nq, :] = qabs_ref[...]
        m_sc[...] = jnp.full(m_sc.shape, -jnp.inf, F32)
        l_sc[...] = jnp.zeros(l_sc.shape, F32)
        acc_sc[...] = jnp.zeros(acc_sc.shape, F32)

    def scores(lat_b, kpe_t):
        nk = lat_b.shape[0]
        full = _dot_nt(lhs_sc[...], lat_b)
        knt = full[:nkn]
        nsum = jnp.sum((knt * knt).reshape(QK_NOPE, N_HEADS, nk), axis=0)
        rsum = jnp.sum(kpe_t * kpe_t, axis=0, keepdims=True)
        r = lax.rsqrt((nsum + rsum) * (1.0 / QK_HEAD) + EPS)
        s = full[nkn:] + _dot(qpe, kpe_t.astype(BF16))
        return s * jnp.concatenate([r] * reps, axis=0)

    @pl.when(c == 0)
    def _():
        lat_b = tlat_ref[...].astype(BF16)
        s = scores(lat_b, tkpe_ref[...])
        key = lax.broadcasted_iota(jnp.int32, s.shape, 1)
        qt = lax.broadcasted_iota(jnp.int32, s.shape, 0) // N_HEADS
        _softmax_update(jnp.where(key <= qt, s, -jnp.inf), m_sc, l_sc, acc_sc,
                        lambda p: _dot(p, lat_b))

    for j in range(npg):
        latb_sc[j * PAGE_SIZE:(j + 1) * PAGE_SIZE, :] = lat_refs[j][...].astype(BF16)
    parts = []
    for j in range(0, npg, 2):
        kpe_t = jnp.concatenate([kpe_refs[j][...], kpe_refs[j + 1][...]], axis=1)
        parts.append(scores(latb_sc[j * PAGE_SIZE:(j + 2) * PAGE_SIZE, :], kpe_t))
    _softmax_update(jnp.concatenate(parts, axis=1), m_sc, l_sc, acc_sc,
                    lambda p: _dot(p, latb_sc[...]))

    @pl.when(c == pl.num_programs(1) - 1)
    def _():
        o_lat = (acc_sc[...] / jnp.sum(l_sc[...], axis=-1, keepdims=True)).astype(BF16)
        full = _dot(o_lat, wuv_ref[...])
        row_h = lax.broadcasted_iota(jnp.int32, full.shape, 0) % N_HEADS
        col_h = lax.broadcasted_iota(jnp.int32, full.shape, 1) // V_HEAD
        own = jnp.where(row_h == col_h, full, 0.0)
        o_ref[...] = jnp.sum(own.reshape(reps, N_HEADS, ATTN_WIDTH), axis=1)


def _sample_attn_call(page_table, qabs, qpe, w_ukt, w_uv, tail_lat, tail_kpe_t, cache_lat, cache_kpe_t):
    nseq, n_pages = page_table.shape
    nq = qabs.shape[1]
    npg = PAGES_PER_STEP
    grid = (nseq, n_pages // npg)

    def page_spec(rows, width, j):
        return pl.BlockSpec((None, rows, width), lambda b, c, pt: (pt[b, c * npg + j], 0, 0))

    per_seq = lambda rows, width: pl.BlockSpec((None, rows, width), lambda b, c, pt: (b, 0, 0))
    whole = lambda shape: pl.BlockSpec(shape, lambda b, c, pt: (0,) * len(shape),
                                       pipeline_mode=pl.Buffered(1))
    in_specs = [per_seq(nq, KV_LORA), per_seq(nq, QK_ROPE), whole(w_ukt.shape), whole(w_uv.shape),
                per_seq(TAIL_KEYS, KV_LORA), per_seq(QK_ROPE, TAIL_KEYS)]
    in_specs += [page_spec(PAGE_SIZE, KV_LORA, j) for j in range(npg)]
    in_specs += [page_spec(QK_ROPE, PAGE_SIZE, j) for j in range(npg)]
    reps = nq // N_HEADS
    return pl.pallas_call(
        _sample_attn_kernel,
        grid_spec=pltpu.PrefetchScalarGridSpec(
            num_scalar_prefetch=1,
            grid=grid,
            in_specs=in_specs,
            out_specs=pl.BlockSpec((None, reps, ATTN_WIDTH), lambda b, c, pt: (b, 0, 0)),
            scratch_shapes=[pltpu.VMEM((nq, LANES), F32), pltpu.VMEM((nq, LANES), F32),
                            pltpu.VMEM((nq, KV_LORA), F32),
                            pltpu.VMEM((npg * PAGE_SIZE, KV_LORA), BF16),
                            pltpu.VMEM((N_HEADS * QK_NOPE + nq, KV_LORA), BF16)],
        ),
        out_shape=jax.ShapeDtypeStruct((nseq, reps, ATTN_WIDTH), F32),
        compiler_params=pltpu.CompilerParams(dimension_semantics=("arbitrary", "arbitrary"),
                                             vmem_limit_bytes=VMEM_LIMIT),
        name="sample_attn",
    )(page_table, qabs, qpe, w_ukt, w_uv, tail_lat, tail_kpe_t,
      *([cache_lat] * npg), *([cache_kpe_t] * npg))


def _merge_ffn_kernel(x_ref, attn_ref, convn_ref, g_attn_ref, w_o_ref, g_ffn_ref, w_up_ref,
                      w_down_ref, y_ref):
    an = _rms(attn_ref[...].astype(F32), g_attn_ref[...]).astype(BF16)
    x1 = x_ref[...] + (_dot(an, w_o_ref[0:ATTN_WIDTH, :]) + _dot(convn_ref[...], w_o_ref[ATTN_WIDTH:, :]))
    hf = _rms(x1, g_ffn_ref[...]).astype(BF16)
    ffn = None
    for c in range(D_FF // FF_CHUNK):
        cs = slice(c * FF_CHUNK, (c + 1) * FF_CHUNK)
        up = jnp.maximum(_dot(hf, w_up_ref[:, cs]), 0.0)
        part = _dot((up * up).astype(BF16), w_down_ref[cs, :])
        ffn = part if ffn is None else ffn + part
    y_ref[...] = x1 + ffn


def _merge_ffn_call(x, attn, convn, weights, rows):
    n = x.shape[0]
    row = lambda w: pl.BlockSpec((rows, w), lambda i: (i, 0))
    return pl.pallas_call(
        _merge_ffn_kernel,
        grid=(n // rows,),
        in_specs=[row(D_MODEL), row(ATTN_WIDTH), row(CONV_CH)] + [_const_spec(w.shape) for w in weights],
        out_specs=row(D_MODEL),
        out_shape=jax.ShapeDtypeStruct((n, D_MODEL), F32),
        compiler_params=pltpu.CompilerParams(dimension_semantics=("arbitrary",),
                                             vmem_limit_bytes=VMEM_LIMIT),
        name="merge_ffn",
    )(x, attn, convn, *weights)


def _slots(w, width):
    k = w.shape[0]
    w = w.reshape(k, N_HEADS, width)
    return jnp.pad(w, ((0, 0), (0, 0), (0, SLOT - width))).reshape(k, HEADS_W)


def _rot_partner(w):
    return jnp.concatenate([-w[..., HALF_ROPE:], w[..., :HALF_ROPE]], axis=-1)


def _rope_slot(w):
    return jnp.pad(w, ((0, 0), (QK_NOPE, SLOT - QK_HEAD)))


def _layer_weights(w_in, q_lora_g, kv_lora_g, w_uq, w_ukv, q_norm_g, k_norm_g, conv_w, conv_b,
                   conv_out_g, norm_mix_g):
    o1 = Q_LORA
    o2 = o1 + KV_LORA
    o3 = o2 + QK_ROPE
    w_kpe = w_in[:, o2:o3]
    w_in_p = jnp.concatenate(
        [w_in[:, :o2], w_in[:, o3:], _rope_slot(w_kpe), _rope_slot(_rot_partner(w_kpe))],
        axis=1).astype(BF16)
    uq = w_uq.reshape(Q_LORA, N_HEADS, QK_HEAD)
    w_uq_p = _slots(w_uq, QK_HEAD).astype(BF16)
    uq_rot = jnp.pad(_rot_partner(uq[..., QK_NOPE:]), ((0, 0), (0, 0), (QK_NOPE, SLOT - QK_HEAD)))
    w_uq_r = uq_rot.reshape(Q_LORA, HEADS_W).astype(BF16)
    ukv = w_ukv.reshape(KV_LORA, N_HEADS, QK_NOPE + V_HEAD)
    w_uk_p = _slots(ukv[..., :QK_NOPE].reshape(KV_LORA, N_HEADS * QK_NOPE), QK_NOPE).astype(BF16)
    w_uv = ukv[..., QK_NOPE:].reshape(KV_LORA, ATTN_WIDTH).astype(BF16)
    w_ukt = jnp.transpose(ukv[..., :QK_NOPE], (2, 1, 0)).reshape(N_HEADS * QK_NOPE, KV_LORA).astype(BF16)
    pad_g = lambda g: jnp.pad(g, (0, SLOT - QK_HEAD))[None, :]
    gq = pad_g(q_norm_g) * ATTN_SCALE
    gk = pad_g(k_norm_g)
    mixer = (norm_mix_g[None, :], w_in_p, q_lora_g[None, :], w_uq_p, w_uq_r, kv_lora_g[None, :],
             w_uk_p, w_uv, gq, gk, conv_w, conv_b[None, :], conv_out_g[None, :])
    return mixer, w_ukt, w_uv


def _rope_slot_tables(pos):
    inv_freq = ROPE_THETA ** (-(jnp.arange(0, QK_ROPE, 2, dtype=F32) / QK_ROPE))
    ang = pos.astype(F32)[:, None] * inv_freq[None, :]
    n = pos.shape[0]
    cos2 = jnp.concatenate([jnp.cos(ang)] * 2, axis=1)
    sin2 = jnp.concatenate([jnp.sin(ang)] * 2, axis=1)
    cos_t = jnp.concatenate([jnp.ones((n, QK_NOPE), F32), cos2, jnp.zeros((n, SLOT - QK_HEAD), F32)], axis=1)
    sin_t = jnp.pad(sin2, ((0, 0), (QK_NOPE, SLOT - QK_HEAD)))
    return cos_t, sin_t


def kernel(x_prompt, x_sample, cache_kv_latent, cache_k_rope, state_conv, page_table, meta_tokens,
           norm_mix_g, w_in, q_lora_g, kv_lora_g, w_uq, w_ukv, q_norm_g, k_norm_g, conv_w, conv_b,
           attn_out_g, conv_out_g, w_o, norm_ffn_g, w_up, w_down):
    depth = w_in.shape[0]
    assert depth == 1, "the prompt and sample streams are chained for a single layer"
    nb, seq, _ = x_prompt.shape
    nseq, dec, _ = x_sample.shape
    past = page_table.shape[1] * PAGE_SIZE
    l = 0

    mixer_w, w_ukt, w_uv = _layer_weights(w_in[l], q_lora_g[l], kv_lora_g[l], w_uq[l], w_ukv[l],
                                          q_norm_g[l], k_norm_g[l], conv_w[l], conv_b[l],
                                          conv_out_g[l], norm_mix_g[l])
    ffn_w = (attn_out_g[l][None, :], w_o[l].astype(BF16), norm_ffn_g[l][None, :],
             w_up[l].astype(BF16), w_down[l].astype(BF16))

    ns = nseq * dec
    xs_rows = jnp.concatenate([x_sample.reshape(ns, D_MODEL), meta_tokens.astype(F32)], axis=0)
    pos_s = jnp.concatenate([jnp.tile(past + jnp.arange(dec), nseq), jnp.arange(N_META)])
    cos_s, sin_s = _rope_slot_tables(pos_s)
    t_in_seq = jnp.concatenate([jnp.tile(jnp.arange(dec), nseq), jnp.arange(N_META)])
    m1 = (t_in_seq >= 1).astype(F32)[:, None]
    m2 = (t_in_seq >= 2).astype(F32)[:, None]
    st = state_conv[l].astype(F32)
    zrow = jnp.zeros((nseq, 1, CONV_CH), F32)
    s1 = jnp.concatenate([st[:, 1:2], zrow, zrow, zrow], axis=1)[:, :dec]
    s2 = jnp.concatenate([st[:, 0:1], st[:, 1:2], zrow, zrow], axis=1)[:, :dec]
    zmeta = jnp.zeros((N_META, CONV_CH), F32)
    s1 = jnp.concatenate([s1.reshape(ns, CONV_CH), zmeta], axis=0)
    s2 = jnp.concatenate([s2.reshape(ns, CONV_CH), zmeta], axis=0)
    (qk_s, qabs_s, k_s, v_s, lat_s, kpe_s, convn_s, u_s) = _mixer_call(
        xs_rows, cos_s, sin_s, (m1, m2, s1, s2), mixer_w, sample_mode=True)

    pad_meta = lambda a: jnp.pad(a[ns:], ((0, LANES - N_META), (0, 0)))
    k_meta, v_meta = pad_meta(k_s), pad_meta(v_s)
    lat_meta, kpe_meta, u_meta = lat_s[ns:], kpe_s[ns:], u_s[ns:]

    cos_p, sin_p = _rope_slot_tables(N_META + jnp.arange(seq))
    q_p, k_p, v_p, lat_p, kpe_p, convn_p, utail_p = _mixer_call(
        x_prompt, cos_p, sin_p, u_meta[N_META - SUBLANES:], mixer_w, sample_mode=False)
    attn_p = _prompt_attn_call(q_p, k_p, v_p, k_meta, v_meta)
    y_prompt = _merge_ffn_call(x_prompt.reshape(nb * seq, D_MODEL), attn_p.reshape(nb * seq, ATTN_WIDTH),
                               convn_p.reshape(nb * seq, CONV_CH), ffn_w, ROW_TILE)
    y_prompt = y_prompt.reshape(nb, seq, D_MODEL)
    bcast = lambda a: jnp.broadcast_to(a[None], (nb,) + a.shape)
    new_lat_prompt = jnp.concatenate([bcast(lat_meta), lat_p], axis=1)[None]
    new_kpe_prompt = jnp.concatenate([bcast(kpe_meta), kpe_p], axis=1)[None]
    new_conv_prompt = utail_p[:, -1, SUBLANES - (CONV_W - 1):][None]

    nq = dec * N_HEADS
    qabs = qabs_s[:ns].reshape(nseq, nq, KV_LORA)
    qpe = qk_s[:ns].reshape(ns, N_HEADS, SLOT)[:, :, QK_NOPE:QK_HEAD].reshape(nseq, nq, QK_ROPE)
    pad_tail = lambda a: jnp.pad(a[:ns].reshape(nseq, dec, -1), ((0, 0), (0, TAIL_KEYS - dec), (0, 0)))
    attn_s = _sample_attn_call(page_table, qabs, qpe, w_ukt, w_uv, pad_tail(lat_s),
                               jnp.swapaxes(pad_tail(kpe_s), 1, 2),
                               cache_kv_latent[l], jnp.swapaxes(cache_k_rope[l], 1, 2))
    y_sample = _merge_ffn_call(x_sample.reshape(ns, D_MODEL), attn_s.reshape(ns, ATTN_WIDTH),
                               convn_s[:ns], ffn_w, ns)
    y_sample = y_sample.reshape(nseq, dec, D_MODEL)
    new_lat_sample = lat_s[:ns].reshape(nseq, dec, KV_LORA)[None]
    new_kpe_sample = kpe_s[:ns].reshape(nseq, dec, QK_ROPE)[None]
    us = jnp.concatenate([st, u_s[:ns].reshape(nseq, dec, CONV_CH)], axis=1)
    new_conv_sample = us[:, -(CONV_W - 1):][None]

    return (y_prompt, y_sample, new_lat_prompt, new_kpe_prompt, new_conv_prompt,
            new_lat_sample, new_kpe_sample, new_conv_sample)
```

```python
import functools

import jax
import jax.numpy as jnp
from jax import lax
from jax.experimental import pallas as pl
from jax.experimental.pallas import tpu as pltpu

D_MODEL = 1024
N_META = 16
N_HEADS = 8
QK_NOPE = 64
QK_ROPE = 32
V_HEAD = 64
QK_HEAD = QK_NOPE + QK_ROPE
Q_LORA = 384
KV_LORA = 256
ATTN_WIDTH = N_HEADS * V_HEAD
CONV_CH = D_MODEL - ATTN_WIDTH
CONV_W = 3
D_FF = 4 * D_MODEL
ROPE_THETA = 10000.0
EPS = 1e-6
PAGE_SIZE = 128
ATTN_SCALE = QK_HEAD ** -0.5

LANES = 128
SUBLANES = 8
SLOT = LANES
HEADS_W = N_HEADS * SLOT
HALF_ROPE = QK_ROPE // 2

C_CQ = 0
C_CKV = C_CQ + Q_LORA
C_GB = C_CKV + KV_LORA
C_GC = C_GB + CONV_CH
C_HC = C_GC + CONV_CH
C_KPE = C_HC + CONV_CH
C_KPR = C_KPE + SLOT
IN_W = C_KPR + SLOT

ROW_TILE = 512
ATT_TQ = 512
ATT_TK = 512
FF_CHUNK = 1024
PAGES_PER_STEP = 16
TAIL_KEYS = PAGE_SIZE
VMEM_LIMIT = 52 * 1024 * 1024

BF16 = jnp.bfloat16
F32 = jnp.float32

_NT = (((1,), (1,)), ((), ()))


def _dot(a, b):
    return jnp.dot(a, b, preferred_element_type=F32)


def _dot_nt(a, b):
    return lax.dot_general(a, b, _NT, preferred_element_type=F32)


def _rms(x, g):
    return x * lax.rsqrt(jnp.mean(x * x, axis=-1, keepdims=True) + EPS) * g


def _const_spec(shape):
    nd = len(shape)
    return pl.BlockSpec(shape, lambda *_: (0,) * nd, pipeline_mode=pl.Buffered(1))


def _mixer_kernel(*refs, rows, sample_mode):
    it = iter(refs)
    x_ref, cos_ref, sin_ref = next(it), next(it), next(it)
    if sample_mode:
        m1_ref, m2_ref, s1_ref, s2_ref = next(it), next(it), next(it), next(it)
    else:
        tail_in_ref = next(it)
    (g_mix_ref, w_in_ref, g_ql_ref, w_uq_ref, w_uqr_ref, g_kvl_ref, w_uk_ref, w_uv_ref,
     gq_ref, gk_ref, cw_ref, cb_ref, g_conv_ref) = (next(it) for _ in range(13))
    if sample_mode:
        (qk_ref, qabs_ref, k_ref, v_ref, lat_ref, kpe_ref, convn_ref, u_ref) = (next(it) for _ in range(8))
    else:
        (q_ref, k_ref, v_ref, lat_ref, kpe_ref, convn_ref, utail_ref) = (next(it) for _ in range(7))
    ubuf = next(it)

    hn = _rms(x_ref[...], g_mix_ref[...]).astype(BF16)
    cos = cos_ref[...]
    sin = sin_ref[...]

    cq = _dot(hn, w_in_ref[:, C_CQ:C_CKV])
    cqn = _rms(cq, g_ql_ref[...]).astype(BF16)
    gq = gq_ref[...]
    gk = gk_ref[...]
    for h in range(N_HEADS):
        sl = slice(h * SLOT, (h + 1) * SLOT)
        qh = _dot(cqn, w_uq_ref[:, sl]) * cos + _dot(cqn, w_uqr_ref[:, sl]) * sin
        ss = jnp.sum(qh * qh, axis=-1, keepdims=True)
        qh = qh * lax.rsqrt(ss * (1.0 / QK_HEAD) + EPS) * gq
        if sample_mode:
            qkh = (qh * gk).astype(BF16)
            qk_ref[:, sl] = qkh
            qabs_ref[:, h * KV_LORA:(h + 1) * KV_LORA] = _dot_nt(qkh, w_uk_ref[:, sl]).astype(BF16)
        else:
            q_ref[:, sl] = qh.astype(BF16)

    ckv = _dot(hn, w_in_ref[:, C_CKV:C_GB])
    lat = _rms(ckv, g_kvl_ref[...])
    lat_ref[...] = lat
    lat_b = lat.astype(BF16)
    zk = _dot(hn, w_in_ref[:, C_KPE:IN_W])
    krot = zk[:, :SLOT] * cos + zk[:, SLOT:] * sin
    kpe_ref[...] = krot[:, QK_NOPE:QK_HEAD]
    ss_rot = jnp.sum(krot * krot, axis=-1, keepdims=True)
    for h in range(N_HEADS):
        sl = slice(h * SLOT, (h + 1) * SLOT)
        kn = _dot(lat_b, w_uk_ref[:, sl])
        ss = jnp.sum(kn * kn, axis=-1, keepdims=True) + ss_rot
        kh = (kn + krot) * lax.rsqrt(ss * (1.0 / QK_HEAD) + EPS) * gk
        k_ref[:, sl] = kh.astype(BF16)
    v_ref[...] = _dot(lat_b, w_uv_ref[...]).astype(BF16)

    gc = _dot(hn, w_in_ref[:, C_GC:C_HC])
    hc = _dot(hn, w_in_ref[:, C_HC:C_KPE])
    u = gc * hc
    if sample_mode:
        ubuf[0:SUBLANES, :] = jnp.zeros((SUBLANES, CONV_CH), F32)
    else:
        t = pl.program_id(1)

        @pl.when(t == 0)
        def _():
            ubuf[0:SUBLANES, :] = tail_in_ref[...]

        @pl.when(t != 0)
        def _():
            ubuf[0:SUBLANES, :] = ubuf[rows:rows + SUBLANES, :]
    ubuf[SUBLANES:SUBLANES + rows, :] = u
    u1 = ubuf[SUBLANES - 1:SUBLANES - 1 + rows, :]
    u2 = ubuf[SUBLANES - 2:SUBLANES - 2 + rows, :]
    if sample_mode:
        u1 = u1 * m1_ref[...] + s1_ref[...]
        u2 = u2 * m2_ref[...] + s2_ref[...]
        u_ref[...] = u
    else:
        utail_ref[...] = u[rows - SUBLANES:, :]
    y = cb_ref[...] + u2 * cw_ref[0:1, :] + u1 * cw_ref[1:2, :] + u * cw_ref[2:3, :]
    gb = _dot(hn, w_in_ref[:, C_GB:C_GC])
    convn_ref[...] = _rms(gb * y, g_conv_ref[...]).astype(BF16)


def _mixer_call(x, cos_t, sin_t, conv_in, weights, *, sample_mode):
    (g_mix, w_in_p, g_ql, w_uq_p, w_uq_r, g_kvl, w_uk_p, w_uv, gq, gk, cw, cb, g_conv) = weights
    wspecs = [_const_spec(w.shape) for w in weights]
    if sample_mode:
        rows = x.shape[0]
        grid = (1,)
        row = lambda w: pl.BlockSpec((rows, w), lambda i: (0, 0))
        m1, m2, s1, s2 = conv_in
        in_specs = [row(D_MODEL), row(SLOT), row(SLOT), row(1), row(1), row(CONV_CH), row(CONV_CH)]
        args = [x, cos_t, sin_t, m1, m2, s1, s2]
        widths = [(HEADS_W, BF16), (N_HEADS * KV_LORA, BF16), (HEADS_W, BF16), (ATTN_WIDTH, BF16),
                  (KV_LORA, F32), (QK_ROPE, F32), (CONV_CH, BF16), (CONV_CH, F32)]
        out_shape = [jax.ShapeDtypeStruct((rows, w), d) for w, d in widths]
        out_specs = [row(w) for w, _ in widths]
        sem = ("arbitrary",)
    else:
        nb, seq, _ = x.shape
        rows = ROW_TILE
        nt = seq // rows
        grid = (nb, nt)
        row3 = lambda w: pl.BlockSpec((None, rows, w), lambda b, t: (b, t, 0))
        tab = pl.BlockSpec((rows, SLOT), lambda b, t: (t, 0))
        in_specs = [row3(D_MODEL), tab, tab, _const_spec(conv_in.shape)]
        args = [x, cos_t, sin_t, conv_in]
        widths = [(HEADS_W, BF16), (HEADS_W, BF16), (ATTN_WIDTH, BF16), (KV_LORA, F32),
                  (QK_ROPE, F32), (CONV_CH, BF16)]
        out_shape = [jax.ShapeDtypeStruct((nb, seq, w), d) for w, d in widths]
        out_shape.append(jax.ShapeDtypeStruct((nb, nt, SUBLANES, CONV_CH), F32))
        out_specs = [row3(w) for w, _ in widths]
        out_specs.append(pl.BlockSpec((None, None, SUBLANES, CONV_CH), lambda b, t: (b, t, 0, 0)))
        sem = ("arbitrary", "arbitrary")
    return pl.pallas_call(
        functools.partial(_mixer_kernel, rows=rows, sample_mode=sample_mode),
        grid=grid,
        in_specs=in_specs + wspecs,
        out_specs=out_specs,
        out_shape=out_shape,
        scratch_shapes=[pltpu.VMEM((rows + 2 * SUBLANES, CONV_CH), F32)],
        compiler_params=pltpu.CompilerParams(dimension_semantics=sem, vmem_limit_bytes=VMEM_LIMIT),
        name="mixer_sample" if sample_mode else "mixer_prompt",
    )(*args, *weights)


def _softmax_update(s, m_ref, l_ref, acc_ref, pv):
    n = s.shape[1] // LANES
    m_old = m_ref[...]
    m_new = jnp.maximum(m_old, jnp.max(s, axis=-1, keepdims=True))
    alpha = jnp.exp(m_old - m_new)
    ps = [jnp.exp(s[:, c * LANES:(c + 1) * LANES] - m_new) for c in range(n)]
    l_ref[...] = alpha * l_ref[...] + functools.reduce(lambda a, b: a + b, ps)
    p = ps[0] if n == 1 else jnp.concatenate(ps, axis=1)
    w = acc_ref.shape[-1] // LANES
    alpha_w = alpha if w == 1 else jnp.concatenate([alpha] * w, axis=1)
    acc_ref[...] = alpha_w * acc_ref[...] + pv(p.astype(BF16))
    m_ref[...] = m_new


def _prompt_attn_kernel(q_ref, k_ref, v_ref, km_ref, vm_ref, o_ref, m_sc, l_sc, acc_sc):
    qi = pl.program_id(2)
    lane = lax.broadcasted_iota(jnp.int32, (ATT_TQ, SLOT), 1)
    m_sc[...] = jnp.full(m_sc.shape, -jnp.inf, F32)
    l_sc[...] = jnp.zeros(l_sc.shape, F32)
    acc_sc[...] = jnp.zeros(acc_sc.shape, F32)

    def step(hh, k_blk, v_blk, mask):
        sl = slice(hh * SLOT, (hh + 1) * SLOT)
        s = _dot_nt(q_ref[:, sl], k_blk[:, sl])
        if mask is not None:
            s = jnp.where(mask, s, -jnp.inf)
        _softmax_update(s, m_sc.at[hh], l_sc.at[hh], acc_sc.at[hh], lambda p: _dot(p, v_blk))

    meta = lane < N_META
    for hh in range(2):
        step(hh, km_ref[...], vm_ref[...], meta)

    def block(ki, mask):
        ks = pl.multiple_of(ki * ATT_TK, ATT_TK)
        k_blk = k_ref[pl.ds(ks, ATT_TK), :]
        v_blk = v_ref[pl.ds(ks, ATT_TK), :]
        for hh in range(2):
            step(hh, k_blk, v_blk, mask)

    def body(ki, carry):
        block(ki, None)
        return carry

    lax.fori_loop(0, qi, body, 0)
    block(qi, lax.broadcasted_iota(jnp.int32, (ATT_TQ, ATT_TK), 1)
          <= lax.broadcasted_iota(jnp.int32, (ATT_TQ, ATT_TK), 0))

    outs = [acc_sc[hh] / jnp.sum(l_sc[hh], axis=-1, keepdims=True) for hh in range(2)]
    o_ref[...] = jnp.where(lane < V_HEAD, outs[0], outs[1]).astype(o_ref.dtype)


def _prompt_attn_call(q, k, v, k_meta, v_meta):
    nb, seq, _ = q.shape
    grid = (nb, N_HEADS // 2, seq // ATT_TQ)
    return pl.pallas_call(
        _prompt_attn_kernel,
        grid=grid,
        in_specs=[
            pl.BlockSpec((None, ATT_TQ, 2 * SLOT), lambda b, p, i: (b, i, p)),
            pl.BlockSpec((None, seq, 2 * SLOT), lambda b, p, i: (b, 0, p)),
            pl.BlockSpec((None, seq, SLOT), lambda b, p, i: (b, 0, p)),
            pl.BlockSpec((LANES, 2 * SLOT), lambda b, p, i: (0, p)),
            pl.BlockSpec((LANES, SLOT), lambda b, p, i: (0, p)),
        ],
        out_specs=pl.BlockSpec((None, ATT_TQ, SLOT), lambda b, p, i: (b, i, p)),
        out_shape=jax.ShapeDtypeStruct((nb, seq, ATTN_WIDTH), BF16),
        scratch_shapes=[pltpu.VMEM((2, ATT_TQ, LANES), F32)] * 3,
        compiler_params=pltpu.CompilerParams(
            dimension_semantics=("arbitrary", "arbitrary", "arbitrary"),
            vmem_limit_bytes=VMEM_LIMIT),
        name="prompt_attn",
    )(q, k, v, k_meta, v_meta)


def _sample_attn_kernel(pt_ref, qabs_ref, qpe_ref, wukt_ref, wuv_ref, tlat_ref, tkpe_ref,
                        lat_hbm, kpe_hbm, o_ref,
                        m_sc, l_sc, acc_sc, latb_sc, lhs_sc, lat_buf, kpe_buf, sems):
    npg = PAGES_PER_STEP
    b = pl.program_id(0)
    c = pl.program_id(1)
    nchunk = pl.num_programs(1)
    g = b * nchunk + c
    last = pl.num_programs(0) * nchunk - 1
    slot = g % 2
    nq = qabs_ref.shape[0]
    reps = nq // N_HEADS
    nkn = N_HEADS * QK_NOPE
    qpe = qpe_ref[...]

    def chunk_copies(gg, sl):
        bb = gg // nchunk
        cc = gg % nchunk
        copies = []
        for j in range(npg):
            page = pt_ref[bb, cc * npg + j]
            copies.append(pltpu.make_async_copy(lat_hbm.at[page], lat_buf.at[sl, j], sems.at[0, sl]))
            copies.append(pltpu.make_async_copy(kpe_hbm.at[page], kpe_buf.at[sl, j], sems.at[1, sl]))
        return copies

    @pl.when(g == 0)
    def _():
        lhs_sc[0:nkn, :] = wukt_ref[...]
        for cp in chunk_copies(0, 0):
            cp.start()

    for cp in chunk_copies(jnp.minimum(g + 1, last), 1 - slot):
        cp.start()

    @pl.when(c == 0)
    def _():
        lhs_sc[nkn:nkn + nq, :] = qabs_ref[...]
        m_sc[...] = jnp.full(m_sc.shape, -jnp.inf, F32)
        l_sc[...] = jnp.zeros(l_sc.shape, F32)
        acc_sc[...] = jnp.zeros(acc_sc.shape, F32)

    def scores(lat_b, kpe_t):
        nk = lat_b.shape[0]
        full = _dot_nt(lhs_sc[...], lat_b)
        knt = full[:nkn]
        nsum = jnp.sum((knt * knt).reshape(QK_NOPE, N_HEADS, nk), axis=0)
        rsum = jnp.sum(kpe_t * kpe_t, axis=0, keepdims=True)
        r = lax.rsqrt((nsum + rsum) * (1.0 / QK_HEAD) + EPS)
        s = full[nkn:] + _dot(qpe, kpe_t.astype(BF16))
        return s * jnp.concatenate([r] * reps, axis=0)

    @pl.when(c == 0)
    def _():
        lat_b = tlat_ref[...].astype(BF16)
        s = scores(lat_b, tkpe_ref[...])
        key = lax.broadcasted_iota(jnp.int32, s.shape, 1)
        qt = lax.broadcasted_iota(jnp.int32, s.shape, 0) // N_HEADS
        _softmax_update(jnp.where(key <= qt, s, -jnp.inf), m_sc, l_sc, acc_sc,
                        lambda p: _dot(p, lat_b))

    for cp in chunk_copies(g, slot):
        cp.wait()
    for j in range(npg):
        latb_sc[j * PAGE_SIZE:(j + 1) * PAGE_SIZE, :] = lat_buf[slot, j].astype(BF16)
    parts = []
    for j in range(0, npg, 2):
        kpe_t = jnp.concatenate([kpe_buf[slot, j], kpe_buf[slot, j + 1]], axis=1)
        parts.append(scores(latb_sc[j * PAGE_SIZE:(j + 2) * PAGE_SIZE, :], kpe_t))
    _softmax_update(jnp.concatenate(parts, axis=1), m_sc, l_sc, acc_sc,
                    lambda p: _dot(p, latb_sc[...]))

    @pl.when(c == nchunk - 1)
    def _():
        o_lat = (acc_sc[...] / jnp.sum(l_sc[...], axis=-1, keepdims=True)).astype(BF16)
        full = _dot(o_lat, wuv_ref[...])
        row_h = lax.broadcasted_iota(jnp.int32, full.shape, 0) % N_HEADS
        col_h = lax.broadcasted_iota(jnp.int32, full.shape, 1) // V_HEAD
        own = jnp.where(row_h == col_h, full, 0.0)
        o_ref[...] = jnp.sum(own.reshape(reps, N_HEADS, ATTN_WIDTH), axis=1)

    @pl.when(g == last)
    def _():
        for cp in chunk_copies(last, 1 - slot):
            cp.wait()


def _sample_attn_call(page_table, qabs, qpe, w_ukt, w_uv, tail_lat, tail_kpe_t, cache_lat, cache_kpe_t):
    nseq, n_pages = page_table.shape
    nq = qabs.shape[1]
    npg = PAGES_PER_STEP
    grid = (nseq, n_pages // npg)

    per_seq = lambda rows, width: pl.BlockSpec((None, rows, width), lambda b, c, pt: (b, 0, 0))
    whole = lambda shape: pl.BlockSpec(shape, lambda b, c, pt: (0,) * len(shape),
                                       pipeline_mode=pl.Buffered(1))
    in_specs = [per_seq(nq, KV_LORA), per_seq(nq, QK_ROPE), whole(w_ukt.shape), whole(w_uv.shape),
                per_seq(TAIL_KEYS, KV_LORA), per_seq(QK_ROPE, TAIL_KEYS),
                pl.BlockSpec(memory_space=pl.ANY), pl.BlockSpec(memory_space=pl.ANY)]
    reps = nq // N_HEADS
    return pl.pallas_call(
        _sample_attn_kernel,
        grid_spec=pltpu.PrefetchScalarGridSpec(
            num_scalar_prefetch=1,
            grid=grid,
            in_specs=in_specs,
            out_specs=pl.BlockSpec((None, reps, ATTN_WIDTH), lambda b, c, pt: (b, 0, 0)),
            scratch_shapes=[pltpu.VMEM((nq, LANES), F32), pltpu.VMEM((nq, LANES), F32),
                            pltpu.VMEM((nq, KV_LORA), F32),
                            pltpu.VMEM((npg * PAGE_SIZE, KV_LORA), BF16),
                            pltpu.VMEM((N_HEADS * QK_NOPE + nq, KV_LORA), BF16),
                            pltpu.VMEM((2, npg, PAGE_SIZE, KV_LORA), F32),
                            pltpu.VMEM((2, npg, QK_ROPE, PAGE_SIZE), F32),
                            pltpu.SemaphoreType.DMA((2, 2))],
        ),
        out_shape=jax.ShapeDtypeStruct((nseq, reps, ATTN_WIDTH), F32),
        compiler_params=pltpu.CompilerParams(dimension_semantics=("arbitrary", "arbitrary"),
                                             vmem_limit_bytes=VMEM_LIMIT),
        name="sample_attn",
    )(page_table, qabs, qpe, w_ukt, w_uv, tail_lat, tail_kpe_t, cache_lat, cache_kpe_t)


def _merge_ffn_kernel(x_ref, attn_ref, convn_ref, g_attn_ref, w_o_ref, g_ffn_ref, w_up_ref,
                      w_down_ref, y_ref):
    an = _rms(attn_ref[...].astype(F32), g_attn_ref[...]).astype(BF16)
    x1 = x_ref[...] + (_dot(an, w_o_ref[0:ATTN_WIDTH, :]) + _dot(convn_ref[...], w_o_ref[ATTN_WIDTH:, :]))
    hf = _rms(x1, g_ffn_ref[...]).astype(BF16)
    ffn = None
    for c in range(D_FF // FF_CHUNK):
        cs = slice(c * FF_CHUNK, (c + 1) * FF_CHUNK)
        up = jnp.maximum(_dot(hf, w_up_ref[:, cs]), 0.0)
        part = _dot((up * up).astype(BF16), w_down_ref[cs, :])
        ffn = part if ffn is None else ffn + part
    y_ref[...] = x1 + ffn


def _merge_ffn_call(x, attn, convn, weights, rows):
    n = x.shape[0]
    row = lambda w: pl.BlockSpec((rows, w), lambda i: (i, 0))
    return pl.pallas_call(
        _merge_ffn_kernel,
        grid=(n // rows,),
        in_specs=[row(D_MODEL), row(ATTN_WIDTH), row(CONV_CH)] + [_const_spec(w.shape) for w in weights],
        out_specs=row(D_MODEL),
        out_shape=jax.ShapeDtypeStruct((n, D_MODEL), F32),
        compiler_params=pltpu.CompilerParams(dimension_semantics=("arbitrary",),
                                             vmem_limit_bytes=VMEM_LIMIT),
        name="merge_ffn",
    )(x, attn, convn, *weights)


def _slots(w, width):
    k = w.shape[0]
    w = w.reshape(k, N_HEADS, width)
    return jnp.pad(w, ((0, 0), (0, 0), (0, SLOT - width))).reshape(k, HEADS_W)


def _rot_partner(w):
    return jnp.concatenate([-w[..., HALF_ROPE:], w[..., :HALF_ROPE]], axis=-1)


def _rope_slot(w):
    return jnp.pad(w, ((0, 0), (QK_NOPE, SLOT - QK_HEAD)))


def _layer_weights(w_in, q_lora_g, kv_lora_g, w_uq, w_ukv, q_norm_g, k_norm_g, conv_w, conv_b,
                   conv_out_g, norm_mix_g):
    o1 = Q_LORA
    o2 = o1 + KV_LORA
    o3 = o2 + QK_ROPE
    w_kpe = w_in[:, o2:o3]
    w_in_p = jnp.concatenate(
        [w_in[:, :o2], w_in[:, o3:], _rope_slot(w_kpe), _rope_slot(_rot_partner(w_kpe))],
        axis=1).astype(BF16)
    uq = w_uq.reshape(Q_LORA, N_HEADS, QK_HEAD)
    w_uq_p = _slots(w_uq, QK_HEAD).astype(BF16)
    uq_rot = jnp.pad(_rot_partner(uq[..., QK_NOPE:]), ((0, 0), (0, 0), (QK_NOPE, SLOT - QK_HEAD)))
    w_uq_r = uq_rot.reshape(Q_LORA, HEADS_W).astype(BF16)
    ukv = w_ukv.reshape(KV_LORA, N_HEADS, QK_NOPE + V_HEAD)
    w_uk_p = _slots(ukv[..., :QK_NOPE].reshape(KV_LORA, N_HEADS * QK_NOPE), QK_NOPE).astype(BF16)
    w_uv = ukv[..., QK_NOPE:].reshape(KV_LORA, ATTN_WIDTH).astype(BF16)
    w_ukt = jnp.transpose(ukv[..., :QK_NOPE], (2, 1, 0)).reshape(N_HEADS * QK_NOPE, KV_LORA).astype(BF16)
    pad_g = lambda g: jnp.pad(g, (0, SLOT - QK_HEAD))[None, :]
    gq = pad_g(q_norm_g) * ATTN_SCALE
    gk = pad_g(k_norm_g)
    mixer = (norm_mix_g[None, :], w_in_p, q_lora_g[None, :], w_uq_p, w_uq_r, kv_lora_g[None, :],
             w_uk_p, w_uv, gq, gk, conv_w, conv_b[None, :], conv_out_g[None, :])
    return mixer, w_ukt, w_uv


def _rope_slot_tables(pos):
    inv_freq = ROPE_THETA ** (-(jnp.arange(0, QK_ROPE, 2, dtype=F32) / QK_ROPE))
    ang = pos.astype(F32)[:, None] * inv_freq[None, :]
    n = pos.shape[0]
    cos2 = jnp.concatenate([jnp.cos(ang)] * 2, axis=1)
    sin2 = jnp.concatenate([jnp.sin(ang)] * 2, axis=1)
    cos_t = jnp.concatenate([jnp.ones((n, QK_NOPE), F32), cos2, jnp.zeros((n, SLOT - QK_HEAD), F32)], axis=1)
    sin_t = jnp.pad(sin2, ((0, 0), (QK_NOPE, SLOT - QK_HEAD)))
    return cos_t, sin_t


def kernel(x_prompt, x_sample, cache_kv_latent, cache_k_rope, state_conv, page_table, meta_tokens,
           norm_mix_g, w_in, q_lora_g, kv_lora_g, w_uq, w_ukv, q_norm_g, k_norm_g, conv_w, conv_b,
           attn_out_g, conv_out_g, w_o, norm_ffn_g, w_up, w_down):
    depth = w_in.shape[0]
    assert depth == 1, "the prompt and sample streams are chained for a single layer"
    nb, seq, _ = x_prompt.shape
    nseq, dec, _ = x_sample.shape
    past = page_table.shape[1] * PAGE_SIZE
    l = 0

    mixer_w, w_ukt, w_uv = _layer_weights(w_in[l], q_lora_g[l], kv_lora_g[l], w_uq[l], w_ukv[l],
                                          q_norm_g[l], k_norm_g[l], conv_w[l], conv_b[l],
                                          conv_out_g[l], norm_mix_g[l])
    ffn_w = (attn_out_g[l][None, :], w_o[l].astype(BF16), norm_ffn_g[l][None, :],
             w_up[l].astype(BF16), w_down[l].astype(BF16))

    ns = nseq * dec
    xs_rows = jnp.concatenate([x_sample.reshape(ns, D_MODEL), meta_tokens.astype(F32)], axis=0)
    pos_s = jnp.concatenate([jnp.tile(past + jnp.arange(dec), nseq), jnp.arange(N_META)])
    cos_s, sin_s = _rope_slot_tables(pos_s)
    t_in_seq = jnp.concatenate([jnp.tile(jnp.arange(dec), nseq), jnp.arange(N_META)])
    m1 = (t_in_seq >= 1).astype(F32)[:, None]
    m2 = (t_in_seq >= 2).astype(F32)[:, None]
    st = state_conv[l].astype(F32)
    zrow = jnp.zeros((nseq, 1, CONV_CH), F32)
    s1 = jnp.concatenate([st[:, 1:2], zrow, zrow, zrow], axis=1)[:, :dec]
    s2 = jnp.concatenate([st[:, 0:1], st[:, 1:2], zrow, zrow], axis=1)[:, :dec]
    zmeta = jnp.zeros((N_META, CONV_CH), F32)
    s1 = jnp.concatenate([s1.reshape(ns, CONV_CH), zmeta], axis=0)
    s2 = jnp.concatenate([s2.reshape(ns, CONV_CH), zmeta], axis=0)
    (qk_s, qabs_s, k_s, v_s, lat_s, kpe_s, convn_s, u_s) = _mixer_call(
        xs_rows, cos_s, sin_s, (m1, m2, s1, s2), mixer_w, sample_mode=True)

    pad_meta = lambda a: jnp.pad(a[ns:], ((0, LANES - N_META), (0, 0)))
    k_meta, v_meta = pad_meta(k_s), pad_meta(v_s)
    lat_meta, kpe_meta, u_meta = lat_s[ns:], kpe_s[ns:], u_s[ns:]

    cos_p, sin_p = _rope_slot_tables(N_META + jnp.arange(seq))
    q_p, k_p, v_p, lat_p, kpe_p, convn_p, utail_p = _mixer_call(
        x_prompt, cos_p, sin_p, u_meta[N_META - SUBLANES:], mixer_w, sample_mode=False)
    attn_p = _prompt_attn_call(q_p, k_p, v_p, k_meta, v_meta)
    y_prompt = _merge_ffn_call(x_prompt.reshape(nb * seq, D_MODEL), attn_p.reshape(nb * seq, ATTN_WIDTH),
                               convn_p.reshape(nb * seq, CONV_CH), ffn_w, ROW_TILE)
    y_prompt = y_prompt.reshape(nb, seq, D_MODEL)
    bcast = lambda a: jnp.broadcast_to(a[None], (nb,) + a.shape)
    new_lat_prompt = jnp.concatenate([bcast(lat_meta), lat_p], axis=1)[None]
    new_kpe_prompt = jnp.concatenate([bcast(kpe_meta), kpe_p], axis=1)[None]
    new_conv_prompt = utail_p[:, -1, SUBLANES - (CONV_W - 1):][None]

    nq = dec * N_HEADS
    qabs = qabs_s[:ns].reshape(nseq, nq, KV_LORA)
    qpe = qk_s[:ns].reshape(ns, N_HEADS, SLOT)[:, :, QK_NOPE:QK_HEAD].reshape(nseq, nq, QK_ROPE)
    pad_tail = lambda a: jnp.pad(a[:ns].reshape(nseq, dec, -1), ((0, 0), (0, TAIL_KEYS - dec), (0, 0)))
    attn_s = _sample_attn_call(page_table, qabs, qpe, w_ukt, w_uv, pad_tail(lat_s),
                               jnp.swapaxes(pad_tail(kpe_s), 1, 2),
                               cache_kv_latent[l], jnp.swapaxes(cache_k_rope[l], 1, 2))
    y_sample = _merge_ffn_call(x_sample.reshape(ns, D_MODEL), attn_s.reshape(ns, ATTN_WIDTH),
                               convn_s[:ns], ffn_w, ns)
    y_sample = y_sample.reshape(nseq, dec, D_MODEL)
    new_lat_sample = lat_s[:ns].reshape(nseq, dec, KV_LORA)[None]
    new_kpe_sample = kpe_s[:ns].reshape(nseq, dec, QK_ROPE)[None]
    us = jnp.concatenate([st, u_s[:ns].reshape(nseq, dec, CONV_CH)], axis=1)
    new_conv_sample = us[:, -(CONV_W - 1):][None]

    return (y_prompt, y_sample, new_lat_prompt, new_kpe_prompt, new_conv_prompt,
            new_lat_sample, new_kpe_sample, new_conv_sample)
```

```python
import functools

import jax
import jax.numpy as jnp
from jax import lax
from jax.experimental import pallas as pl
from jax.experimental.pallas import tpu as pltpu

D_MODEL = 1024
N_META = 16
N_HEADS = 8
QK_NOPE = 64
QK_ROPE = 32
V_HEAD = 64
QK_HEAD = QK_NOPE + QK_ROPE
Q_LORA = 384
KV_LORA = 256
ATTN_WIDTH = N_HEADS * V_HEAD
CONV_CH = D_MODEL - ATTN_WIDTH
CONV_W = 3
D_FF = 4 * D_MODEL
ROPE_THETA = 10000.0
EPS = 1e-6
PAGE_SIZE = 128
ATTN_SCALE = QK_HEAD ** -0.5

LANES = 128
SUBLANES = 8
SLOT = LANES
HEADS_W = N_HEADS * SLOT
HALF_ROPE = QK_ROPE // 2

C_CQ = 0
C_CKV = C_CQ + Q_LORA
C_GB = C_CKV + KV_LORA
C_GC = C_GB + CONV_CH
C_HC = C_GC + CONV_CH
C_KPE = C_HC + CONV_CH
C_KPR = C_KPE + SLOT
IN_W = C_KPR + SLOT

ROW_TILE = 512
ATT_TQ = 512
ATT_TK = 512
FF_CHUNK = 1024
PAGES_PER_CHUNK = 16
TAIL_KEYS = PAGE_SIZE
VMEM_LIMIT = 52 * 1024 * 1024

BF16 = jnp.bfloat16
F32 = jnp.float32

_NT = (((1,), (1,)), ((), ()))


def _dot(a, b):
    return jnp.dot(a, b, preferred_element_type=F32)


def _dot_nt(a, b):
    return lax.dot_general(a, b, _NT, preferred_element_type=F32)


def _rms(x, g):
    return x * lax.rsqrt(jnp.mean(x * x, axis=-1, keepdims=True) + EPS) * g


def _const_spec(shape):
    nd = len(shape)
    return pl.BlockSpec(shape, lambda *_: (0,) * nd, pipeline_mode=pl.Buffered(1))


def _mixer_kernel(*refs, rows, sample_mode):
    it = iter(refs)
    x_ref, cos_ref, sin_ref = next(it), next(it), next(it)
    if sample_mode:
        m1_ref, m2_ref, s1_ref, s2_ref = next(it), next(it), next(it), next(it)
    else:
        tail_in_ref = next(it)
    (g_mix_ref, w_in_ref, g_ql_ref, w_uq_ref, w_uqr_ref, g_kvl_ref, w_uk_ref, w_uv_ref,
     gq_ref, gk_ref, cw_ref, cb_ref, g_conv_ref) = (next(it) for _ in range(13))
    if sample_mode:
        (qk_ref, qabs_ref, k_ref, v_ref, lat_ref, kpe_ref, convn_ref, u_ref) = (next(it) for _ in range(8))
    else:
        (q_ref, k_ref, v_ref, lat_ref, kpe_ref, convn_ref, utail_ref) = (next(it) for _ in range(7))
    ubuf = next(it)

    hn = _rms(x_ref[...], g_mix_ref[...]).astype(BF16)
    cos = cos_ref[...]
    sin = sin_ref[...]

    cq = _dot(hn, w_in_ref[:, C_CQ:C_CKV])
    cqn = _rms(cq, g_ql_ref[...]).astype(BF16)
    gq = gq_ref[...]
    gk = gk_ref[...]
    for h in range(N_HEADS):
        sl = slice(h * SLOT, (h + 1) * SLOT)
        qh = _dot(cqn, w_uq_ref[:, sl]) * cos + _dot(cqn, w_uqr_ref[:, sl]) * sin
        ss = jnp.sum(qh * qh, axis=-1, keepdims=True)
        qh = qh * lax.rsqrt(ss * (1.0 / QK_HEAD) + EPS) * gq
        if sample_mode:
            qkh = (qh * gk).astype(BF16)
            qk_ref[:, sl] = qkh
            qabs_ref[:, h * KV_LORA:(h + 1) * KV_LORA] = _dot_nt(qkh, w_uk_ref[:, sl]).astype(BF16)
        else:
            q_ref[:, sl] = qh.astype(BF16)

    ckv = _dot(hn, w_in_ref[:, C_CKV:C_GB])
    lat = _rms(ckv, g_kvl_ref[...])
    lat_ref[...] = lat
    lat_b = lat.astype(BF16)
    zk = _dot(hn, w_in_ref[:, C_KPE:IN_W])
    krot = zk[:, :SLOT] * cos + zk[:, SLOT:] * sin
    kpe_ref[...] = krot[:, QK_NOPE:QK_HEAD]
    ss_rot = jnp.sum(krot * krot, axis=-1, keepdims=True)
    for h in range(N_HEADS):
        sl = slice(h * SLOT, (h + 1) * SLOT)
        kn = _dot(lat_b, w_uk_ref[:, sl])
        ss = jnp.sum(kn * kn, axis=-1, keepdims=True) + ss_rot
        kh = (kn + krot) * lax.rsqrt(ss * (1.0 / QK_HEAD) + EPS) * gk
        k_ref[:, sl] = kh.astype(BF16)
    v_ref[...] = _dot(lat_b, w_uv_ref[...]).astype(BF16)

    gc = _dot(hn, w_in_ref[:, C_GC:C_HC])
    hc = _dot(hn, w_in_ref[:, C_HC:C_KPE])
    u = gc * hc
    if sample_mode:
        ubuf[0:SUBLANES, :] = jnp.zeros((SUBLANES, CONV_CH), F32)
    else:
        t = pl.program_id(1)

        @pl.when(t == 0)
        def _():
            ubuf[0:SUBLANES, :] = tail_in_ref[...]

        @pl.when(t != 0)
        def _():
            ubuf[0:SUBLANES, :] = ubuf[rows:rows + SUBLANES, :]
    ubuf[SUBLANES:SUBLANES + rows, :] = u
    u1 = ubuf[SUBLANES - 1:SUBLANES - 1 + rows, :]
    u2 = ubuf[SUBLANES - 2:SUBLANES - 2 + rows, :]
    if sample_mode:
        u1 = u1 * m1_ref[...] + s1_ref[...]
        u2 = u2 * m2_ref[...] + s2_ref[...]
        u_ref[...] = u
    else:
        utail_ref[...] = u[rows - SUBLANES:, :]
    y = cb_ref[...] + u2 * cw_ref[0:1, :] + u1 * cw_ref[1:2, :] + u * cw_ref[2:3, :]
    gb = _dot(hn, w_in_ref[:, C_GB:C_GC])
    convn_ref[...] = _rms(gb * y, g_conv_ref[...]).astype(BF16)


def _mixer_call(x, cos_t, sin_t, conv_in, weights, *, sample_mode):
    (g_mix, w_in_p, g_ql, w_uq_p, w_uq_r, g_kvl, w_uk_p, w_uv, gq, gk, cw, cb, g_conv) = weights
    wspecs = [_const_spec(w.shape) for w in weights]
    if sample_mode:
        rows = x.shape[0]
        grid = (1,)
        row = lambda w: pl.BlockSpec((rows, w), lambda i: (0, 0))
        m1, m2, s1, s2 = conv_in
        in_specs = [row(D_MODEL), row(SLOT), row(SLOT), row(1), row(1), row(CONV_CH), row(CONV_CH)]
        args = [x, cos_t, sin_t, m1, m2, s1, s2]
        widths = [(HEADS_W, BF16), (N_HEADS * KV_LORA, BF16), (HEADS_W, BF16), (ATTN_WIDTH, BF16),
                  (KV_LORA, F32), (QK_ROPE, F32), (CONV_CH, BF16), (CONV_CH, F32)]
        out_shape = [jax.ShapeDtypeStruct((rows, w), d) for w, d in widths]
        out_specs = [row(w) for w, _ in widths]
        sem = ("arbitrary",)
    else:
        nb, seq, _ = x.shape
        rows = ROW_TILE
        nt = seq // rows
        grid = (nb, nt)
        row3 = lambda w: pl.BlockSpec((None, rows, w), lambda b, t: (b, t, 0))
        tab = pl.BlockSpec((rows, SLOT), lambda b, t: (t, 0))
        in_specs = [row3(D_MODEL), tab, tab, _const_spec(conv_in.shape)]
        args = [x, cos_t, sin_t, conv_in]
        widths = [(HEADS_W, BF16), (HEADS_W, BF16), (ATTN_WIDTH, BF16), (KV_LORA, F32),
                  (QK_ROPE, F32), (CONV_CH, BF16)]
        out_shape = [jax.ShapeDtypeStruct((nb, seq, w), d) for w, d in widths]
        out_shape.append(jax.ShapeDtypeStruct((nb, nt, SUBLANES, CONV_CH), F32))
        out_specs = [row3(w) for w, _ in widths]
        out_specs.append(pl.BlockSpec((None, None, SUBLANES, CONV_CH), lambda b, t: (b, t, 0, 0)))
        sem = ("arbitrary", "arbitrary")
    return pl.pallas_call(
        functools.partial(_mixer_kernel, rows=rows, sample_mode=sample_mode),
        grid=grid,
        in_specs=in_specs + wspecs,
        out_specs=out_specs,
        out_shape=out_shape,
        scratch_shapes=[pltpu.VMEM((rows + 2 * SUBLANES, CONV_CH), F32)],
        compiler_params=pltpu.CompilerParams(dimension_semantics=sem, vmem_limit_bytes=VMEM_LIMIT),
        name="mixer_sample" if sample_mode else "mixer_prompt",
    )(*args, *weights)


def _softmax_update(s, m_ref, l_ref, acc_ref, pv):
    n = s.shape[1] // LANES
    m_old = m_ref[...]
    m_new = jnp.maximum(m_old, jnp.max(s, axis=-1, keepdims=True))
    alpha = jnp.exp(m_old - m_new)
    ps = [jnp.exp(s[:, c * LANES:(c + 1) * LANES] - m_new) for c in range(n)]
    l_ref[...] = alpha * l_ref[...] + functools.reduce(lambda a, b: a + b, ps)
    p = ps[0] if n == 1 else jnp.concatenate(ps, axis=1)
    w = acc_ref.shape[-1] // LANES
    alpha_w = alpha if w == 1 else jnp.concatenate([alpha] * w, axis=1)
    acc_ref[...] = alpha_w * acc_ref[...] + pv(p.astype(BF16))
    m_ref[...] = m_new


def _prompt_attn_kernel(q_ref, k_ref, v_ref, km_ref, vm_ref, o_ref, m_sc, l_sc, acc_sc):
    seq = q_ref.shape[0]
    half = ATT_TQ // 2
    lane = lax.broadcasted_iota(jnp.int32, (ATT_TQ, SLOT), 1)
    m_sc[...] = jnp.full(m_sc.shape, -jnp.inf, F32)
    l_sc[...] = jnp.zeros(l_sc.shape, F32)
    acc_sc[...] = jnp.zeros(acc_sc.shape, F32)

    def step(qi, hh, r0, nr, k_blk, v_blk, mask):
        sl = slice(hh * SLOT, (hh + 1) * SLOT)
        rows = pl.ds(r0, nr)
        s = _dot_nt(q_ref[pl.ds(qi * ATT_TQ + r0, nr), sl], k_blk[:, sl])
        if mask is not None:
            s = jnp.where(mask, s, -jnp.inf)
        _softmax_update(s, m_sc.at[qi, hh, rows], l_sc.at[qi, hh, rows], acc_sc.at[qi, hh, rows],
                        lambda p: _dot(p, v_blk))

    meta = lane < N_META
    col_h = lax.broadcasted_iota(jnp.int32, (half, half), 1)
    row_h = lax.broadcasted_iota(jnp.int32, (half, half), 0)
    col_f = lax.broadcasted_iota(jnp.int32, (half, ATT_TK), 1)
    row_f = lax.broadcasted_iota(jnp.int32, (half, ATT_TK), 0)
    for qi in range(seq // ATT_TQ):
        for hh in range(2):
            step(qi, hh, 0, ATT_TQ, km_ref[...], vm_ref[...], meta)
        for ki in range(qi):
            ks = slice(ki * ATT_TK, (ki + 1) * ATT_TK)
            for hh in range(2):
                step(qi, hh, 0, ATT_TQ, k_ref[ks, :], v_ref[ks, :], None)
        d0 = qi * ATT_TK
        for hh in range(2):
            step(qi, hh, 0, half, k_ref[d0:d0 + half, :], v_ref[d0:d0 + half, :], col_h <= row_h)
            step(qi, hh, half, half, k_ref[d0:d0 + ATT_TK, :], v_ref[d0:d0 + ATT_TK, :],
                 col_f <= row_f + half)
        outs = [acc_sc[qi, hh] / jnp.sum(l_sc[qi, hh], axis=-1, keepdims=True) for hh in range(2)]
        o_ref[qi * ATT_TQ:(qi + 1) * ATT_TQ, :] = (
            jnp.where(lane < V_HEAD, outs[0], outs[1]).astype(o_ref.dtype))


def _prompt_attn_call(q, k, v, k_meta, v_meta):
    nb, seq, _ = q.shape
    assert ATT_TQ == ATT_TK and seq % ATT_TQ == 0
    grid = (nb, N_HEADS // 2)
    return pl.pallas_call(
        _prompt_attn_kernel,
        grid=grid,
        in_specs=[
            pl.BlockSpec((None, seq, 2 * SLOT), lambda b, p: (b, 0, p)),
            pl.BlockSpec((None, seq, 2 * SLOT), lambda b, p: (b, 0, p)),
            pl.BlockSpec((None, seq, SLOT), lambda b, p: (b, 0, p)),
            pl.BlockSpec((LANES, 2 * SLOT), lambda b, p: (0, p)),
            pl.BlockSpec((LANES, SLOT), lambda b, p: (0, p)),
        ],
        out_specs=pl.BlockSpec((None, seq, SLOT), lambda b, p: (b, 0, p)),
        out_shape=jax.ShapeDtypeStruct((nb, seq, ATTN_WIDTH), BF16),
        scratch_shapes=[pltpu.VMEM((seq // ATT_TQ, 2, ATT_TQ, LANES), F32)] * 3,
        compiler_params=pltpu.CompilerParams(
            dimension_semantics=("arbitrary", "arbitrary"),
            vmem_limit_bytes=VMEM_LIMIT),
        name="prompt_attn",
    )(q, k, v, k_meta, v_meta)


def _sample_attn_kernel(pt_ref, qabs_ref, qpe_ref, wukt_ref, wuv_ref, tlat_ref, tkpe_ref,
                        lat_hbm, kpe_hbm, o_ref,
                        m_sc, l_sc, acc_sc, latb_sc, lhs_sc, lat_buf, kpe_buf, sems):
    npg = PAGES_PER_CHUNK
    nchunk = lat_buf.shape[1] // npg
    b = pl.program_id(0)
    last = pl.num_programs(0) - 1
    slot = b % 2
    nq = qabs_ref.shape[0]
    reps = nq // N_HEADS
    nkn = N_HEADS * QK_NOPE
    qpe = qpe_ref[...]

    def chunk_copies(bb, sl, cc):
        copies = []
        for j in range(cc * npg, (cc + 1) * npg):
            page = pt_ref[bb, j]
            copies.append(pltpu.make_async_copy(lat_hbm.at[page], lat_buf.at[sl, j], sems.at[0, sl, cc]))
            copies.append(pltpu.make_async_copy(kpe_hbm.at[page], kpe_buf.at[sl, j], sems.at[1, sl, cc]))
        return copies

    @pl.when(b == 0)
    def _():
        lhs_sc[0:nkn, :] = wukt_ref[...]
        for cc in range(nchunk):
            for cp in chunk_copies(0, 0, cc):
                cp.start()

    nxt = jnp.minimum(b + 1, last)

    lhs_sc[nkn:nkn + nq, :] = qabs_ref[...]
    m_sc[...] = jnp.full(m_sc.shape, -jnp.inf, F32)
    l_sc[...] = jnp.zeros(l_sc.shape, F32)
    acc_sc[...] = jnp.zeros(acc_sc.shape, F32)

    def scores(lat_b, kpe_t):
        nk = lat_b.shape[0]
        full = _dot_nt(lhs_sc[...], lat_b)
        knt = full[:nkn]
        nsum = jnp.sum((knt * knt).reshape(QK_NOPE, N_HEADS, nk), axis=0)
        rsum = jnp.sum(kpe_t * kpe_t, axis=0, keepdims=True)
        r = lax.rsqrt((nsum + rsum) * (1.0 / QK_HEAD) + EPS)
        s = full[nkn:] + _dot(qpe, kpe_t.astype(BF16))
        return s * jnp.concatenate([r] * reps, axis=0)

    tail_b = tlat_ref[...].astype(BF16)
    s = scores(tail_b, tkpe_ref[...])
    key = lax.broadcasted_iota(jnp.int32, s.shape, 1)
    qt = lax.broadcasted_iota(jnp.int32, s.shape, 0) // N_HEADS
    _softmax_update(jnp.where(key <= qt, s, -jnp.inf), m_sc, l_sc, acc_sc, lambda p: _dot(p, tail_b))

    for cc in range(nchunk):
        for cp in chunk_copies(b, slot, cc):
            cp.wait()
    for cc in range(nchunk):
        for cp in chunk_copies(nxt, 1 - slot, cc):
            cp.start()
        rows = slice(cc * npg * PAGE_SIZE, (cc + 1) * npg * PAGE_SIZE)
        for j in range(cc * npg, (cc + 1) * npg):
            latb_sc[j * PAGE_SIZE:(j + 1) * PAGE_SIZE, :] = lat_buf[slot, j].astype(BF16)
        parts = []
        for j in range(cc * npg, (cc + 1) * npg, 2):
            kpe_t = jnp.concatenate([kpe_buf[slot, j], kpe_buf[slot, j + 1]], axis=1)
            parts.append(scores(latb_sc[j * PAGE_SIZE:(j + 2) * PAGE_SIZE, :], kpe_t))
        _softmax_update(jnp.concatenate(parts, axis=1), m_sc, l_sc, acc_sc,
                        lambda p: _dot(p, latb_sc[rows, :]))

    o_lat = (acc_sc[...] / jnp.sum(l_sc[...], axis=-1, keepdims=True)).astype(BF16)
    full = _dot(o_lat, wuv_ref[...])
    row_h = lax.broadcasted_iota(jnp.int32, full.shape, 0) % N_HEADS
    col_h = lax.broadcasted_iota(jnp.int32, full.shape, 1) // V_HEAD
    own = jnp.where(row_h == col_h, full, 0.0)
    o_ref[...] = jnp.sum(own.reshape(reps, N_HEADS, ATTN_WIDTH), axis=1)

    @pl.when(b == last)
    def _():
        for cc in range(nchunk):
            for cp in chunk_copies(last, 1 - slot, cc):
                cp.wait()


def _sample_attn_call(page_table, qabs, qpe, w_ukt, w_uv, tail_lat, tail_kpe_t, cache_lat, cache_kpe_t):
    nseq, n_pages = page_table.shape
    nq = qabs.shape[1]
    assert n_pages % PAGES_PER_CHUNK == 0 and PAGES_PER_CHUNK % 2 == 0
    grid = (nseq,)

    per_seq = lambda rows, width: pl.BlockSpec((None, rows, width), lambda b, pt: (b, 0, 0))
    whole = lambda shape: pl.BlockSpec(shape, lambda b, pt: (0,) * len(shape),
                                       pipeline_mode=pl.Buffered(1))
    in_specs = [per_seq(nq, KV_LORA), per_seq(nq, QK_ROPE), whole(w_ukt.shape), whole(w_uv.shape),
                per_seq(TAIL_KEYS, KV_LORA), per_seq(QK_ROPE, TAIL_KEYS),
                pl.BlockSpec(memory_space=pl.ANY), pl.BlockSpec(memory_space=pl.ANY)]
    reps = nq // N_HEADS
    return pl.pallas_call(
        _sample_attn_kernel,
        grid_spec=pltpu.PrefetchScalarGridSpec(
            num_scalar_prefetch=1,
            grid=grid,
            in_specs=in_specs,
            out_specs=pl.BlockSpec((None, reps, ATTN_WIDTH), lambda b, pt: (b, 0, 0)),
            scratch_shapes=[pltpu.VMEM((nq, LANES), F32), pltpu.VMEM((nq, LANES), F32),
                            pltpu.VMEM((nq, KV_LORA), F32),
                            pltpu.VMEM((n_pages * PAGE_SIZE, KV_LORA), BF16),
                            pltpu.VMEM((N_HEADS * QK_NOPE + nq, KV_LORA), BF16),
                            pltpu.VMEM((2, n_pages, PAGE_SIZE, KV_LORA), F32),
                            pltpu.VMEM((2, n_pages, QK_ROPE, PAGE_SIZE), F32),
                            pltpu.SemaphoreType.DMA((2, 2, n_pages // PAGES_PER_CHUNK))],
        ),
        out_shape=jax.ShapeDtypeStruct((nseq, reps, ATTN_WIDTH), F32),
        compiler_params=pltpu.CompilerParams(dimension_semantics=("arbitrary",),
                                             vmem_limit_bytes=VMEM_LIMIT),
        name="sample_attn",
    )(page_table, qabs, qpe, w_ukt, w_uv, tail_lat, tail_kpe_t, cache_lat, cache_kpe_t)


def _merge_ffn_kernel(x_ref, attn_ref, convn_ref, g_attn_ref, w_o_ref, g_ffn_ref, w_up_ref,
                      w_down_ref, y_ref):
    an = _rms(attn_ref[...].astype(F32), g_attn_ref[...]).astype(BF16)
    x1 = x_ref[...] + (_dot(an, w_o_ref[0:ATTN_WIDTH, :]) + _dot(convn_ref[...], w_o_ref[ATTN_WIDTH:, :]))
    hf = _rms(x1, g_ffn_ref[...]).astype(BF16)
    ffn = None
    for c in range(D_FF // FF_CHUNK):
        cs = slice(c * FF_CHUNK, (c + 1) * FF_CHUNK)
        up = jnp.maximum(_dot(hf, w_up_ref[:, cs]), 0.0)
        part = _dot((up * up).astype(BF16), w_down_ref[cs, :])
        ffn = part if ffn is None else ffn + part
    y_ref[...] = x1 + ffn


def _merge_ffn_call(x, attn, convn, weights, rows):
    n = x.shape[0]
    row = lambda w: pl.BlockSpec((rows, w), lambda i: (i, 0))
    return pl.pallas_call(
        _merge_ffn_kernel,
        grid=(n // rows,),
        in_specs=[row(D_MODEL), row(ATTN_WIDTH), row(CONV_CH)] + [_const_spec(w.shape) for w in weights],
        out_specs=row(D_MODEL),
        out_shape=jax.ShapeDtypeStruct((n, D_MODEL), F32),
        compiler_params=pltpu.CompilerParams(dimension_semantics=("arbitrary",),
                                             vmem_limit_bytes=VMEM_LIMIT),
        name="merge_ffn",
    )(x, attn, convn, *weights)


def _slots(w, width):
    k = w.shape[0]
    w = w.reshape(k, N_HEADS, width)
    return jnp.pad(w, ((0, 0), (0, 0), (0, SLOT - width))).reshape(k, HEADS_W)


def _rot_partner(w):
    return jnp.concatenate([-w[..., HALF_ROPE:], w[..., :HALF_ROPE]], axis=-1)


def _rope_slot(w):
    return jnp.pad(w, ((0, 0), (QK_NOPE, SLOT - QK_HEAD)))


def _layer_weights(w_in, q_lora_g, kv_lora_g, w_uq, w_ukv, q_norm_g, k_norm_g, conv_w, conv_b,
                   conv_out_g, norm_mix_g):
    o1 = Q_LORA
    o2 = o1 + KV_LORA
    o3 = o2 + QK_ROPE
    w_kpe = w_in[:, o2:o3]
    w_in_p = jnp.concatenate(
        [w_in[:, :o2], w_in[:, o3:], _rope_slot(w_kpe), _rope_slot(_rot_partner(w_kpe))],
        axis=1).astype(BF16)
    uq = w_uq.reshape(Q_LORA, N_HEADS, QK_HEAD)
    w_uq_p = _slots(w_uq, QK_HEAD).astype(BF16)
    uq_rot = jnp.pad(_rot_partner(uq[..., QK_NOPE:]), ((0, 0), (0, 0), (QK_NOPE, SLOT - QK_HEAD)))
    w_uq_r = uq_rot.reshape(Q_LORA, HEADS_W).astype(BF16)
    ukv = w_ukv.reshape(KV_LORA, N_HEADS, QK_NOPE + V_HEAD)
    w_uk_p = _slots(ukv[..., :QK_NOPE].reshape(KV_LORA, N_HEADS * QK_NOPE), QK_NOPE).astype(BF16)
    w_uv = ukv[..., QK_NOPE:].reshape(KV_LORA, ATTN_WIDTH).astype(BF16)
    w_ukt = jnp.transpose(ukv[..., :QK_NOPE], (2, 1, 0)).reshape(N_HEADS * QK_NOPE, KV_LORA).astype(BF16)
    pad_g = lambda g: jnp.pad(g, (0, SLOT - QK_HEAD))[None, :]
    gq = pad_g(q_norm_g) * ATTN_SCALE
    gk = pad_g(k_norm_g)
    mixer = (norm_mix_g[None, :], w_in_p, q_lora_g[None, :], w_uq_p, w_uq_r, kv_lora_g[None, :],
             w_uk_p, w_uv, gq, gk, conv_w, conv_b[None, :], conv_out_g[None, :])
    return mixer, w_ukt, w_uv


def _rope_slot_tables(pos):
    inv_freq = ROPE_THETA ** (-(jnp.arange(0, QK_ROPE, 2, dtype=F32) / QK_ROPE))
    ang = pos.astype(F32)[:, None] * inv_freq[None, :]
    n = pos.shape[0]
    cos2 = jnp.concatenate([jnp.cos(ang)] * 2, axis=1)
    sin2 = jnp.concatenate([jnp.sin(ang)] * 2, axis=1)
    cos_t = jnp.concatenate([jnp.ones((n, QK_NOPE), F32), cos2, jnp.zeros((n, SLOT - QK_HEAD), F32)], axis=1)
    sin_t = jnp.pad(sin2, ((0, 0), (QK_NOPE, SLOT - QK_HEAD)))
    return cos_t, sin_t


def kernel(x_prompt, x_sample, cache_kv_latent, cache_k_rope, state_conv, page_table, meta_tokens,
           norm_mix_g, w_in, q_lora_g, kv_lora_g, w_uq, w_ukv, q_norm_g, k_norm_g, conv_w, conv_b,
           attn_out_g, conv_out_g, w_o, norm_ffn_g, w_up, w_down):
    depth = w_in.shape[0]
    assert depth == 1, "the prompt and sample streams are chained for a single layer"
    nb, seq, _ = x_prompt.shape
    nseq, dec, _ = x_sample.shape
    past = page_table.shape[1] * PAGE_SIZE
    l = 0

    mixer_w, w_ukt, w_uv = _layer_weights(w_in[l], q_lora_g[l], kv_lora_g[l], w_uq[l], w_ukv[l],
                                          q_norm_g[l], k_norm_g[l], conv_w[l], conv_b[l],
                                          conv_out_g[l], norm_mix_g[l])
    ffn_w = (attn_out_g[l][None, :], w_o[l].astype(BF16), norm_ffn_g[l][None, :],
             w_up[l].astype(BF16), w_down[l].astype(BF16))

    ns = nseq * dec
    xs_rows = jnp.concatenate([x_sample.reshape(ns, D_MODEL), meta_tokens.astype(F32)], axis=0)
    pos_s = jnp.concatenate([jnp.tile(past + jnp.arange(dec), nseq), jnp.arange(N_META)])
    cos_s, sin_s = _rope_slot_tables(pos_s)
    t_in_seq = jnp.concatenate([jnp.tile(jnp.arange(dec), nseq), jnp.arange(N_META)])
    m1 = (t_in_seq >= 1).astype(F32)[:, None]
    m2 = (t_in_seq >= 2).astype(F32)[:, None]
    st = state_conv[l].astype(F32)
    zrow = jnp.zeros((nseq, 1, CONV_CH), F32)
    s1 = jnp.concatenate([st[:, 1:2], zrow, zrow, zrow], axis=1)[:, :dec]
    s2 = jnp.concatenate([st[:, 0:1], st[:, 1:2], zrow, zrow], axis=1)[:, :dec]
    zmeta = jnp.zeros((N_META, CONV_CH), F32)
    s1 = jnp.concatenate([s1.reshape(ns, CONV_CH), zmeta], axis=0)
    s2 = jnp.concatenate([s2.reshape(ns, CONV_CH), zmeta], axis=0)
    (qk_s, qabs_s, k_s, v_s, lat_s, kpe_s, convn_s, u_s) = _mixer_call(
        xs_rows, cos_s, sin_s, (m1, m2, s1, s2), mixer_w, sample_mode=True)

    pad_meta = lambda a: jnp.pad(a[ns:], ((0, LANES - N_META), (0, 0)))
    k_meta, v_meta = pad_meta(k_s), pad_meta(v_s)
    lat_meta, kpe_meta, u_meta = lat_s[ns:], kpe_s[ns:], u_s[ns:]

    cos_p, sin_p = _rope_slot_tables(N_META + jnp.arange(seq))
    q_p, k_p, v_p, lat_p, kpe_p, convn_p, utail_p = _mixer_call(
        x_prompt, cos_p, sin_p, u_meta[N_META - SUBLANES:], mixer_w, sample_mode=False)
    attn_p = _prompt_attn_call(q_p, k_p, v_p, k_meta, v_meta)
    y_prompt = _merge_ffn_call(x_prompt.reshape(nb * seq, D_MODEL), attn_p.reshape(nb * seq, ATTN_WIDTH),
                               convn_p.reshape(nb * seq, CONV_CH), ffn_w, ROW_TILE)
    y_prompt = y_prompt.reshape(nb, seq, D_MODEL)
    bcast = lambda a: jnp.broadcast_to(a[None], (nb,) + a.shape)
    new_lat_prompt = jnp.concatenate([bcast(lat_meta), lat_p], axis=1)[None]
    new_kpe_prompt = jnp.concatenate([bcast(kpe_meta), kpe_p], axis=1)[None]
    new_conv_prompt = utail_p[:, -1, SUBLANES - (CONV_W - 1):][None]

    nq = dec * N_HEADS
    qabs = qabs_s[:ns].reshape(nseq, nq, KV_LORA)
    qpe = qk_s[:ns].reshape(ns, N_HEADS, SLOT)[:, :, QK_NOPE:QK_HEAD].reshape(nseq, nq, QK_ROPE)
    pad_tail = lambda a: jnp.pad(a[:ns].reshape(nseq, dec, -1), ((0, 0), (0, TAIL_KEYS - dec), (0, 0)))
    attn_s = _sample_attn_call(page_table, qabs, qpe, w_ukt, w_uv, pad_tail(lat_s),
                               jnp.swapaxes(pad_tail(kpe_s), 1, 2),
                               cache_kv_latent[l], jnp.swapaxes(cache_k_rope[l], 1, 2))
    y_sample = _merge_ffn_call(x_sample.reshape(ns, D_MODEL), attn_s.reshape(ns, ATTN_WIDTH),
                               convn_s[:ns], ffn_w, ns)
    y_sample = y_sample.reshape(nseq, dec, D_MODEL)
    new_lat_sample = lat_s[:ns].reshape(nseq, dec, KV_LORA)[None]
    new_kpe_sample = kpe_s[:ns].reshape(nseq, dec, QK_ROPE)[None]
    us = jnp.concatenate([st, u_s[:ns].reshape(nseq, dec, CONV_CH)], axis=1)
    new_conv_sample = us[:, -(CONV_W - 1):][None]

    return (y_prompt, y_sample, new_lat_prompt, new_kpe_prompt, new_conv_prompt,
            new_lat_sample, new_kpe_sample, new_conv_sample)
```

```python
import functools

import jax
import jax.numpy as jnp
from jax import lax
from jax.experimental import pallas as pl
from jax.experimental.pallas import tpu as pltpu

D_MODEL = 1024
N_META = 16
N_HEADS = 8
QK_NOPE = 64
QK_ROPE = 32
V_HEAD = 64
QK_HEAD = QK_NOPE + QK_ROPE
Q_LORA = 384
KV_LORA = 256
ATTN_WIDTH = N_HEADS * V_HEAD
CONV_CH = D_MODEL - ATTN_WIDTH
CONV_W = 3
D_FF = 4 * D_MODEL
ROPE_THETA = 10000.0
EPS = 1e-6
PAGE_SIZE = 128
ATTN_SCALE = QK_HEAD ** -0.5
LOG2_E = 1.4426950408889634

LANES = 128
SUBLANES = 8
SLOT = LANES
HEADS_W = N_HEADS * SLOT
HALF_ROPE = QK_ROPE // 2

C_CQ = 0
C_CKV = C_CQ + Q_LORA
C_GB = C_CKV + KV_LORA
C_GC = C_GB + CONV_CH
C_HC = C_GC + CONV_CH
C_KPE = C_HC + CONV_CH
C_KPR = C_KPE + SLOT
IN_W = C_KPR + SLOT

ROW_TILE = 512
ATT_TQ = 512
ATT_TK = 512
FF_CHUNK = 1024
LAST_CHUNK_PAGES = 32
DMA_GROUP = 8
TAIL_KEYS = PAGE_SIZE
VMEM_LIMIT = 52 * 1024 * 1024

BF16 = jnp.bfloat16
F32 = jnp.float32

_NT = (((1,), (1,)), ((), ()))


def _dot(a, b):
    return jnp.dot(a, b, preferred_element_type=F32)


def _dot_nt(a, b):
    return lax.dot_general(a, b, _NT, preferred_element_type=F32)


def _rms(x, g):
    return x * lax.rsqrt(jnp.mean(x * x, axis=-1, keepdims=True) + EPS) * g


def _const_spec(shape):
    nd = len(shape)
    return pl.BlockSpec(shape, lambda *_: (0,) * nd, pipeline_mode=pl.Buffered(1))


def _mixer_kernel(*refs, rows, sample_mode):
    it = iter(refs)
    x_ref, cos_ref, sin_ref = next(it), next(it), next(it)
    if sample_mode:
        m1_ref, m2_ref, s1_ref, s2_ref = next(it), next(it), next(it), next(it)
    else:
        tail_in_ref = next(it)
    (g_mix_ref, w_in_ref, g_ql_ref, w_uq_ref, w_uqr_ref, g_kvl_ref, w_uk_ref, w_uv_ref,
     gq_ref, gk_ref, cw_ref, cb_ref, g_conv_ref) = (next(it) for _ in range(13))
    if sample_mode:
        (qk_ref, qabs_ref, k_ref, v_ref, lat_ref, kpe_ref, convn_ref, u_ref) = (next(it) for _ in range(8))
    else:
        (q_ref, k_ref, v_ref, lat_ref, kpe_ref, convn_ref, utail_ref) = (next(it) for _ in range(7))
    ubuf = next(it)

    hn = _rms(x_ref[...], g_mix_ref[...]).astype(BF16)
    cos = cos_ref[...]
    sin = sin_ref[...]

    cq = _dot(hn, w_in_ref[:, C_CQ:C_CKV])
    cqn = _rms(cq, g_ql_ref[...]).astype(BF16)
    gq = gq_ref[...]
    gk = gk_ref[...]
    for h in range(N_HEADS):
        sl = slice(h * SLOT, (h + 1) * SLOT)
        qh = _dot(cqn, w_uq_ref[:, sl]) * cos + _dot(cqn, w_uqr_ref[:, sl]) * sin
        ss = jnp.sum(qh * qh, axis=-1, keepdims=True)
        qh = qh * lax.rsqrt(ss * (1.0 / QK_HEAD) + EPS) * gq
        if sample_mode:
            qkh = (qh * gk).astype(BF16)
            qk_ref[:, sl] = qkh
            qabs_ref[:, h * KV_LORA:(h + 1) * KV_LORA] = _dot_nt(qkh, w_uk_ref[:, sl]).astype(BF16)
        else:
            q_ref[:, sl] = qh.astype(BF16)

    ckv = _dot(hn, w_in_ref[:, C_CKV:C_GB])
    lat = _rms(ckv, g_kvl_ref[...])
    lat_ref[...] = lat
    lat_b = lat.astype(BF16)
    zk = _dot(hn, w_in_ref[:, C_KPE:IN_W])
    krot = zk[:, :SLOT] * cos + zk[:, SLOT:] * sin
    kpe_ref[...] = krot[:, QK_NOPE:QK_HEAD]
    ss_rot = jnp.sum(krot * krot, axis=-1, keepdims=True)
    for h in range(N_HEADS):
        sl = slice(h * SLOT, (h + 1) * SLOT)
        kn = _dot(lat_b, w_uk_ref[:, sl])
        ss = jnp.sum(kn * kn, axis=-1, keepdims=True) + ss_rot
        kh = (kn + krot) * lax.rsqrt(ss * (1.0 / QK_HEAD) + EPS) * gk
        k_ref[:, sl] = kh.astype(BF16)
    v_ref[...] = _dot(lat_b, w_uv_ref[...]).astype(BF16)

    gc = _dot(hn, w_in_ref[:, C_GC:C_HC])
    hc = _dot(hn, w_in_ref[:, C_HC:C_KPE])
    u = gc * hc
    if sample_mode:
        ubuf[0:SUBLANES, :] = jnp.zeros((SUBLANES, CONV_CH), F32)
    else:
        t = pl.program_id(1)

        @pl.when(t == 0)
        def _():
            ubuf[0:SUBLANES, :] = tail_in_ref[...]

        @pl.when(t != 0)
        def _():
            ubuf[0:SUBLANES, :] = ubuf[rows:rows + SUBLANES, :]
    ubuf[SUBLANES:SUBLANES + rows, :] = u
    u1 = ubuf[SUBLANES - 1:SUBLANES - 1 + rows, :]
    u2 = ubuf[SUBLANES - 2:SUBLANES - 2 + rows, :]
    if sample_mode:
        u1 = u1 * m1_ref[...] + s1_ref[...]
        u2 = u2 * m2_ref[...] + s2_ref[...]
        u_ref[...] = u
    else:
        utail_ref[...] = u[rows - SUBLANES:, :]
    y = cb_ref[...] + u2 * cw_ref[0:1, :] + u1 * cw_ref[1:2, :] + u * cw_ref[2:3, :]
    gb = _dot(hn, w_in_ref[:, C_GB:C_GC])
    convn_ref[...] = _rms(gb * y, g_conv_ref[...]).astype(BF16)


def _mixer_call(x, cos_t, sin_t, conv_in, weights, *, sample_mode):
    (g_mix, w_in_p, g_ql, w_uq_p, w_uq_r, g_kvl, w_uk_p, w_uv, gq, gk, cw, cb, g_conv) = weights
    wspecs = [_const_spec(w.shape) for w in weights]
    if sample_mode:
        rows = x.shape[0]
        grid = (1,)
        row = lambda w: pl.BlockSpec((rows, w), lambda i: (0, 0))
        m1, m2, s1, s2 = conv_in
        in_specs = [row(D_MODEL), row(SLOT), row(SLOT), row(1), row(1), row(CONV_CH), row(CONV_CH)]
        args = [x, cos_t, sin_t, m1, m2, s1, s2]
        widths = [(HEADS_W, BF16), (N_HEADS * KV_LORA, BF16), (HEADS_W, BF16), (ATTN_WIDTH, BF16),
                  (KV_LORA, F32), (QK_ROPE, F32), (CONV_CH, BF16), (CONV_CH, F32)]
        out_shape = [jax.ShapeDtypeStruct((rows, w), d) for w, d in widths]
        out_specs = [row(w) for w, _ in widths]
        sem = ("arbitrary",)
    else:
        nb, seq, _ = x.shape
        rows = ROW_TILE
        nt = seq // rows
        grid = (nb, nt)
        row3 = lambda w: pl.BlockSpec((None, rows, w), lambda b, t: (b, t, 0))
        tab = pl.BlockSpec((rows, SLOT), lambda b, t: (t, 0))
        in_specs = [row3(D_MODEL), tab, tab, _const_spec(conv_in.shape)]
        args = [x, cos_t, sin_t, conv_in]
        widths = [(HEADS_W, BF16), (HEADS_W, BF16), (ATTN_WIDTH, BF16), (KV_LORA, F32),
                  (QK_ROPE, F32), (CONV_CH, BF16)]
        out_shape = [jax.ShapeDtypeStruct((nb, seq, w), d) for w, d in widths]
        out_shape.append(jax.ShapeDtypeStruct((nb, nt, SUBLANES, CONV_CH), F32))
        out_specs = [row3(w) for w, _ in widths]
        out_specs.append(pl.BlockSpec((None, None, SUBLANES, CONV_CH), lambda b, t: (b, t, 0, 0)))
        sem = ("arbitrary", "arbitrary")
    return pl.pallas_call(
        functools.partial(_mixer_kernel, rows=rows, sample_mode=sample_mode),
        grid=grid,
        in_specs=in_specs + wspecs,
        out_specs=out_specs,
        out_shape=out_shape,
        scratch_shapes=[pltpu.VMEM((rows + 2 * SUBLANES, CONV_CH), F32)],
        compiler_params=pltpu.CompilerParams(dimension_semantics=sem, vmem_limit_bytes=VMEM_LIMIT),
        name="mixer_sample" if sample_mode else "mixer_prompt",
    )(*args, *weights)


def _softmax_update(s, m_ref, l_ref, acc_ref, pv):
    n = s.shape[1] // LANES
    m_old = m_ref[...]
    m_new = jnp.maximum(m_old, jnp.max(s, axis=-1, keepdims=True))
    alpha = jnp.exp2(m_old - m_new)
    ps = [jnp.exp2(s[:, c * LANES:(c + 1) * LANES] - m_new) for c in range(n)]
    l_ref[...] = alpha * l_ref[...] + functools.reduce(lambda a, b: a + b, ps)
    p = ps[0] if n == 1 else jnp.concatenate(ps, axis=1)
    w = acc_ref.shape[-1] // LANES
    alpha_w = alpha if w == 1 else jnp.concatenate([alpha] * w, axis=1)
    acc_ref[...] = alpha_w * acc_ref[...] + pv(p.astype(BF16))
    m_ref[...] = m_new


def _prompt_attn_kernel(q_ref, k_ref, v_ref, km_ref, vm_ref, o_ref, m_sc, l_sc, acc_sc):
    seq = q_ref.shape[0]
    half = ATT_TQ // 2
    lane = lax.broadcasted_iota(jnp.int32, (ATT_TQ, SLOT), 1)
    m_sc[...] = jnp.full(m_sc.shape, -jnp.inf, F32)
    l_sc[...] = jnp.zeros(l_sc.shape, F32)
    acc_sc[...] = jnp.zeros(acc_sc.shape, F32)

    def step(qi, hh, r0, nr, k_blk, v_blk, mask):
        sl = slice(hh * SLOT, (hh + 1) * SLOT)
        rows = pl.ds(r0, nr)
        s = _dot_nt(q_ref[pl.ds(qi * ATT_TQ + r0, nr), sl], k_blk[:, sl])
        if mask is not None:
            s = jnp.where(mask, s, -jnp.inf)
        _softmax_update(s, m_sc.at[qi, hh, rows], l_sc.at[qi, hh, rows], acc_sc.at[qi, hh, rows],
                        lambda p: _dot(p, v_blk))

    meta = lane < N_META
    col_h = lax.broadcasted_iota(jnp.int32, (half, half), 1)
    row_h = lax.broadcasted_iota(jnp.int32, (half, half), 0)
    col_f = lax.broadcasted_iota(jnp.int32, (half, ATT_TK), 1)
    row_f = lax.broadcasted_iota(jnp.int32, (half, ATT_TK), 0)
    for qi in range(seq // ATT_TQ):
        for hh in range(2):
            step(qi, hh, 0, ATT_TQ, km_ref[...], vm_ref[...], meta)
        for ki in range(qi):
            ks = slice(ki * ATT_TK, (ki + 1) * ATT_TK)
            for hh in range(2):
                step(qi, hh, 0, ATT_TQ, k_ref[ks, :], v_ref[ks, :], None)
        d0 = qi * ATT_TK
        for hh in range(2):
            step(qi, hh, 0, half, k_ref[d0:d0 + half, :], v_ref[d0:d0 + half, :], col_h <= row_h)
            step(qi, hh, half, half, k_ref[d0:d0 + ATT_TK, :], v_ref[d0:d0 + ATT_TK, :],
                 col_f <= row_f + half)
        outs = [acc_sc[qi, hh] / jnp.sum(l_sc[qi, hh], axis=-1, keepdims=True) for hh in range(2)]
        o_ref[qi * ATT_TQ:(qi + 1) * ATT_TQ, :] = (
            jnp.where(lane < V_HEAD, outs[0], outs[1]).astype(o_ref.dtype))


def _prompt_attn_call(q, k, v, k_meta, v_meta):
    nb, seq, _ = q.shape
    assert ATT_TQ == ATT_TK and seq % ATT_TQ == 0
    grid = (nb, N_HEADS // 2)
    return pl.pallas_call(
        _prompt_attn_kernel,
        grid=grid,
        in_specs=[
            pl.BlockSpec((None, seq, 2 * SLOT), lambda b, p: (b, 0, p)),
            pl.BlockSpec((None, seq, 2 * SLOT), lambda b, p: (b, 0, p)),
            pl.BlockSpec((None, seq, SLOT), lambda b, p: (b, 0, p)),
            pl.BlockSpec((LANES, 2 * SLOT), lambda b, p: (0, p)),
            pl.BlockSpec((LANES, SLOT), lambda b, p: (0, p)),
        ],
        out_specs=pl.BlockSpec((None, seq, SLOT), lambda b, p: (b, 0, p)),
        out_shape=jax.ShapeDtypeStruct((nb, seq, ATTN_WIDTH), BF16),
        scratch_shapes=[pltpu.VMEM((seq // ATT_TQ, 2, ATT_TQ, LANES), F32)] * 3,
        compiler_params=pltpu.CompilerParams(
            dimension_semantics=("arbitrary", "arbitrary"),
            vmem_limit_bytes=VMEM_LIMIT),
        name="prompt_attn",
    )(q, k, v, k_meta, v_meta)


def _sample_attn_kernel(pt_ref, qabs_ref, qpe_ref, wukt_ref, wuv_ref, tlat_ref, tkpe_ref,
                        lat_hbm, kpe_hbm, o_ref,
                        m_sc, l_sc, acc_sc, latb_sc, lhs_sc, lat_buf, kpe_buf, sems):
    n_pages = lat_buf.shape[1]
    chunk_pages = (n_pages - LAST_CHUNK_PAGES, LAST_CHUNK_PAGES)
    b = pl.program_id(0)
    last = pl.num_programs(0) - 1
    slot = b % 2
    nq = qabs_ref.shape[0]
    reps = nq // N_HEADS
    nkn = N_HEADS * QK_NOPE
    qpe = qpe_ref[...]

    ngroup = lat_buf.shape[1] // DMA_GROUP

    def group_copies(bb, sl, gg):
        copies = []
        for j in range(gg * DMA_GROUP, (gg + 1) * DMA_GROUP):
            page = pt_ref[bb, j]
            copies.append(pltpu.make_async_copy(lat_hbm.at[page], lat_buf.at[sl, j], sems.at[0, sl]))
            copies.append(pltpu.make_async_copy(kpe_hbm.at[page], kpe_buf.at[sl, j], sems.at[1, sl]))
        return copies

    @pl.when(b == 0)
    def _():
        lhs_sc[0:nkn, :] = wukt_ref[...]
        for gg in range(ngroup):
            for cp in group_copies(0, 0, gg):
                cp.start()

    nxt = jnp.minimum(b + 1, last)

    lhs_sc[nkn:nkn + nq, :] = qabs_ref[...]
    m_sc[...] = jnp.full(m_sc.shape, -jnp.inf, F32)
    l_sc[...] = jnp.zeros(l_sc.shape, F32)
    acc_sc[...] = jnp.zeros(acc_sc.shape, F32)

    def scores(lat_b, kpe_t):
        nk = lat_b.shape[0]
        full = _dot_nt(lhs_sc[...], lat_b)
        knt = full[:nkn]
        nsum = jnp.sum((knt * knt).reshape(QK_NOPE, N_HEADS, nk), axis=0)
        rsum = jnp.sum(kpe_t * kpe_t, axis=0, keepdims=True)
        r = lax.rsqrt((nsum + rsum) * (1.0 / QK_HEAD) + EPS)
        s = full[nkn:] + _dot(qpe, kpe_t.astype(BF16))
        return s * jnp.concatenate([r] * reps, axis=0)

    for gg in range(ngroup):
        for cp in group_copies(b, slot, gg):
            cp.wait()

    tail_b = tlat_ref[...].astype(BF16)
    s = scores(tail_b, tkpe_ref[...])
    key = lax.broadcasted_iota(jnp.int32, s.shape, 1)
    qt = lax.broadcasted_iota(jnp.int32, s.shape, 0) // N_HEADS
    s_tail = jnp.where(key <= qt, s, -jnp.inf)

    p0 = 0
    for npg in chunk_pages:
        rows = slice(p0 * PAGE_SIZE, (p0 + npg) * PAGE_SIZE)
        first = p0 == 0
        parts = [s_tail] if first else []
        for j in range(p0, p0 + npg, 2):
            for jj in (j, j + 1):
                latb_sc[jj * PAGE_SIZE:(jj + 1) * PAGE_SIZE, :] = lat_buf[slot, jj].astype(BF16)
            kpe_t = jnp.concatenate([kpe_buf[slot, j], kpe_buf[slot, j + 1]], axis=1)
            parts.append(scores(latb_sc[j * PAGE_SIZE:(j + 2) * PAGE_SIZE, :], kpe_t))
            if (j + 2) % DMA_GROUP == 0:
                for cp in group_copies(nxt, 1 - slot, (j + 2) // DMA_GROUP - 1):
                    cp.start()
        if first:
            pv = lambda p: _dot(p[:, :TAIL_KEYS], tail_b) + _dot(p[:, TAIL_KEYS:], latb_sc[rows, :])
        else:
            pv = lambda p: _dot(p, latb_sc[rows, :])
        _softmax_update(jnp.concatenate(parts, axis=1), m_sc, l_sc, acc_sc, pv)
        p0 += npg

    o_lat = (acc_sc[...] / jnp.sum(l_sc[...], axis=-1, keepdims=True)).astype(BF16)
    full = _dot(o_lat, wuv_ref[...])
    row_h = lax.broadcasted_iota(jnp.int32, full.shape, 0) % N_HEADS
    col_h = lax.broadcasted_iota(jnp.int32, full.shape, 1) // V_HEAD
    own = jnp.where(row_h == col_h, full, 0.0)
    o_ref[...] = jnp.sum(own.reshape(reps, N_HEADS, ATTN_WIDTH), axis=1)

    @pl.when(b == last)
    def _():
        for gg in range(ngroup):
            for cp in group_copies(last, 1 - slot, gg):
                cp.wait()


def _sample_attn_call(page_table, qabs, qpe, w_ukt, w_uv, tail_lat, tail_kpe_t, cache_lat, cache_kpe_t):
    nseq, n_pages = page_table.shape
    nq = qabs.shape[1]
    assert n_pages % DMA_GROUP == 0 and DMA_GROUP % 2 == 0
    assert 0 < LAST_CHUNK_PAGES < n_pages and LAST_CHUNK_PAGES % 2 == 0
    grid = (nseq,)

    per_seq = lambda rows, width: pl.BlockSpec((None, rows, width), lambda b, pt: (b, 0, 0))
    whole = lambda shape: pl.BlockSpec(shape, lambda b, pt: (0,) * len(shape),
                                       pipeline_mode=pl.Buffered(1))
    in_specs = [per_seq(nq, KV_LORA), per_seq(nq, QK_ROPE), whole(w_ukt.shape), whole(w_uv.shape),
                per_seq(TAIL_KEYS, KV_LORA), per_seq(QK_ROPE, TAIL_KEYS),
                pl.BlockSpec(memory_space=pl.ANY), pl.BlockSpec(memory_space=pl.ANY)]
    reps = nq // N_HEADS
    return pl.pallas_call(
        _sample_attn_kernel,
        grid_spec=pltpu.PrefetchScalarGridSpec(
            num_scalar_prefetch=1,
            grid=grid,
            in_specs=in_specs,
            out_specs=pl.BlockSpec((None, reps, ATTN_WIDTH), lambda b, pt: (b, 0, 0)),
            scratch_shapes=[pltpu.VMEM((nq, LANES), F32), pltpu.VMEM((nq, LANES), F32),
                            pltpu.VMEM((nq, KV_LORA), F32),
                            pltpu.VMEM((n_pages * PAGE_SIZE, KV_LORA), BF16),
                            pltpu.VMEM((N_HEADS * QK_NOPE + nq, KV_LORA), BF16),
                            pltpu.VMEM((2, n_pages, PAGE_SIZE, KV_LORA), F32),
                            pltpu.VMEM((2, n_pages, QK_ROPE, PAGE_SIZE), F32),
                            pltpu.SemaphoreType.DMA((2, 2))],
        ),
        out_shape=jax.ShapeDtypeStruct((nseq, reps, ATTN_WIDTH), F32),
        compiler_params=pltpu.CompilerParams(dimension_semantics=("arbitrary",),
                                             vmem_limit_bytes=VMEM_LIMIT),
        name="sample_attn",
    )(page_table, qabs, qpe, w_ukt, w_uv, tail_lat, tail_kpe_t, cache_lat, cache_kpe_t)


def _merge_ffn_kernel(x_ref, attn_ref, convn_ref, g_attn_ref, w_o_ref, g_ffn_ref, w_up_ref,
                      w_down_ref, y_ref):
    an = _rms(attn_ref[...].astype(F32), g_attn_ref[...]).astype(BF16)
    x1 = x_ref[...] + (_dot(an, w_o_ref[0:ATTN_WIDTH, :]) + _dot(convn_ref[...], w_o_ref[ATTN_WIDTH:, :]))
    hf = _rms(x1, g_ffn_ref[...]).astype(BF16)
    ffn = None
    for c in range(D_FF // FF_CHUNK):
        cs = slice(c * FF_CHUNK, (c + 1) * FF_CHUNK)
        up = jnp.maximum(_dot(hf, w_up_ref[:, cs]), 0.0)
        part = _dot((up * up).astype(BF16), w_down_ref[cs, :])
        ffn = part if ffn is None else ffn + part
    y_ref[...] = x1 + ffn


def _merge_ffn_call(x, attn, convn, weights, rows):
    n = x.shape[0]
    row = lambda w: pl.BlockSpec((rows, w), lambda i: (i, 0))
    return pl.pallas_call(
        _merge_ffn_kernel,
        grid=(n // rows,),
        in_specs=[row(D_MODEL), row(ATTN_WIDTH), row(CONV_CH)] + [_const_spec(w.shape) for w in weights],
        out_specs=row(D_MODEL),
        out_shape=jax.ShapeDtypeStruct((n, D_MODEL), F32),
        compiler_params=pltpu.CompilerParams(dimension_semantics=("arbitrary",),
                                             vmem_limit_bytes=VMEM_LIMIT),
        name="merge_ffn",
    )(x, attn, convn, *weights)


def _slots(w, width):
    k = w.shape[0]
    w = w.reshape(k, N_HEADS, width)
    return jnp.pad(w, ((0, 0), (0, 0), (0, SLOT - width))).reshape(k, HEADS_W)


def _rot_partner(w):
    return jnp.concatenate([-w[..., HALF_ROPE:], w[..., :HALF_ROPE]], axis=-1)


def _rope_slot(w):
    return jnp.pad(w, ((0, 0), (QK_NOPE, SLOT - QK_HEAD)))


def _layer_weights(w_in, q_lora_g, kv_lora_g, w_uq, w_ukv, q_norm_g, k_norm_g, conv_w, conv_b,
                   conv_out_g, norm_mix_g):
    o1 = Q_LORA
    o2 = o1 + KV_LORA
    o3 = o2 + QK_ROPE
    w_kpe = w_in[:, o2:o3]
    w_in_p = jnp.concatenate(
        [w_in[:, :o2], w_in[:, o3:], _rope_slot(w_kpe), _rope_slot(_rot_partner(w_kpe))],
        axis=1).astype(BF16)
    uq = w_uq.reshape(Q_LORA, N_HEADS, QK_HEAD)
    w_uq_p = _slots(w_uq, QK_HEAD).astype(BF16)
    uq_rot = jnp.pad(_rot_partner(uq[..., QK_NOPE:]), ((0, 0), (0, 0), (QK_NOPE, SLOT - QK_HEAD)))
    w_uq_r = uq_rot.reshape(Q_LORA, HEADS_W).astype(BF16)
    ukv = w_ukv.reshape(KV_LORA, N_HEADS, QK_NOPE + V_HEAD)
    w_uk_p = _slots(ukv[..., :QK_NOPE].reshape(KV_LORA, N_HEADS * QK_NOPE), QK_NOPE).astype(BF16)
    w_uv = ukv[..., QK_NOPE:].reshape(KV_LORA, ATTN_WIDTH).astype(BF16)
    w_ukt = jnp.transpose(ukv[..., :QK_NOPE], (2, 1, 0)).reshape(N_HEADS * QK_NOPE, KV_LORA).astype(BF16)
    pad_g = lambda g: jnp.pad(g, (0, SLOT - QK_HEAD))[None, :]
    gq = pad_g(q_norm_g) * (ATTN_SCALE * LOG2_E)
    gk = pad_g(k_norm_g)
    mixer = (norm_mix_g[None, :], w_in_p, q_lora_g[None, :], w_uq_p, w_uq_r, kv_lora_g[None, :],
             w_uk_p, w_uv, gq, gk, conv_w, conv_b[None, :], conv_out_g[None, :])
    return mixer, w_ukt, w_uv


def _rope_slot_tables(pos):
    inv_freq = ROPE_THETA ** (-(jnp.arange(0, QK_ROPE, 2, dtype=F32) / QK_ROPE))
    ang = pos.astype(F32)[:, None] * inv_freq[None, :]
    n = pos.shape[0]
    cos2 = jnp.concatenate([jnp.cos(ang)] * 2, axis=1)
    sin2 = jnp.concatenate([jnp.sin(ang)] * 2, axis=1)
    cos_t = jnp.concatenate([jnp.ones((n, QK_NOPE), F32), cos2, jnp.zeros((n, SLOT - QK_HEAD), F32)], axis=1)
    sin_t = jnp.pad(sin2, ((0, 0), (QK_NOPE, SLOT - QK_HEAD)))
    return cos_t, sin_t


def kernel(x_prompt, x_sample, cache_kv_latent, cache_k_rope, state_conv, page_table, meta_tokens,
           norm_mix_g, w_in, q_lora_g, kv_lora_g, w_uq, w_ukv, q_norm_g, k_norm_g, conv_w, conv_b,
           attn_out_g, conv_out_g, w_o, norm_ffn_g, w_up, w_down):
    depth = w_in.shape[0]
    assert depth == 1, "the prompt and sample streams are chained for a single layer"
    nb, seq, _ = x_prompt.shape
    nseq, dec, _ = x_sample.shape
    past = page_table.shape[1] * PAGE_SIZE
    l = 0

    mixer_w, w_ukt, w_uv = _layer_weights(w_in[l], q_lora_g[l], kv_lora_g[l], w_uq[l], w_ukv[l],
                                          q_norm_g[l], k_norm_g[l], conv_w[l], conv_b[l],
                                          conv_out_g[l], norm_mix_g[l])
    ffn_w = (attn_out_g[l][None, :], w_o[l].astype(BF16), norm_ffn_g[l][None, :],
             w_up[l].astype(BF16), w_down[l].astype(BF16))

    ns = nseq * dec
    xs_rows = jnp.concatenate([x_sample.reshape(ns, D_MODEL), meta_tokens.astype(F32)], axis=0)
    pos_s = jnp.concatenate([jnp.tile(past + jnp.arange(dec), nseq), jnp.arange(N_META)])
    cos_s, sin_s = _rope_slot_tables(pos_s)
    t_in_seq = jnp.concatenate([jnp.tile(jnp.arange(dec), nseq), jnp.arange(N_META)])
    m1 = (t_in_seq >= 1).astype(F32)[:, None]
    m2 = (t_in_seq >= 2).astype(F32)[:, None]
    st = state_conv[l].astype(F32)
    zrow = jnp.zeros((nseq, 1, CONV_CH), F32)
    s1 = jnp.concatenate([st[:, 1:2], zrow, zrow, zrow], axis=1)[:, :dec]
    s2 = jnp.concatenate([st[:, 0:1], st[:, 1:2], zrow, zrow], axis=1)[:, :dec]
    zmeta = jnp.zeros((N_META, CONV_CH), F32)
    s1 = jnp.concatenate([s1.reshape(ns, CONV_CH), zmeta], axis=0)
    s2 = jnp.concatenate([s2.reshape(ns, CONV_CH), zmeta], axis=0)
    (qk_s, qabs_s, k_s, v_s, lat_s, kpe_s, convn_s, u_s) = _mixer_call(
        xs_rows, cos_s, sin_s, (m1, m2, s1, s2), mixer_w, sample_mode=True)

    pad_meta = lambda a: jnp.pad(a[ns:], ((0, LANES - N_META), (0, 0)))
    k_meta, v_meta = pad_meta(k_s), pad_meta(v_s)
    lat_meta, kpe_meta, u_meta = lat_s[ns:], kpe_s[ns:], u_s[ns:]

    cos_p, sin_p = _rope_slot_tables(N_META + jnp.arange(seq))
    q_p, k_p, v_p, lat_p, kpe_p, convn_p, utail_p = _mixer_call(
        x_prompt, cos_p, sin_p, u_meta[N_META - SUBLANES:], mixer_w, sample_mode=False)
    attn_p = _prompt_attn_call(q_p, k_p, v_p, k_meta, v_meta)
    y_prompt = _merge_ffn_call(x_prompt.reshape(nb * seq, D_MODEL), attn_p.reshape(nb * seq, ATTN_WIDTH),
                               convn_p.reshape(nb * seq, CONV_CH), ffn_w, ROW_TILE)
    y_prompt = y_prompt.reshape(nb, seq, D_MODEL)
    bcast = lambda a: jnp.broadcast_to(a[None], (nb,) + a.shape)
    new_lat_prompt = jnp.concatenate([bcast(lat_meta), lat_p], axis=1)[None]
    new_kpe_prompt = jnp.concatenate([bcast(kpe_meta), kpe_p], axis=1)[None]
    new_conv_prompt = utail_p[:, -1, SUBLANES - (CONV_W - 1):][None]

    nq = dec * N_HEADS
    qabs = qabs_s[:ns].reshape(nseq, nq, KV_LORA)
    qpe = qk_s[:ns].reshape(ns, N_HEADS, SLOT)[:, :, QK_NOPE:QK_HEAD].reshape(nseq, nq, QK_ROPE)
    pad_tail = lambda a: jnp.pad(a[:ns].reshape(nseq, dec, -1), ((0, 0), (0, TAIL_KEYS - dec), (0, 0)))
    attn_s = _sample_attn_call(page_table, qabs, qpe, w_ukt, w_uv, pad_tail(lat_s),
                               jnp.swapaxes(pad_tail(kpe_s), 1, 2),
                               cache_kv_latent[l], jnp.swapaxes(cache_k_rope[l], 1, 2))
    y_sample = _merge_ffn_call(x_sample.reshape(ns, D_MODEL), attn_s.reshape(ns, ATTN_WIDTH),
                               convn_s[:ns], ffn_w, ns)
    y_sample = y_sample.reshape(nseq, dec, D_MODEL)
    new_lat_sample = lat_s[:ns].reshape(nseq, dec, KV_LORA)[None]
    new_kpe_sample = kpe_s[:ns].reshape(nseq, dec, QK_ROPE)[None]
    us = jnp.concatenate([st, u_s[:ns].reshape(nseq, dec, CONV_CH)], axis=1)
    new_conv_sample = us[:, -(CONV_W - 1):][None]

    return (y_prompt, y_sample, new_lat_prompt, new_kpe_prompt, new_conv_prompt,
            new_lat_sample, new_kpe_sample, new_conv_sample)
```

```python
import functools

import jax
import jax.numpy as jnp
from jax import lax
from jax.experimental import pallas as pl
from jax.experimental.pallas import tpu as pltpu

D_MODEL = 1024
N_META = 16
N_HEADS = 8
QK_NOPE = 64
QK_ROPE = 32
V_HEAD = 64
QK_HEAD = QK_NOPE + QK_ROPE
Q_LORA = 384
KV_LORA = 256
ATTN_WIDTH = N_HEADS * V_HEAD
CONV_CH = D_MODEL - ATTN_WIDTH
CONV_W = 3
D_FF = 4 * D_MODEL
ROPE_THETA = 10000.0
EPS = 1e-6
PAGE_SIZE = 128
ATTN_SCALE = QK_HEAD ** -0.5
LOG2_E = 1.4426950408889634

LANES = 128
SUBLANES = 8
SLOT = LANES
HEADS_W = N_HEADS * SLOT
HALF_ROPE = QK_ROPE // 2

C_CQ = 0
C_CKV = C_CQ + Q_LORA
C_GB = C_CKV + KV_LORA
C_GC = C_GB + CONV_CH
C_HC = C_GC + CONV_CH
C_KPE = C_HC + CONV_CH
C_KPR = C_KPE + SLOT
IN_W = C_KPR + SLOT

ROW_TILE = 512
ATT_TQ = 256
ATT_TK = 256
FF_CHUNK = 1024
LAST_CHUNK_PAGES = 32
DMA_GROUP = 8
DMA_ISSUE_PAGES = 4
TAIL_KEYS = PAGE_SIZE
VMEM_LIMIT = 52 * 1024 * 1024

BF16 = jnp.bfloat16
F32 = jnp.float32

_NT = (((1,), (1,)), ((), ()))


def _dot(a, b):
    return jnp.dot(a, b, preferred_element_type=F32)


def _dot_nt(a, b):
    return lax.dot_general(a, b, _NT, preferred_element_type=F32)


def _rms(x, g):
    return x * lax.rsqrt(jnp.mean(x * x, axis=-1, keepdims=True) + EPS) * g


def _const_spec(shape):
    nd = len(shape)
    return pl.BlockSpec(shape, lambda *_: (0,) * nd, pipeline_mode=pl.Buffered(1))


def _mixer_kernel(*refs, rows, sample_mode):
    it = iter(refs)
    x_ref, cos_ref, sin_ref = next(it), next(it), next(it)
    if sample_mode:
        m1_ref, m2_ref, s1_ref, s2_ref = next(it), next(it), next(it), next(it)
    else:
        tail_in_ref = next(it)
    (g_mix_ref, w_in_ref, g_ql_ref, w_uq_ref, w_uqr_ref, g_kvl_ref, w_uk_ref, w_uv_ref,
     gq_ref, gk_ref, cw_ref, cb_ref, g_conv_ref) = (next(it) for _ in range(13))
    if sample_mode:
        (qk_ref, qabs_ref, k_ref, v_ref, lat_ref, kpe_ref, convn_ref, u_ref) = (next(it) for _ in range(8))
    else:
        (q_ref, k_ref, v_ref, lat_ref, kpe_ref, convn_ref, utail_ref) = (next(it) for _ in range(7))
    ubuf = next(it)

    hn = _rms(x_ref[...], g_mix_ref[...]).astype(BF16)
    cos = cos_ref[...]
    sin = sin_ref[...]

    cq = _dot(hn, w_in_ref[:, C_CQ:C_CKV])
    cqn = _rms(cq, g_ql_ref[...]).astype(BF16)
    gq = gq_ref[...]
    gk = gk_ref[...]
    for h in range(N_HEADS):
        sl = slice(h * SLOT, (h + 1) * SLOT)
        qh = _dot(cqn, w_uq_ref[:, sl]) * cos + _dot(cqn, w_uqr_ref[:, sl]) * sin
        ss = jnp.sum(qh * qh, axis=-1, keepdims=True)
        qh = qh * lax.rsqrt(ss * (1.0 / QK_HEAD) + EPS) * gq
        if sample_mode:
            qkh = (qh * gk).astype(BF16)
            qk_ref[:, sl] = qkh
            qabs_ref[:, h * KV_LORA:(h + 1) * KV_LORA] = _dot_nt(qkh, w_uk_ref[:, sl]).astype(BF16)
        else:
            q_ref[:, sl] = qh.astype(BF16)

    ckv = _dot(hn, w_in_ref[:, C_CKV:C_GB])
    lat = _rms(ckv, g_kvl_ref[...])
    lat_ref[...] = lat
    lat_b = lat.astype(BF16)
    zk = _dot(hn, w_in_ref[:, C_KPE:IN_W])
    krot = zk[:, :SLOT] * cos + zk[:, SLOT:] * sin
    kpe_ref[...] = krot[:, QK_NOPE:QK_HEAD]
    ss_rot = jnp.sum(krot * krot, axis=-1, keepdims=True)
    for h in range(N_HEADS):
        sl = slice(h * SLOT, (h + 1) * SLOT)
        kn = _dot(lat_b, w_uk_ref[:, sl])
        ss = jnp.sum(kn * kn, axis=-1, keepdims=True) + ss_rot
        kh = (kn + krot) * lax.rsqrt(ss * (1.0 / QK_HEAD) + EPS) * gk
        k_ref[:, sl] = kh.astype(BF16)
    v_ref[...] = _dot(lat_b, w_uv_ref[...]).astype(BF16)

    gc = _dot(hn, w_in_ref[:, C_GC:C_HC])
    hc = _dot(hn, w_in_ref[:, C_HC:C_KPE])
    u = gc * hc
    if sample_mode:
        ubuf[0:SUBLANES, :] = jnp.zeros((SUBLANES, CONV_CH), F32)
    else:
        t = pl.program_id(1)

        @pl.when(t == 0)
        def _():
            ubuf[0:SUBLANES, :] = tail_in_ref[...]

        @pl.when(t != 0)
        def _():
            ubuf[0:SUBLANES, :] = ubuf[rows:rows + SUBLANES, :]
    ubuf[SUBLANES:SUBLANES + rows, :] = u
    u1 = ubuf[SUBLANES - 1:SUBLANES - 1 + rows, :]
    u2 = ubuf[SUBLANES - 2:SUBLANES - 2 + rows, :]
    if sample_mode:
        u1 = u1 * m1_ref[...] + s1_ref[...]
        u2 = u2 * m2_ref[...] + s2_ref[...]
        u_ref[...] = u
    else:
        utail_ref[...] = u[rows - SUBLANES:, :]
    y = cb_ref[...] + u2 * cw_ref[0:1, :] + u1 * cw_ref[1:2, :] + u * cw_ref[2:3, :]
    gb = _dot(hn, w_in_ref[:, C_GB:C_GC])
    convn_ref[...] = _rms(gb * y, g_conv_ref[...]).astype(BF16)


def _mixer_call(x, cos_t, sin_t, conv_in, weights, *, sample_mode):
    wspecs = [_const_spec(w.shape) for w in weights]
    if sample_mode:
        rows = x.shape[0]
        grid = (1,)
        row = lambda w: pl.BlockSpec((rows, w), lambda i: (0, 0))
        m1, m2, s1, s2 = conv_in
        in_specs = [row(D_MODEL), row(SLOT), row(SLOT), row(1), row(1), row(CONV_CH), row(CONV_CH)]
        args = [x, cos_t, sin_t, m1, m2, s1, s2]
        widths = [(HEADS_W, BF16), (N_HEADS * KV_LORA, BF16), (HEADS_W, BF16), (ATTN_WIDTH, BF16),
                  (KV_LORA, F32), (QK_ROPE, F32), (CONV_CH, BF16), (CONV_CH, F32)]
        out_shape = [jax.ShapeDtypeStruct((rows, w), d) for w, d in widths]
        out_specs = [row(w) for w, _ in widths]
        sem = ("arbitrary",)
    else:
        nb, seq, _ = x.shape
        rows = ROW_TILE
        nt = seq // rows
        grid = (nb, nt)
        row3 = lambda w: pl.BlockSpec((None, rows, w), lambda b, t: (b, t, 0))
        tab = pl.BlockSpec((rows, SLOT), lambda b, t: (t, 0))
        in_specs = [row3(D_MODEL), tab, tab, _const_spec(conv_in.shape)]
        args = [x, cos_t, sin_t, conv_in]
        widths = [(HEADS_W, BF16), (HEADS_W, BF16), (ATTN_WIDTH, BF16), (KV_LORA, F32),
                  (QK_ROPE, F32), (CONV_CH, BF16)]
        out_shape = [jax.ShapeDtypeStruct((nb, seq, w), d) for w, d in widths]
        out_shape.append(jax.ShapeDtypeStruct((nb, nt, SUBLANES, CONV_CH), F32))
        out_specs = [row3(w) for w, _ in widths]
        out_specs.append(pl.BlockSpec((None, None, SUBLANES, CONV_CH), lambda b, t: (b, t, 0, 0)))
        sem = ("arbitrary", "arbitrary")
    return pl.pallas_call(
        functools.partial(_mixer_kernel, rows=rows, sample_mode=sample_mode),
        grid=grid,
        in_specs=in_specs + wspecs,
        out_specs=out_specs,
        out_shape=out_shape,
        scratch_shapes=[pltpu.VMEM((rows + 2 * SUBLANES, CONV_CH), F32)],
        compiler_params=pltpu.CompilerParams(dimension_semantics=sem, vmem_limit_bytes=VMEM_LIMIT),
        name="mixer_sample" if sample_mode else "mixer_prompt",
    )(*args, *weights)


def _softmax_update(s, m_ref, l_ref, acc_ref, pv):
    n = s.shape[1] // LANES
    m_old = m_ref[...]
    m_new = jnp.maximum(m_old, jnp.max(s, axis=-1, keepdims=True))
    alpha = jnp.exp2(m_old - m_new)
    ps = [jnp.exp2(s[:, c * LANES:(c + 1) * LANES] - m_new) for c in range(n)]
    l_ref[...] = alpha * l_ref[...] + functools.reduce(lambda a, b: a + b, ps)
    p = ps[0] if n == 1 else jnp.concatenate(ps, axis=1)
    w = acc_ref.shape[-1] // LANES
    alpha_w = alpha if w == 1 else jnp.concatenate([alpha] * w, axis=1)
    acc_ref[...] = alpha_w * acc_ref[...] + pv(p.astype(BF16))
    m_ref[...] = m_new


def _prompt_attn_kernel(q_ref, k_ref, v_ref, km_ref, vm_ref, o_ref, m_sc, l_sc, acc_sc):
    seq = q_ref.shape[0]
    half = ATT_TQ // 2
    lane = lax.broadcasted_iota(jnp.int32, (ATT_TQ, SLOT), 1)
    m_sc[...] = jnp.full(m_sc.shape, -jnp.inf, F32)
    l_sc[...] = jnp.zeros(l_sc.shape, F32)
    acc_sc[...] = jnp.zeros(acc_sc.shape, F32)

    def step(qi, hh, r0, nr, k_blk, v_blk, mask):
        sl = slice(hh * SLOT, (hh + 1) * SLOT)
        rows = pl.ds(r0, nr)
        s = _dot_nt(q_ref[pl.ds(qi * ATT_TQ + r0, nr), sl], k_blk[:, sl])
        if mask is not None:
            s = jnp.where(mask, s, -jnp.inf)
        _softmax_update(s, m_sc.at[qi, hh, rows], l_sc.at[qi, hh, rows], acc_sc.at[qi, hh, rows],
                        lambda p: _dot(p, v_blk))

    meta = lane < N_META
    col_h = lax.broadcasted_iota(jnp.int32, (half, half), 1)
    row_h = lax.broadcasted_iota(jnp.int32, (half, half), 0)
    col_f = lax.broadcasted_iota(jnp.int32, (half, ATT_TK), 1)
    row_f = lax.broadcasted_iota(jnp.int32, (half, ATT_TK), 0)
    for qi in range(seq // ATT_TQ):
        for hh in range(2):
            step(qi, hh, 0, ATT_TQ, km_ref[...], vm_ref[...], meta)
        for ki in range(qi):
            ks = slice(ki * ATT_TK, (ki + 1) * ATT_TK)
            for hh in range(2):
                step(qi, hh, 0, ATT_TQ, k_ref[ks, :], v_ref[ks, :], None)
        d0 = qi * ATT_TK
        for hh in range(2):
            step(qi, hh, 0, half, k_ref[d0:d0 + half, :], v_ref[d0:d0 + half, :], col_h <= row_h)
            step(qi, hh, half, half, k_ref[d0:d0 + ATT_TK, :], v_ref[d0:d0 + ATT_TK, :],
                 col_f <= row_f + half)
        outs = [acc_sc[qi, hh] / jnp.sum(l_sc[qi, hh], axis=-1, keepdims=True) for hh in range(2)]
        o_ref[qi * ATT_TQ:(qi + 1) * ATT_TQ, :] = (
            jnp.where(lane < V_HEAD, outs[0], outs[1]).astype(o_ref.dtype))


def _prompt_attn_call(q, k, v, k_meta, v_meta):
    nb, seq, _ = q.shape
    assert ATT_TQ == ATT_TK and seq % ATT_TQ == 0
    grid = (nb, N_HEADS // 2)
    return pl.pallas_call(
        _prompt_attn_kernel,
        grid=grid,
        in_specs=[
            pl.BlockSpec((None, seq, 2 * SLOT), lambda b, p: (b, 0, p)),
            pl.BlockSpec((None, seq, 2 * SLOT), lambda b, p: (b, 0, p)),
            pl.BlockSpec((None, seq, SLOT), lambda b, p: (b, 0, p)),
            pl.BlockSpec((LANES, 2 * SLOT), lambda b, p: (0, p)),
            pl.BlockSpec((LANES, SLOT), lambda b, p: (0, p)),
        ],
        out_specs=pl.BlockSpec((None, seq, SLOT), lambda b, p: (b, 0, p)),
        out_shape=jax.ShapeDtypeStruct((nb, seq, ATTN_WIDTH), BF16),
        scratch_shapes=[pltpu.VMEM((seq // ATT_TQ, 2, ATT_TQ, LANES), F32)] * 3,
        compiler_params=pltpu.CompilerParams(
            dimension_semantics=("arbitrary", "arbitrary"),
            vmem_limit_bytes=VMEM_LIMIT),
        name="prompt_attn",
    )(q, k, v, k_meta, v_meta)


def _sample_attn_kernel(pt_ref, qabs_ref, qpe_ref, wukt_ref, wuv_ref, tlat_ref, tkpe_ref,
                        lat_hbm, kpe_hbm, o_ref,
                        m_sc, l_sc, acc_sc, latb_sc, lhs_sc, lat_buf, kpe_buf, sems):
    n_pages = lat_buf.shape[1]
    chunk_pages = (n_pages - LAST_CHUNK_PAGES, LAST_CHUNK_PAGES)
    b = pl.program_id(0)
    last = pl.num_programs(0) - 1
    slot = b % 2
    nq = qabs_ref.shape[0]
    reps = nq // N_HEADS
    nkn = N_HEADS * QK_NOPE
    qpe = qpe_ref[...]

    ngroup = lat_buf.shape[1] // DMA_GROUP

    def group_copies(bb, sl, gg):
        copies = []
        for j in range(gg * DMA_GROUP, (gg + 1) * DMA_GROUP):
            page = pt_ref[bb, j]
            copies.append(pltpu.make_async_copy(lat_hbm.at[page], lat_buf.at[sl, j], sems.at[0, sl]))
            copies.append(pltpu.make_async_copy(kpe_hbm.at[page], kpe_buf.at[sl, j], sems.at[1, sl]))
        return copies

    @pl.when(b == 0)
    def _():
        lhs_sc[0:nkn, :] = wukt_ref[...]
        for gg in range(ngroup):
            for cp in group_copies(0, 0, gg):
                cp.start()

    nxt = jnp.minimum(b + 1, last)

    lhs_sc[nkn:nkn + nq, :] = qabs_ref[...]
    m_sc[...] = jnp.full(m_sc.shape, -jnp.inf, F32)
    l_sc[...] = jnp.zeros(l_sc.shape, F32)
    acc_sc[...] = jnp.zeros(acc_sc.shape, F32)

    def scores(lat_b, kpe_t):
        nk = lat_b.shape[0]
        full = _dot_nt(lhs_sc[...], lat_b)
        knt = full[:nkn]
        nsum = jnp.sum((knt * knt).reshape(QK_NOPE, N_HEADS, nk), axis=0)
        rsum = jnp.sum(kpe_t * kpe_t, axis=0, keepdims=True)
        r = lax.rsqrt((nsum + rsum) * (1.0 / QK_HEAD) + EPS)
        s = full[nkn:] + _dot(qpe, kpe_t.astype(BF16))
        return s * jnp.concatenate([r] * reps, axis=0)

    for gg in range(ngroup):
        for cp in group_copies(b, slot, gg):
            cp.wait()

    tail_b = tlat_ref[...].astype(BF16)
    s = scores(tail_b, tkpe_ref[...])
    key = lax.broadcasted_iota(jnp.int32, s.shape, 1)
    qt = lax.broadcasted_iota(jnp.int32, s.shape, 0) // N_HEADS
    s_tail = jnp.where(key <= qt, s, -jnp.inf)

    p0 = 0
    for npg in chunk_pages:
        rows = slice(p0 * PAGE_SIZE, (p0 + npg) * PAGE_SIZE)
        first = p0 == 0
        parts = [s_tail] if first else []
        for j in range(p0, p0 + npg, 2):
            for jj in (j, j + 1):
                latb_sc[jj * PAGE_SIZE:(jj + 1) * PAGE_SIZE, :] = lat_buf[slot, jj].astype(BF16)
            kpe_t = jnp.concatenate([kpe_buf[slot, j], kpe_buf[slot, j + 1]], axis=1)
            parts.append(scores(latb_sc[j * PAGE_SIZE:(j + 2) * PAGE_SIZE, :], kpe_t))
            if (j + 2) % DMA_ISSUE_PAGES == 0 and (j + 2) // DMA_ISSUE_PAGES <= ngroup:
                for cp in group_copies(nxt, 1 - slot, (j + 2) // DMA_ISSUE_PAGES - 1):
                    cp.start()
        if first:
            pv = lambda p: _dot(p[:, :TAIL_KEYS], tail_b) + _dot(p[:, TAIL_KEYS:], latb_sc[rows, :])
        else:
            pv = lambda p: _dot(p, latb_sc[rows, :])
        _softmax_update(jnp.concatenate(parts, axis=1), m_sc, l_sc, acc_sc, pv)
        p0 += npg

    o_lat = (acc_sc[...] / jnp.sum(l_sc[...], axis=-1, keepdims=True)).astype(BF16)
    full = _dot(o_lat, wuv_ref[...])
    row_h = lax.broadcasted_iota(jnp.int32, full.shape, 0) % N_HEADS
    col_h = lax.broadcasted_iota(jnp.int32, full.shape, 1) // V_HEAD
    own = jnp.where(row_h == col_h, full, 0.0)
    o_ref[...] = jnp.sum(own.reshape(reps, N_HEADS, ATTN_WIDTH), axis=1)

    @pl.when(b == last)
    def _():
        for gg in range(ngroup):
            for cp in group_copies(last, 1 - slot, gg):
                cp.wait()


def _sample_attn_call(page_table, qabs, qpe, w_ukt, w_uv, tail_lat, tail_kpe_t, cache_lat, cache_kpe_t):
    nseq, n_pages = page_table.shape
    nq = qabs.shape[1]
    assert n_pages % DMA_GROUP == 0 and DMA_GROUP % 2 == 0 and DMA_ISSUE_PAGES % 2 == 0
    assert DMA_ISSUE_PAGES <= DMA_GROUP
    assert 0 < LAST_CHUNK_PAGES < n_pages and LAST_CHUNK_PAGES % 2 == 0
    grid = (nseq,)

    per_seq = lambda rows, width: pl.BlockSpec((None, rows, width), lambda b, pt: (b, 0, 0))
    whole = lambda shape: pl.BlockSpec(shape, lambda b, pt: (0,) * len(shape),
                                       pipeline_mode=pl.Buffered(1))
    in_specs = [per_seq(nq, KV_LORA), per_seq(nq, QK_ROPE), whole(w_ukt.shape), whole(w_uv.shape),
                per_seq(TAIL_KEYS, KV_LORA), per_seq(QK_ROPE, TAIL_KEYS),
                pl.BlockSpec(memory_space=pl.ANY), pl.BlockSpec(memory_space=pl.ANY)]
    reps = nq // N_HEADS
    return pl.pallas_call(
        _sample_attn_kernel,
        grid_spec=pltpu.PrefetchScalarGridSpec(
            num_scalar_prefetch=1,
            grid=grid,
            in_specs=in_specs,
            out_specs=pl.BlockSpec((None, reps, ATTN_WIDTH), lambda b, pt: (b, 0, 0)),
            scratch_shapes=[pltpu.VMEM((nq, LANES), F32), pltpu.VMEM((nq, LANES), F32),
                            pltpu.VMEM((nq, KV_LORA), F32),
                            pltpu.VMEM((n_pages * PAGE_SIZE, KV_LORA), BF16),
                            pltpu.VMEM((N_HEADS * QK_NOPE + nq, KV_LORA), BF16),
                            pltpu.VMEM((2, n_pages, PAGE_SIZE, KV_LORA), F32),
                            pltpu.VMEM((2, n_pages, QK_ROPE, PAGE_SIZE), F32),
                            pltpu.SemaphoreType.DMA((2, 2))],
        ),
        out_shape=jax.ShapeDtypeStruct((nseq, reps, ATTN_WIDTH), F32),
        compiler_params=pltpu.CompilerParams(dimension_semantics=("arbitrary",),
                                             vmem_limit_bytes=VMEM_LIMIT),
        name="sample_attn",
    )(page_table, qabs, qpe, w_ukt, w_uv, tail_lat, tail_kpe_t, cache_lat, cache_kpe_t)


def _merge_ffn_kernel(x_ref, attn_ref, convn_ref, g_attn_ref, w_o_ref, g_ffn_ref, w_up_ref,
                      w_down_ref, y_ref):
    an = _rms(attn_ref[...].astype(F32), g_attn_ref[...]).astype(BF16)
    x1 = x_ref[...] + (_dot(an, w_o_ref[0:ATTN_WIDTH, :]) + _dot(convn_ref[...], w_o_ref[ATTN_WIDTH:, :]))
    hf = _rms(x1, g_ffn_ref[...]).astype(BF16)
    ffn = None
    for c in range(D_FF // FF_CHUNK):
        cs = slice(c * FF_CHUNK, (c + 1) * FF_CHUNK)
        up = jnp.maximum(_dot(hf, w_up_ref[:, cs]), 0.0)
        part = _dot((up * up).astype(BF16), w_down_ref[cs, :])
        ffn = part if ffn is None else ffn + part
    y_ref[...] = x1 + ffn


def _merge_ffn_call(x, attn, convn, weights, rows):
    n = x.shape[0]
    row = lambda w: pl.BlockSpec((rows, w), lambda i: (i, 0))
    return pl.pallas_call(
        _merge_ffn_kernel,
        grid=(n // rows,),
        in_specs=[row(D_MODEL), row(ATTN_WIDTH), row(CONV_CH)] + [_const_spec(w.shape) for w in weights],
        out_specs=row(D_MODEL),
        out_shape=jax.ShapeDtypeStruct((n, D_MODEL), F32),
        compiler_params=pltpu.CompilerParams(dimension_semantics=("arbitrary",),
                                             vmem_limit_bytes=VMEM_LIMIT),
        name="merge_ffn",
    )(x, attn, convn, *weights)


def _slots(w, width):
    k = w.shape[0]
    w = w.reshape(k, N_HEADS, width)
    return jnp.pad(w, ((0, 0), (0, 0), (0, SLOT - width))).reshape(k, HEADS_W)


def _rot_partner(w):
    return jnp.concatenate([-w[..., HALF_ROPE:], w[..., :HALF_ROPE]], axis=-1)


def _rope_slot(w):
    return jnp.pad(w, ((0, 0), (QK_NOPE, SLOT - QK_HEAD)))


def _layer_weights(w_in, q_lora_g, kv_lora_g, w_uq, w_ukv, q_norm_g, k_norm_g, conv_w, conv_b,
                   conv_out_g, norm_mix_g):
    o1 = Q_LORA
    o2 = o1 + KV_LORA
    o3 = o2 + QK_ROPE
    w_kpe = w_in[:, o2:o3]
    w_in_p = jnp.concatenate(
        [w_in[:, :o2], w_in[:, o3:], _rope_slot(w_kpe), _rope_slot(_rot_partner(w_kpe))],
        axis=1).astype(BF16)
    uq = w_uq.reshape(Q_LORA, N_HEADS, QK_HEAD)
    w_uq_p = _slots(w_uq, QK_HEAD).astype(BF16)
    uq_rot = jnp.pad(_rot_partner(uq[..., QK_NOPE:]), ((0, 0), (0, 0), (QK_NOPE, SLOT - QK_HEAD)))
    w_uq_r = uq_rot.reshape(Q_LORA, HEADS_W).astype(BF16)
    ukv = w_ukv.reshape(KV_LORA, N_HEADS, QK_NOPE + V_HEAD)
    w_uk_p = _slots(ukv[..., :QK_NOPE].reshape(KV_LORA, N_HEADS * QK_NOPE), QK_NOPE).astype(BF16)
    w_uv = ukv[..., QK_NOPE:].reshape(KV_LORA, ATTN_WIDTH).astype(BF16)
    w_ukt = jnp.transpose(ukv[..., :QK_NOPE], (2, 1, 0)).reshape(N_HEADS * QK_NOPE, KV_LORA).astype(BF16)
    pad_g = lambda g: jnp.pad(g, (0, SLOT - QK_HEAD))[None, :]
    gq = pad_g(q_norm_g) * (ATTN_SCALE * LOG2_E)
    gk = pad_g(k_norm_g)
    mixer = (norm_mix_g[None, :], w_in_p, q_lora_g[None, :], w_uq_p, w_uq_r, kv_lora_g[None, :],
             w_uk_p, w_uv, gq, gk, conv_w, conv_b[None, :], conv_out_g[None, :])
    return mixer, w_ukt, w_uv


def _rope_slot_tables(pos):
    inv_freq = ROPE_THETA ** (-(jnp.arange(0, QK_ROPE, 2, dtype=F32) / QK_ROPE))
    ang = pos.astype(F32)[:, None] * inv_freq[None, :]
    n = pos.shape[0]
    cos2 = jnp.concatenate([jnp.cos(ang)] * 2, axis=1)
    sin2 = jnp.concatenate([jnp.sin(ang)] * 2, axis=1)
    cos_t = jnp.concatenate([jnp.ones((n, QK_NOPE), F32), cos2, jnp.zeros((n, SLOT - QK_HEAD), F32)], axis=1)
    sin_t = jnp.pad(sin2, ((0, 0), (QK_NOPE, SLOT - QK_HEAD)))
    return cos_t, sin_t


def kernel(x_prompt, x_sample, cache_kv_latent, cache_k_rope, state_conv, page_table, meta_tokens,
           norm_mix_g, w_in, q_lora_g, kv_lora_g, w_uq, w_ukv, q_norm_g, k_norm_g, conv_w, conv_b,
           attn_out_g, conv_out_g, w_o, norm_ffn_g, w_up, w_down):
    depth = w_in.shape[0]
    assert depth == 1, "the prompt and sample streams are chained for a single layer"
    nb, seq, _ = x_prompt.shape
    nseq, dec, _ = x_sample.shape
    past = page_table.shape[1] * PAGE_SIZE
    l = 0

    mixer_w, w_ukt, w_uv = _layer_weights(w_in[l], q_lora_g[l], kv_lora_g[l], w_uq[l], w_ukv[l],
                                          q_norm_g[l], k_norm_g[l], conv_w[l], conv_b[l],
                                          conv_out_g[l], norm_mix_g[l])
    ffn_w = (attn_out_g[l][None, :], w_o[l].astype(BF16), norm_ffn_g[l][None, :],
             w_up[l].astype(BF16), w_down[l].astype(BF16))

    ns = nseq * dec
    xs_rows = jnp.concatenate([x_sample.reshape(ns, D_MODEL), meta_tokens.astype(F32)], axis=0)
    pos_s = jnp.concatenate([jnp.tile(past + jnp.arange(dec), nseq), jnp.arange(N_META)])
    cos_s, sin_s = _rope_slot_tables(pos_s)
    t_in_seq = jnp.concatenate([jnp.tile(jnp.arange(dec), nseq), jnp.arange(N_META)])
    m1 = (t_in_seq >= 1).astype(F32)[:, None]
    m2 = (t_in_seq >= 2).astype(F32)[:, None]
    st = state_conv[l].astype(F32)
    zrow = jnp.zeros((nseq, 1, CONV_CH), F32)
    s1 = jnp.concatenate([st[:, 1:2], zrow, zrow, zrow], axis=1)[:, :dec]
    s2 = jnp.concatenate([st[:, 0:1], st[:, 1:2], zrow, zrow], axis=1)[:, :dec]
    zmeta = jnp.zeros((N_META, CONV_CH), F32)
    s1 = jnp.concatenate([s1.reshape(ns, CONV_CH), zmeta], axis=0)
    s2 = jnp.concatenate([s2.reshape(ns, CONV_CH), zmeta], axis=0)
    (qk_s, qabs_s, k_s, v_s, lat_s, kpe_s, convn_s, u_s) = _mixer_call(
        xs_rows, cos_s, sin_s, (m1, m2, s1, s2), mixer_w, sample_mode=True)

    pad_meta = lambda a: jnp.pad(a[ns:], ((0, LANES - N_META), (0, 0)))
    k_meta, v_meta = pad_meta(k_s), pad_meta(v_s)
    lat_meta, kpe_meta, u_meta = lat_s[ns:], kpe_s[ns:], u_s[ns:]

    cos_p, sin_p = _rope_slot_tables(N_META + jnp.arange(seq))
    q_p, k_p, v_p, lat_p, kpe_p, convn_p, utail_p = _mixer_call(
        x_prompt, cos_p, sin_p, u_meta[N_META - SUBLANES:], mixer_w, sample_mode=False)
    attn_p = _prompt_attn_call(q_p, k_p, v_p, k_meta, v_meta)
    y_prompt = _merge_ffn_call(x_prompt.reshape(nb * seq, D_MODEL), attn_p.reshape(nb * seq, ATTN_WIDTH),
                               convn_p.reshape(nb * seq, CONV_CH), ffn_w, ROW_TILE)
    y_prompt = y_prompt.reshape(nb, seq, D_MODEL)
    bcast = lambda a: jnp.broadcast_to(a[None], (nb,) + a.shape)
    new_lat_prompt = jnp.concatenate([bcast(lat_meta), lat_p], axis=1)[None]
    new_kpe_prompt = jnp.concatenate([bcast(kpe_meta), kpe_p], axis=1)[None]
    new_conv_prompt = utail_p[:, -1, SUBLANES - (CONV_W - 1):][None]

    nq = dec * N_HEADS
    qabs = qabs_s[:ns].reshape(nseq, nq, KV_LORA)
    qpe = qk_s[:ns].reshape(ns, N_HEADS, SLOT)[:, :, QK_NOPE:QK_HEAD].reshape(nseq, nq, QK_ROPE)
    assert dec <= TAIL_KEYS
    pad_tail = lambda a: jnp.pad(a[:ns].reshape(nseq, dec, -1), ((0, 0), (0, TAIL_KEYS - dec), (0, 0)))
    attn_s = _sample_attn_call(page_table, qabs, qpe, w_ukt, w_uv, pad_tail(lat_s),
                               jnp.swapaxes(pad_tail(kpe_s), 1, 2),
                               cache_kv_latent[l], jnp.swapaxes(cache_k_rope[l], 1, 2))
    y_sample = _merge_ffn_call(x_sample.reshape(ns, D_MODEL), attn_s.reshape(ns, ATTN_WIDTH),
                               convn_s[:ns], ffn_w, ns)
    y_sample = y_sample.reshape(nseq, dec, D_MODEL)
    new_lat_sample = lat_s[:ns].reshape(nseq, dec, KV_LORA)[None]
    new_kpe_sample = kpe_s[:ns].reshape(nseq, dec, QK_ROPE)[None]
    us = jnp.concatenate([st, u_s[:ns].reshape(nseq, dec, CONV_CH)], axis=1)
    new_conv_sample = us[:, -(CONV_W - 1):][None]

    return (y_prompt, y_sample, new_lat_prompt, new_kpe_prompt, new_conv_prompt,
            new_lat_sample, new_kpe_sample, new_conv_sample)
```

```python
import functools

import jax
import jax.numpy as jnp
from jax import lax
from jax.experimental import pallas as pl
from jax.experimental.pallas import tpu as pltpu

D_MODEL = 1024
N_META = 16
N_HEADS = 8
QK_NOPE = 64
QK_ROPE = 32
V_HEAD = 64
QK_HEAD = QK_NOPE + QK_ROPE
Q_LORA = 384
KV_LORA = 256
ATTN_WIDTH = N_HEADS * V_HEAD
CONV_CH = D_MODEL - ATTN_WIDTH
CONV_W = 3
D_FF = 4 * D_MODEL
ROPE_THETA = 10000.0
EPS = 1e-6
PAGE_SIZE = 128
ATTN_SCALE = QK_HEAD ** -0.5
LOG2_E = 1.4426950408889634

LANES = 128
SUBLANES = 8
SLOT = LANES
HEADS_W = N_HEADS * SLOT
HALF_ROPE = QK_ROPE // 2

C_CQ = 0
C_CKV = C_CQ + Q_LORA
C_GB = C_CKV + KV_LORA
C_GC = C_GB + CONV_CH
C_HC = C_GC + CONV_CH
C_KPE = C_HC + CONV_CH
C_KPR = C_KPE + SLOT
IN_W = C_KPR + SLOT

ROW_TILE = 512
ATT_TQ = 256
ATT_TK = 256
ATT_LOOKAHEAD = 4
FF_CHUNK = 1024
CHUNK_PAGES = 8
DMA_GROUP = 8
DMA_ISSUE_PAGES = 4
TAIL_KEYS = PAGE_SIZE
VMEM_LIMIT = 52 * 1024 * 1024

BF16 = jnp.bfloat16
F32 = jnp.float32

_NT = (((1,), (1,)), ((), ()))


def _dot(a, b):
    return jnp.dot(a, b, preferred_element_type=F32)


def _dot_nt(a, b):
    return lax.dot_general(a, b, _NT, preferred_element_type=F32)


def _rms(x, g):
    return x * lax.rsqrt(jnp.mean(x * x, axis=-1, keepdims=True) + EPS) * g


def _const_spec(shape):
    nd = len(shape)
    return pl.BlockSpec(shape, lambda *_: (0,) * nd, pipeline_mode=pl.Buffered(1))


def _mixer_kernel(*refs, rows, sample_mode):
    it = iter(refs)
    x_ref, cos_ref, sin_ref = next(it), next(it), next(it)
    if sample_mode:
        m1_ref, m2_ref, s1_ref, s2_ref = next(it), next(it), next(it), next(it)
    else:
        tail_in_ref = next(it)
    (g_mix_ref, w_in_ref, g_ql_ref, w_uq_ref, w_uqr_ref, g_kvl_ref, w_uk_ref, w_uv_ref,
     gq_ref, gk_ref, cw_ref, cb_ref, g_conv_ref) = (next(it) for _ in range(13))
    if sample_mode:
        (qk_ref, qabs_ref, k_ref, v_ref, lat_ref, kpe_ref, convn_ref, u_ref) = (next(it) for _ in range(8))
    else:
        (q_ref, k_ref, v_ref, lat_ref, kpe_ref, convn_ref, utail_ref) = (next(it) for _ in range(7))
    ubuf = next(it)

    hn = _rms(x_ref[...], g_mix_ref[...]).astype(BF16)
    cos = cos_ref[...]
    sin = sin_ref[...]

    cq = _dot(hn, w_in_ref[:, C_CQ:C_CKV])
    ckv = _dot(hn, w_in_ref[:, C_CKV:C_GB])
    zk = _dot(hn, w_in_ref[:, C_KPE:IN_W])
    gc = _dot(hn, w_in_ref[:, C_GC:C_HC])
    hc = _dot(hn, w_in_ref[:, C_HC:C_KPE])
    gb = _dot(hn, w_in_ref[:, C_GB:C_GC])

    cqn = _rms(cq, g_ql_ref[...]).astype(BF16)
    gq = gq_ref[...]
    gk = gk_ref[...]
    for h in range(N_HEADS):
        sl = slice(h * SLOT, (h + 1) * SLOT)
        qh = _dot(cqn, w_uq_ref[:, sl]) * cos + _dot(cqn, w_uqr_ref[:, sl]) * sin
        ss = jnp.sum(qh * qh, axis=-1, keepdims=True)
        qh = qh * lax.rsqrt(ss * (1.0 / QK_HEAD) + EPS) * gq
        if sample_mode:
            qkh = (qh * gk).astype(BF16)
            qk_ref[:, sl] = qkh
            qabs_ref[:, h * KV_LORA:(h + 1) * KV_LORA] = _dot_nt(qkh, w_uk_ref[:, sl]).astype(BF16)
        else:
            q_ref[:, sl] = qh.astype(BF16)

    lat = _rms(ckv, g_kvl_ref[...])
    lat_ref[...] = lat
    lat_b = lat.astype(BF16)
    krot = zk[:, :SLOT] * cos + zk[:, SLOT:] * sin
    kpe_ref[...] = krot[:, QK_NOPE:QK_HEAD]
    ss_rot = jnp.sum(krot * krot, axis=-1, keepdims=True)
    for h in range(N_HEADS):
        sl = slice(h * SLOT, (h + 1) * SLOT)
        kn = _dot(lat_b, w_uk_ref[:, sl])
        ss = jnp.sum(kn * kn, axis=-1, keepdims=True) + ss_rot
        kh = (kn + krot) * lax.rsqrt(ss * (1.0 / QK_HEAD) + EPS) * gk
        k_ref[:, sl] = kh.astype(BF16)
    v_ref[...] = _dot(lat_b, w_uv_ref[...]).astype(BF16)

    u = gc * hc
    if sample_mode:
        ubuf[0:SUBLANES, :] = jnp.zeros((SUBLANES, CONV_CH), F32)
    else:
        t = pl.program_id(1)

        @pl.when(t == 0)
        def _():
            ubuf[0:SUBLANES, :] = tail_in_ref[...]

        @pl.when(t != 0)
        def _():
            ubuf[0:SUBLANES, :] = ubuf[rows:rows + SUBLANES, :]
    ubuf[SUBLANES:SUBLANES + rows, :] = u
    u1 = ubuf[SUBLANES - 1:SUBLANES - 1 + rows, :]
    u2 = ubuf[SUBLANES - 2:SUBLANES - 2 + rows, :]
    if sample_mode:
        u1 = u1 * m1_ref[...] + s1_ref[...]
        u2 = u2 * m2_ref[...] + s2_ref[...]
        u_ref[...] = u
    else:
        utail_ref[...] = u[rows - SUBLANES:, :]
    y = cb_ref[...] + u2 * cw_ref[0:1, :] + u1 * cw_ref[1:2, :] + u * cw_ref[2:3, :]
    convn_ref[...] = _rms(gb * y, g_conv_ref[...]).astype(BF16)


def _mixer_call(x, cos_t, sin_t, conv_in, weights, *, sample_mode):
    wspecs = [_const_spec(w.shape) for w in weights]
    if sample_mode:
        rows = x.shape[0]
        grid = (1,)
        row = lambda w: pl.BlockSpec((rows, w), lambda i: (0, 0))
        m1, m2, s1, s2 = conv_in
        in_specs = [row(D_MODEL), row(SLOT), row(SLOT), row(1), row(1), row(CONV_CH), row(CONV_CH)]
        args = [x, cos_t, sin_t, m1, m2, s1, s2]
        widths = [(HEADS_W, BF16), (N_HEADS * KV_LORA, BF16), (HEADS_W, BF16), (ATTN_WIDTH, BF16),
                  (KV_LORA, F32), (QK_ROPE, F32), (CONV_CH, BF16), (CONV_CH, F32)]
        out_shape = [jax.ShapeDtypeStruct((rows, w), d) for w, d in widths]
        out_specs = [row(w) for w, _ in widths]
        sem = ("arbitrary",)
    else:
        nb, seq, _ = x.shape
        rows = ROW_TILE
        nt = seq // rows
        grid = (nb, nt)
        row3 = lambda w: pl.BlockSpec((None, rows, w), lambda b, t: (b, t, 0))
        tab = pl.BlockSpec((rows, SLOT), lambda b, t: (t, 0))
        in_specs = [row3(D_MODEL), tab, tab, _const_spec(conv_in.shape)]
        args = [x, cos_t, sin_t, conv_in]
        widths = [(HEADS_W, BF16), (HEADS_W, BF16), (ATTN_WIDTH, BF16), (KV_LORA, F32),
                  (QK_ROPE, F32), (CONV_CH, BF16)]
        out_shape = [jax.ShapeDtypeStruct((nb, seq, w), d) for w, d in widths]
        out_shape.append(jax.ShapeDtypeStruct((nb, nt, SUBLANES, CONV_CH), F32))
        out_specs = [row3(w) for w, _ in widths]
        out_specs.append(pl.BlockSpec((None, None, SUBLANES, CONV_CH), lambda b, t: (b, t, 0, 0)))
        sem = ("arbitrary", "arbitrary")
    return pl.pallas_call(
        functools.partial(_mixer_kernel, rows=rows, sample_mode=sample_mode),
        grid=grid,
        in_specs=in_specs + wspecs,
        out_specs=out_specs,
        out_shape=out_shape,
        scratch_shapes=[pltpu.VMEM((rows + 2 * SUBLANES, CONV_CH), F32)],
        compiler_params=pltpu.CompilerParams(dimension_semantics=sem, vmem_limit_bytes=VMEM_LIMIT),
        name="mixer_sample" if sample_mode else "mixer_prompt",
    )(*args, *weights)


def _softmax_step(s, state, pv):
    m_old, l_old, acc_old = state
    n = s.shape[1] // LANES
    m_new = jnp.maximum(m_old, jnp.max(s, axis=-1, keepdims=True))
    alpha = jnp.exp2(m_old - m_new)
    ps = [jnp.exp2(s[:, c * LANES:(c + 1) * LANES] - m_new) for c in range(n)]
    l_new = alpha * l_old + functools.reduce(lambda a, b: a + b, ps)
    p = ps[0] if n == 1 else jnp.concatenate(ps, axis=1)
    w = acc_old.shape[-1] // LANES
    alpha_w = alpha if w == 1 else jnp.concatenate([alpha] * w, axis=1)
    return m_new, l_new, alpha_w * acc_old + pv(p.astype(BF16))


def _softmax_update(s, m_ref, l_ref, acc_ref, pv):
    m_ref[...], l_ref[...], acc_ref[...] = _softmax_step(s, (m_ref[...], l_ref[...], acc_ref[...]), pv)


def _prompt_attn_kernel(q_ref, k_ref, v_ref, km_ref, vm_ref, o_ref, m_sc, l_sc, acc_sc):
    seq = q_ref.shape[0]
    half = ATT_TQ // 2
    lane = lax.broadcasted_iota(jnp.int32, (ATT_TQ, SLOT), 1)
    m_sc[...] = jnp.full(m_sc.shape, -jnp.inf, F32)
    l_sc[...] = jnp.zeros(l_sc.shape, F32)
    acc_sc[...] = jnp.zeros(acc_sc.shape, F32)

    meta = lane < N_META
    col_h = lax.broadcasted_iota(jnp.int32, (half, half), 1)
    row_h = lax.broadcasted_iota(jnp.int32, (half, half), 0)
    col_f = lax.broadcasted_iota(jnp.int32, (half, ATT_TK), 1)
    row_f = lax.broadcasted_iota(jnp.int32, (half, ATT_TK), 0)

    steps = []
    for qi in range(seq // ATT_TQ):
        for hh in range(2):
            steps.append((qi, hh, 0, ATT_TQ, None, meta))
        for ki in range(qi):
            for hh in range(2):
                steps.append((qi, hh, 0, ATT_TQ, slice(ki * ATT_TK, (ki + 1) * ATT_TK), None))
        d0 = qi * ATT_TK
        for hh in range(2):
            steps.append((qi, hh, 0, half, slice(d0, d0 + half), col_h <= row_h))
            steps.append((qi, hh, half, half, slice(d0, d0 + ATT_TK), col_f <= row_f + half))
        steps.append((qi,))

    def scores(qi, hh, r0, nr, ks, mask):
        sl = slice(hh * SLOT, (hh + 1) * SLOT)
        k_blk = km_ref[:, sl] if ks is None else k_ref[ks, sl]
        s = _dot_nt(q_ref[pl.ds(qi * ATT_TQ + r0, nr), sl], k_blk)
        return s if mask is None else jnp.where(mask, s, -jnp.inf)

    def update(s, qi, hh, r0, nr, ks, mask):
        rows = pl.ds(r0, nr)
        v_blk = vm_ref[...] if ks is None else v_ref[ks, :]
        _softmax_update(s, m_sc.at[qi, hh, rows], l_sc.at[qi, hh, rows], acc_sc.at[qi, hh, rows],
                        lambda p: _dot(p, v_blk))

    def finalize(qi):
        outs = [acc_sc[qi, hh] / jnp.sum(l_sc[qi, hh], axis=-1, keepdims=True) for hh in range(2)]
        o_ref[qi * ATT_TQ:(qi + 1) * ATT_TQ, :] = (
            jnp.where(lane < V_HEAD, outs[0], outs[1]).astype(o_ref.dtype))

    pending = []
    for st in steps:
        pending.append((st, scores(*st) if len(st) > 1 else None))
        if len(pending) > ATT_LOOKAHEAD:
            d, s = pending.pop(0)
            finalize(*d) if s is None else update(s, *d)
    for d, s in pending:
        finalize(*d) if s is None else update(s, *d)


def _prompt_attn_call(q, k, v, k_meta, v_meta):
    nb, seq, _ = q.shape
    assert ATT_TQ == ATT_TK and seq % ATT_TQ == 0
    grid = (nb, N_HEADS // 2)
    return pl.pallas_call(
        _prompt_attn_kernel,
        grid=grid,
        in_specs=[
            pl.BlockSpec((None, seq, 2 * SLOT), lambda b, p: (b, 0, p)),
            pl.BlockSpec((None, seq, 2 * SLOT), lambda b, p: (b, 0, p)),
            pl.BlockSpec((None, seq, SLOT), lambda b, p: (b, 0, p)),
            pl.BlockSpec((LANES, 2 * SLOT), lambda b, p: (0, p)),
            pl.BlockSpec((LANES, SLOT), lambda b, p: (0, p)),
        ],
        out_specs=pl.BlockSpec((None, seq, SLOT), lambda b, p: (b, 0, p)),
        out_shape=jax.ShapeDtypeStruct((nb, seq, ATTN_WIDTH), BF16),
        scratch_shapes=[pltpu.VMEM((seq // ATT_TQ, 2, ATT_TQ, LANES), F32)] * 3,
        compiler_params=pltpu.CompilerParams(
            dimension_semantics=("arbitrary", "arbitrary"),
            vmem_limit_bytes=VMEM_LIMIT),
        name="prompt_attn",
    )(q, k, v, k_meta, v_meta)


def _sample_attn_kernel(pt_ref, qabs_ref, qpe_ref, wukt_ref, wuv_ref, tlat_ref, tkpe_ref,
                        lat_hbm, kpe_hbm, o_ref,
                        latb_sc, lhs_sc, lat_buf, kpe_buf, sems):
    n_pages = lat_buf.shape[1]
    chunk_pages = (CHUNK_PAGES,) * (n_pages // CHUNK_PAGES)
    b = pl.program_id(0)
    last = pl.num_programs(0) - 1
    slot = b % 2
    nq = qabs_ref.shape[0]
    reps = nq // N_HEADS
    nkn = N_HEADS * QK_NOPE
    qpe = qpe_ref[...]

    ngroup = lat_buf.shape[1] // DMA_GROUP

    def group_copies(bb, sl, gg):
        copies = []
        for j in range(gg * DMA_GROUP, (gg + 1) * DMA_GROUP):
            page = pt_ref[bb, j]
            copies.append(pltpu.make_async_copy(lat_hbm.at[page], lat_buf.at[sl, j], sems.at[0, sl]))
            copies.append(pltpu.make_async_copy(kpe_hbm.at[page], kpe_buf.at[sl, j], sems.at[1, sl]))
        return copies

    @pl.when(b == 0)
    def _():
        lhs_sc[0:nkn, :] = wukt_ref[...]
        for gg in range(ngroup):
            for cp in group_copies(0, 0, gg):
                cp.start()

    nxt = jnp.minimum(b + 1, last)

    lhs_sc[nkn:nkn + nq, :] = qabs_ref[...]
    state = (jnp.full((nq, LANES), -jnp.inf, F32), jnp.zeros((nq, LANES), F32),
             jnp.zeros((nq, KV_LORA), F32))

    def scores(lat_b, kpe_t):
        nk = lat_b.shape[0]
        full = _dot_nt(lhs_sc[...], lat_b)
        knt = full[:nkn]
        nsum = jnp.sum((knt * knt).reshape(QK_NOPE, N_HEADS, nk), axis=0)
        rsum = jnp.sum(kpe_t * kpe_t, axis=0, keepdims=True)
        r = lax.rsqrt((nsum + rsum) * (1.0 / QK_HEAD) + EPS)
        s = full[nkn:] + _dot(qpe, kpe_t.astype(BF16))
        return s * jnp.concatenate([r] * reps, axis=0)

    for gg in range(ngroup):
        for cp in group_copies(b, slot, gg):
            cp.wait()

    tail_b = tlat_ref[...].astype(BF16)
    s = scores(tail_b, tkpe_ref[...])
    key = lax.broadcasted_iota(jnp.int32, s.shape, 1)
    qt = lax.broadcasted_iota(jnp.int32, s.shape, 0) // N_HEADS
    s_tail = jnp.where(key <= qt, s, -jnp.inf)

    def chunk_scores(p0, npg):
        parts = [s_tail] if p0 == 0 else []
        for j in range(p0, p0 + npg, 2):
            for jj in (j, j + 1):
                latb_sc[jj * PAGE_SIZE:(jj + 1) * PAGE_SIZE, :] = lat_buf[slot, jj].astype(BF16)
            kpe_t = jnp.concatenate([kpe_buf[slot, j], kpe_buf[slot, j + 1]], axis=1)
            parts.append(scores(latb_sc[j * PAGE_SIZE:(j + 2) * PAGE_SIZE, :], kpe_t))
            if (j + 2) % DMA_ISSUE_PAGES == 0 and (j + 2) // DMA_ISSUE_PAGES <= ngroup:
                for cp in group_copies(nxt, 1 - slot, (j + 2) // DMA_ISSUE_PAGES - 1):
                    cp.start()
        return jnp.concatenate(parts, axis=1)

    def chunk_update(state, s, p0, npg):
        rows = slice(p0 * PAGE_SIZE, (p0 + npg) * PAGE_SIZE)
        if p0 == 0:
            pv = lambda p: _dot(p[:, :TAIL_KEYS], tail_b) + _dot(p[:, TAIL_KEYS:], latb_sc[rows, :])
        else:
            pv = lambda p: _dot(p, latb_sc[rows, :])
        return _softmax_step(s, state, pv)

    starts = [sum(chunk_pages[:i]) for i in range(len(chunk_pages))]
    pending = None
    for p0, npg in zip(starts, chunk_pages):
        s = chunk_scores(p0, npg)
        if pending is not None:
            state = chunk_update(state, *pending)
        pending = (s, p0, npg)
    state = chunk_update(state, *pending)

    _, l_fin, acc_fin = state
    o_lat = (acc_fin / jnp.sum(l_fin, axis=-1, keepdims=True)).astype(BF16)
    full = _dot(o_lat, wuv_ref[...])
    row_h = lax.broadcasted_iota(jnp.int32, full.shape, 0) % N_HEADS
    col_h = lax.broadcasted_iota(jnp.int32, full.shape, 1) // V_HEAD
    own = jnp.where(row_h == col_h, full, 0.0)
    o_ref[...] = jnp.sum(own.reshape(reps, N_HEADS, ATTN_WIDTH), axis=1)

    @pl.when(b == last)
    def _():
        for gg in range(ngroup):
            for cp in group_copies(last, 1 - slot, gg):
                cp.wait()


def _sample_attn_call(page_table, qabs, qpe, w_ukt, w_uv, tail_lat, tail_kpe_t, cache_lat, cache_kpe_t):
    nseq, n_pages = page_table.shape
    nq = qabs.shape[1]
    assert n_pages % DMA_GROUP == 0 and DMA_GROUP % 2 == 0 and DMA_ISSUE_PAGES % 2 == 0
    assert DMA_ISSUE_PAGES <= DMA_GROUP
    assert n_pages % CHUNK_PAGES == 0 and CHUNK_PAGES % 2 == 0
    grid = (nseq,)

    per_seq = lambda rows, width: pl.BlockSpec((None, rows, width), lambda b, pt: (b, 0, 0))
    whole = lambda shape: pl.BlockSpec(shape, lambda b, pt: (0,) * len(shape),
                                       pipeline_mode=pl.Buffered(1))
    in_specs = [per_seq(nq, KV_LORA), per_seq(nq, QK_ROPE), whole(w_ukt.shape), whole(w_uv.shape),
                per_seq(TAIL_KEYS, KV_LORA), per_seq(QK_ROPE, TAIL_KEYS),
                pl.BlockSpec(memory_space=pl.ANY), pl.BlockSpec(memory_space=pl.ANY)]
    reps = nq // N_HEADS
    return pl.pallas_call(
        _sample_attn_kernel,
        grid_spec=pltpu.PrefetchScalarGridSpec(
            num_scalar_prefetch=1,
            grid=grid,
            in_specs=in_specs,
            out_specs=pl.BlockSpec((None, reps, ATTN_WIDTH), lambda b, pt: (b, 0, 0)),
            scratch_shapes=[pltpu.VMEM((n_pages * PAGE_SIZE, KV_LORA), BF16),
                            pltpu.VMEM((N_HEADS * QK_NOPE + nq, KV_LORA), BF16),
                            pltpu.VMEM((2, n_pages, PAGE_SIZE, KV_LORA), F32),
                            pltpu.VMEM((2, n_pages, QK_ROPE, PAGE_SIZE), F32),
                            pltpu.SemaphoreType.DMA((2, 2))],
        ),
        out_shape=jax.ShapeDtypeStruct((nseq, reps, ATTN_WIDTH), F32),
        compiler_params=pltpu.CompilerParams(dimension_semantics=("arbitrary",),
                                             vmem_limit_bytes=VMEM_LIMIT),
        name="sample_attn",
    )(page_table, qabs, qpe, w_ukt, w_uv, tail_lat, tail_kpe_t, cache_lat, cache_kpe_t)


def _merge_ffn_kernel(x_ref, attn_ref, convn_ref, g_attn_ref, w_o_ref, g_ffn_ref, w_up_ref,
                      w_down_ref, y_ref):
    an = _rms(attn_ref[...].astype(F32), g_attn_ref[...]).astype(BF16)
    x1 = x_ref[...] + (_dot(an, w_o_ref[0:ATTN_WIDTH, :]) + _dot(convn_ref[...], w_o_ref[ATTN_WIDTH:, :]))
    hf = _rms(x1, g_ffn_ref[...]).astype(BF16)
    ffn = None
    for c in range(D_FF // FF_CHUNK):
        cs = slice(c * FF_CHUNK, (c + 1) * FF_CHUNK)
        up = jnp.maximum(_dot(hf, w_up_ref[:, cs]), 0.0)
        part = _dot((up * up).astype(BF16), w_down_ref[cs, :])
        ffn = part if ffn is None else ffn + part
    y_ref[...] = x1 + ffn


def _merge_ffn_call(x, attn, convn, weights, rows):
    n = x.shape[0]
    row = lambda w: pl.BlockSpec((rows, w), lambda i: (i, 0))
    return pl.pallas_call(
        _merge_ffn_kernel,
        grid=(n // rows,),
        in_specs=[row(D_MODEL), row(ATTN_WIDTH), row(CONV_CH)] + [_const_spec(w.shape) for w in weights],
        out_specs=row(D_MODEL),
        out_shape=jax.ShapeDtypeStruct((n, D_MODEL), F32),
        compiler_params=pltpu.CompilerParams(dimension_semantics=("arbitrary",),
                                             vmem_limit_bytes=VMEM_LIMIT),
        name="merge_ffn",
    )(x, attn, convn, *weights)


def _slots(w, width):
    k = w.shape[0]
    w = w.reshape(k, N_HEADS, width)
    return jnp.pad(w, ((0, 0), (0, 0), (0, SLOT - width))).reshape(k, HEADS_W)


def _rot_partner(w):
    return jnp.concatenate([-w[..., HALF_ROPE:], w[..., :HALF_ROPE]], axis=-1)


def _rope_slot(w):
    return jnp.pad(w, ((0, 0), (QK_NOPE, SLOT - QK_HEAD)))


def _layer_weights(w_in, q_lora_g, kv_lora_g, w_uq, w_ukv, q_norm_g, k_norm_g, conv_w, conv_b,
                   conv_out_g, norm_mix_g):
    o1 = Q_LORA
    o2 = o1 + KV_LORA
    o3 = o2 + QK_ROPE
    w_kpe = w_in[:, o2:o3]
    w_in_p = jnp.concatenate(
        [w_in[:, :o2], w_in[:, o3:], _rope_slot(w_kpe), _rope_slot(_rot_partner(w_kpe))],
        axis=1).astype(BF16)
    uq = w_uq.reshape(Q_LORA, N_HEADS, QK_HEAD)
    w_uq_p = _slots(w_uq, QK_HEAD).astype(BF16)
    uq_rot = jnp.pad(_rot_partner(uq[..., QK_NOPE:]), ((0, 0), (0, 0), (QK_NOPE, SLOT - QK_HEAD)))
    w_uq_r = uq_rot.reshape(Q_LORA, HEADS_W).astype(BF16)
    ukv = w_ukv.reshape(KV_LORA, N_HEADS, QK_NOPE + V_HEAD)
    w_uk_p = _slots(ukv[..., :QK_NOPE].reshape(KV_LORA, N_HEADS * QK_NOPE), QK_NOPE).astype(BF16)
    w_uv = ukv[..., QK_NOPE:].reshape(KV_LORA, ATTN_WIDTH).astype(BF16)
    w_ukt = jnp.transpose(ukv[..., :QK_NOPE], (2, 1, 0)).reshape(N_HEADS * QK_NOPE, KV_LORA).astype(BF16)
    pad_g = lambda g: jnp.pad(g, (0, SLOT - QK_HEAD))[None, :]
    gq = pad_g(q_norm_g) * (ATTN_SCALE * LOG2_E)
    gk = pad_g(k_norm_g)
    mixer = (norm_mix_g[None, :], w_in_p, q_lora_g[None, :], w_uq_p, w_uq_r, kv_lora_g[None, :],
             w_uk_p, w_uv, gq, gk, conv_w, conv_b[None, :], conv_out_g[None, :])
    return mixer, w_ukt, w_uv


def _rope_slot_tables(pos):
    inv_freq = ROPE_THETA ** (-(jnp.arange(0, QK_ROPE, 2, dtype=F32) / QK_ROPE))
    ang = pos.astype(F32)[:, None] * inv_freq[None, :]
    n = pos.shape[0]
    cos2 = jnp.concatenate([jnp.cos(ang)] * 2, axis=1)
    sin2 = jnp.concatenate([jnp.sin(ang)] * 2, axis=1)
    cos_t = jnp.concatenate([jnp.ones((n, QK_NOPE), F32), cos2, jnp.zeros((n, SLOT - QK_HEAD), F32)], axis=1)
    sin_t = jnp.pad(sin2, ((0, 0), (QK_NOPE, SLOT - QK_HEAD)))
    return cos_t, sin_t


def kernel(x_prompt, x_sample, cache_kv_latent, cache_k_rope, state_conv, page_table, meta_tokens,
           norm_mix_g, w_in, q_lora_g, kv_lora_g, w_uq, w_ukv, q_norm_g, k_norm_g, conv_w, conv_b,
           attn_out_g, conv_out_g, w_o, norm_ffn_g, w_up, w_down):
    depth = w_in.shape[0]
    assert depth == 1, "the prompt and sample streams are chained for a single layer"
    nb, seq, _ = x_prompt.shape
    nseq, dec, _ = x_sample.shape
    past = page_table.shape[1] * PAGE_SIZE
    l = 0

    mixer_w, w_ukt, w_uv = _layer_weights(w_in[l], q_lora_g[l], kv_lora_g[l], w_uq[l], w_ukv[l],
                                          q_norm_g[l], k_norm_g[l], conv_w[l], conv_b[l],
                                          conv_out_g[l], norm_mix_g[l])
    ffn_w = (attn_out_g[l][None, :], w_o[l].astype(BF16), norm_ffn_g[l][None, :],
             w_up[l].astype(BF16), w_down[l].astype(BF16))

    ns = nseq * dec
    xs_rows = jnp.concatenate([x_sample.reshape(ns, D_MODEL), meta_tokens.astype(F32)], axis=0)
    pos_s = jnp.concatenate([jnp.tile(past + jnp.arange(dec), nseq), jnp.arange(N_META)])
    cos_s, sin_s = _rope_slot_tables(pos_s)
    t_in_seq = jnp.concatenate([jnp.tile(jnp.arange(dec), nseq), jnp.arange(N_META)])
    m1 = (t_in_seq >= 1).astype(F32)[:, None]
    m2 = (t_in_seq >= 2).astype(F32)[:, None]
    st = state_conv[l].astype(F32)
    zrow = jnp.zeros((nseq, 1, CONV_CH), F32)
    s1 = jnp.concatenate([st[:, 1:2], zrow, zrow, zrow], axis=1)[:, :dec]
    s2 = jnp.concatenate([st[:, 0:1], st[:, 1:2], zrow, zrow], axis=1)[:, :dec]
    zmeta = jnp.zeros((N_META, CONV_CH), F32)
    s1 = jnp.concatenate([s1.reshape(ns, CONV_CH), zmeta], axis=0)
    s2 = jnp.concatenate([s2.reshape(ns, CONV_CH), zmeta], axis=0)
    (qk_s, qabs_s, k_s, v_s, lat_s, kpe_s, convn_s, u_s) = _mixer_call(
        xs_rows, cos_s, sin_s, (m1, m2, s1, s2), mixer_w, sample_mode=True)

    pad_meta = lambda a: jnp.pad(a[ns:], ((0, LANES - N_META), (0, 0)))
    k_meta, v_meta = pad_meta(k_s), pad_meta(v_s)
    lat_meta, kpe_meta, u_meta = lat_s[ns:], kpe_s[ns:], u_s[ns:]

    cos_p, sin_p = _rope_slot_tables(N_META + jnp.arange(seq))
    q_p, k_p, v_p, lat_p, kpe_p, convn_p, utail_p = _mixer_call(
        x_prompt, cos_p, sin_p, u_meta[N_META - SUBLANES:], mixer_w, sample_mode=False)
    attn_p = _prompt_attn_call(q_p, k_p, v_p, k_meta, v_meta)
    y_prompt = _merge_ffn_call(x_prompt.reshape(nb * seq, D_MODEL), attn_p.reshape(nb * seq, ATTN_WIDTH),
                               convn_p.reshape(nb * seq, CONV_CH), ffn_w, ROW_TILE)
    y_prompt = y_prompt.reshape(nb, seq, D_MODEL)
    bcast = lambda a: jnp.broadcast_to(a[None], (nb,) + a.shape)
    new_lat_prompt = jnp.concatenate([bcast(lat_meta), lat_p], axis=1)[None]
    new_kpe_prompt = jnp.concatenate([bcast(kpe_meta), kpe_p], axis=1)[None]
    new_conv_prompt = utail_p[:, -1, SUBLANES - (CONV_W - 1):][None]

    nq = dec * N_HEADS
    qabs = qabs_s[:ns].reshape(nseq, nq, KV_LORA)
    qpe = qk_s[:ns].reshape(ns, N_HEADS, SLOT)[:, :, QK_NOPE:QK_HEAD].reshape(nseq, nq, QK_ROPE)
    assert dec <= TAIL_KEYS
    pad_tail = lambda a: jnp.pad(a[:ns].reshape(nseq, dec, -1), ((0, 0), (0, TAIL_KEYS - dec), (0, 0)))
    attn_s = _sample_attn_call(page_table, qabs, qpe, w_ukt, w_uv, pad_tail(lat_s),
                               jnp.swapaxes(pad_tail(kpe_s), 1, 2),
                               cache_kv_latent[l], jnp.swapaxes(cache_k_rope[l], 1, 2))
    y_sample = _merge_ffn_call(x_sample.reshape(ns, D_MODEL), attn_s.reshape(ns, ATTN_WIDTH),
                               convn_s[:ns], ffn_w, ns)
    y_sample = y_sample.reshape(nseq, dec, D_MODEL)
    new_lat_sample = lat_s[:ns].reshape(nseq, dec, KV_LORA)[None]
    new_kpe_sample = kpe_s[:ns].reshape(nseq, dec, QK_ROPE)[None]
    us = jnp.concatenate([st, u_s[:ns].reshape(nseq, dec, CONV_CH)], axis=1)
    new_conv_sample = us[:, -(CONV_W - 1):][None]

    return (y_prompt, y_sample, new_lat_prompt, new_kpe_prompt, new_conv_prompt,
            new_lat_sample, new_kpe_sample, new_conv_sample)
```

```python
import functools

import jax
import jax.numpy as jnp
from jax import lax
from jax.experimental import pallas as pl
from jax.experimental.pallas import tpu as pltpu

D_MODEL = 1024
N_META = 16
N_HEADS = 8
QK_NOPE = 64
QK_ROPE = 32
V_HEAD = 64
QK_HEAD = QK_NOPE + QK_ROPE
Q_LORA = 384
KV_LORA = 256
ATTN_WIDTH = N_HEADS * V_HEAD
CONV_CH = D_MODEL - ATTN_WIDTH
CONV_W = 3
D_FF = 4 * D_MODEL
ROPE_THETA = 10000.0
EPS = 1e-6
PAGE_SIZE = 128
ATTN_SCALE = QK_HEAD ** -0.5
LOG2_E = 1.4426950408889634

LANES = 128
SUBLANES = 8
SLOT = LANES
HEADS_W = N_HEADS * SLOT
HALF_ROPE = QK_ROPE // 2

C_CQ = 0
C_CKV = C_CQ + Q_LORA
C_GB = C_CKV + KV_LORA
C_GC = C_GB + CONV_CH
C_HC = C_GC + CONV_CH
C_KPE = C_HC + CONV_CH
C_KPR = C_KPE + SLOT
IN_W = C_KPR + SLOT

ROW_TILE = 512
ATT_TQ = 256
ATT_TK = 256
ATT_TK_WIDE = 2048
ATT_LOOKAHEAD = 4
FF_CHUNK = 1024
CHUNK_PAGES = 8
DMA_GROUP = 8
DMA_ISSUE_PAGES = 2
TAIL_KEYS = PAGE_SIZE
VMEM_LIMIT = 52 * 1024 * 1024

BF16 = jnp.bfloat16
F32 = jnp.float32

_NT = (((1,), (1,)), ((), ()))


def _dot(a, b):
    return jnp.dot(a, b, preferred_element_type=F32)


def _dot_nt(a, b):
    return lax.dot_general(a, b, _NT, preferred_element_type=F32)


def _rms(x, g):
    return x * lax.rsqrt(jnp.mean(x * x, axis=-1, keepdims=True) + EPS) * g


def _const_spec(shape):
    nd = len(shape)
    return pl.BlockSpec(shape, lambda *_: (0,) * nd, pipeline_mode=pl.Buffered(1))


def _mixer_kernel(*refs, rows, sample_mode):
    it = iter(refs)
    x_ref, cos_ref, sin_ref = next(it), next(it), next(it)
    if sample_mode:
        m1_ref, m2_ref, s1_ref, s2_ref = next(it), next(it), next(it), next(it)
    else:
        tail_in_ref = next(it)
    (g_mix_ref, w_in_ref, g_ql_ref, w_uq_ref, w_uqr_ref, g_kvl_ref, w_uk_ref, w_uv_ref,
     gq_ref, gk_ref, cw_ref, cb_ref, g_conv_ref) = (next(it) for _ in range(13))
    if sample_mode:
        (qk_ref, qabs_ref, k_ref, v_ref, lat_ref, kpe_ref, convn_ref, u_ref) = (next(it) for _ in range(8))
    else:
        (q_ref, k_ref, v_ref, lat_ref, kpe_ref, convn_ref, utail_ref) = (next(it) for _ in range(7))
    ubuf = next(it)

    hn = _rms(x_ref[...], g_mix_ref[...]).astype(BF16)
    cos = cos_ref[...]
    sin = sin_ref[...]

    cq = _dot(hn, w_in_ref[:, C_CQ:C_CKV])
    ckv = _dot(hn, w_in_ref[:, C_CKV:C_GB])
    zk = _dot(hn, w_in_ref[:, C_KPE:IN_W])
    gc = _dot(hn, w_in_ref[:, C_GC:C_HC])
    hc = _dot(hn, w_in_ref[:, C_HC:C_KPE])
    gb = _dot(hn, w_in_ref[:, C_GB:C_GC])

    cqn = _rms(cq, g_ql_ref[...]).astype(BF16)
    gq = gq_ref[...]
    gk = gk_ref[...]
    for h in range(N_HEADS):
        sl = slice(h * SLOT, (h + 1) * SLOT)
        qh = _dot(cqn, w_uq_ref[:, sl]) * cos + _dot(cqn, w_uqr_ref[:, sl]) * sin
        ss = jnp.sum(qh * qh, axis=-1, keepdims=True)
        qh = qh * lax.rsqrt(ss * (1.0 / QK_HEAD) + EPS) * gq
        if sample_mode:
            qkh = (qh * gk).astype(BF16)
            qk_ref[:, sl] = qkh
            qabs_ref[:, h * KV_LORA:(h + 1) * KV_LORA] = _dot_nt(qkh, w_uk_ref[:, sl]).astype(BF16)
        else:
            q_ref[:, sl] = qh.astype(BF16)

    lat = _rms(ckv, g_kvl_ref[...])
    lat_ref[...] = lat
    lat_b = lat.astype(BF16)
    krot = zk[:, :SLOT] * cos + zk[:, SLOT:] * sin
    kpe_ref[...] = krot[:, QK_NOPE:QK_HEAD]
    ss_rot = jnp.sum(krot * krot, axis=-1, keepdims=True)
    for h in range(N_HEADS):
        sl = slice(h * SLOT, (h + 1) * SLOT)
        kn = _dot(lat_b, w_uk_ref[:, sl])
        ss = jnp.sum(kn * kn, axis=-1, keepdims=True) + ss_rot
        kh = (kn + krot) * lax.rsqrt(ss * (1.0 / QK_HEAD) + EPS) * gk
        k_ref[:, sl] = kh.astype(BF16)
    v_ref[...] = _dot(lat_b, w_uv_ref[...]).astype(BF16)

    u = gc * hc
    if sample_mode:
        ubuf[0:SUBLANES, :] = jnp.zeros((SUBLANES, CONV_CH), F32)
    else:
        t = pl.program_id(1)

        @pl.when(t == 0)
        def _():
            ubuf[0:SUBLANES, :] = tail_in_ref[...]

        @pl.when(t != 0)
        def _():
            ubuf[0:SUBLANES, :] = ubuf[rows:rows + SUBLANES, :]
    ubuf[SUBLANES:SUBLANES + rows, :] = u
    u1 = ubuf[SUBLANES - 1:SUBLANES - 1 + rows, :]
    u2 = ubuf[SUBLANES - 2:SUBLANES - 2 + rows, :]
    if sample_mode:
        u1 = u1 * m1_ref[...] + s1_ref[...]
        u2 = u2 * m2_ref[...] + s2_ref[...]
        u_ref[...] = u
    else:
        utail_ref[...] = u[rows - SUBLANES:, :]
    y = cb_ref[...] + u2 * cw_ref[0:1, :] + u1 * cw_ref[1:2, :] + u * cw_ref[2:3, :]
    convn_ref[...] = _rms(gb * y, g_conv_ref[...]).astype(BF16)


def _mixer_call(x, cos_t, sin_t, conv_in, weights, *, sample_mode):
    wspecs = [_const_spec(w.shape) for w in weights]
    if sample_mode:
        rows = x.shape[0]
        grid = (1,)
        row = lambda w: pl.BlockSpec((rows, w), lambda i: (0, 0))
        m1, m2, s1, s2 = conv_in
        in_specs = [row(D_MODEL), row(SLOT), row(SLOT), row(1), row(1), row(CONV_CH), row(CONV_CH)]
        args = [x, cos_t, sin_t, m1, m2, s1, s2]
        widths = [(HEADS_W, BF16), (N_HEADS * KV_LORA, BF16), (HEADS_W, BF16), (ATTN_WIDTH, BF16),
                  (KV_LORA, F32), (QK_ROPE, F32), (CONV_CH, BF16), (CONV_CH, F32)]
        out_shape = [jax.ShapeDtypeStruct((rows, w), d) for w, d in widths]
        out_specs = [row(w) for w, _ in widths]
        sem = ("arbitrary",)
    else:
        nb, seq, _ = x.shape
        rows = ROW_TILE
        nt = seq // rows
        grid = (nb, nt)
        row3 = lambda w: pl.BlockSpec((None, rows, w), lambda b, t: (b, t, 0))
        tab = pl.BlockSpec((rows, SLOT), lambda b, t: (t, 0))
        in_specs = [row3(D_MODEL), tab, tab, _const_spec(conv_in.shape)]
        args = [x, cos_t, sin_t, conv_in]
        widths = [(HEADS_W, BF16), (HEADS_W, BF16), (ATTN_WIDTH, BF16), (KV_LORA, F32),
                  (QK_ROPE, F32), (CONV_CH, BF16)]
        out_shape = [jax.ShapeDtypeStruct((nb, seq, w), d) for w, d in widths]
        out_shape.append(jax.ShapeDtypeStruct((nb, nt, SUBLANES, CONV_CH), F32))
        out_specs = [row3(w) for w, _ in widths]
        out_specs.append(pl.BlockSpec((None, None, SUBLANES, CONV_CH), lambda b, t: (b, t, 0, 0)))
        sem = ("arbitrary", "arbitrary")
    return pl.pallas_call(
        functools.partial(_mixer_kernel, rows=rows, sample_mode=sample_mode),
        grid=grid,
        in_specs=in_specs + wspecs,
        out_specs=out_specs,
        out_shape=out_shape,
        scratch_shapes=[pltpu.VMEM((rows + 2 * SUBLANES, CONV_CH), F32)],
        compiler_params=pltpu.CompilerParams(dimension_semantics=sem, vmem_limit_bytes=VMEM_LIMIT),
        name="mixer_sample" if sample_mode else "mixer_prompt",
    )(*args, *weights)


def _softmax_step(s, state, pv):
    m_old, l_old, acc_old = state
    n = s.shape[1] // LANES
    m_new = jnp.maximum(m_old, jnp.max(s, axis=-1, keepdims=True))
    alpha = jnp.exp2(m_old - m_new)
    ps = [jnp.exp2(s[:, c * LANES:(c + 1) * LANES] - m_new) for c in range(n)]
    l_new = alpha * l_old + functools.reduce(lambda a, b: a + b, ps)
    p = ps[0] if n == 1 else jnp.concatenate(ps, axis=1)
    w = acc_old.shape[-1] // LANES
    alpha_w = alpha if w == 1 else jnp.concatenate([alpha] * w, axis=1)
    return m_new, l_new, alpha_w * acc_old + pv(p.astype(BF16))


def _softmax_update(s, m_ref, l_ref, acc_ref, pv):
    m_ref[...], l_ref[...], acc_ref[...] = _softmax_step(s, (m_ref[...], l_ref[...], acc_ref[...]), pv)


def _prompt_attn_kernel(q_ref, k_ref, v_ref, km_ref, vm_ref, o_ref, m_sc, l_sc, acc_sc):
    seq = q_ref.shape[0]
    half = ATT_TQ // 2
    lane = lax.broadcasted_iota(jnp.int32, (ATT_TQ, SLOT), 1)
    m_sc[...] = jnp.full(m_sc.shape, -jnp.inf, F32)
    l_sc[...] = jnp.zeros(l_sc.shape, F32)
    acc_sc[...] = jnp.zeros(acc_sc.shape, F32)

    meta = lane < N_META
    col_h = lax.broadcasted_iota(jnp.int32, (half, half), 1)
    row_h = lax.broadcasted_iota(jnp.int32, (half, half), 0)
    col_f = lax.broadcasted_iota(jnp.int32, (half, ATT_TK), 1)
    row_f = lax.broadcasted_iota(jnp.int32, (half, ATT_TK), 0)

    steps = []
    for qi in range(seq // ATT_TQ):
        for hh in range(2):
            steps.append((qi, hh, 0, ATT_TQ, None, meta))
        d0 = qi * ATT_TK
        for k0 in range(0, d0, ATT_TK_WIDE):
            for hh in range(2):
                steps.append((qi, hh, 0, ATT_TQ, slice(k0, min(k0 + ATT_TK_WIDE, d0)), None))
        for hh in range(2):
            steps.append((qi, hh, 0, half, slice(d0, d0 + half), col_h <= row_h))
            steps.append((qi, hh, half, half, slice(d0, d0 + ATT_TK), col_f <= row_f + half))
        steps.append((qi,))

    def scores(qi, hh, r0, nr, ks, mask):
        sl = slice(hh * SLOT, (hh + 1) * SLOT)
        k_blk = km_ref[:, sl] if ks is None else k_ref[ks, sl]
        s = _dot_nt(q_ref[pl.ds(qi * ATT_TQ + r0, nr), sl], k_blk)
        return s if mask is None else jnp.where(mask, s, -jnp.inf)

    def update(s, qi, hh, r0, nr, ks, mask):
        rows = pl.ds(r0, nr)
        v_blk = vm_ref[...] if ks is None else v_ref[ks, :]
        _softmax_update(s, m_sc.at[qi, hh, rows], l_sc.at[qi, hh, rows], acc_sc.at[qi, hh, rows],
                        lambda p: _dot(p, v_blk))

    def finalize(qi):
        outs = [acc_sc[qi, hh] / jnp.sum(l_sc[qi, hh], axis=-1, keepdims=True) for hh in range(2)]
        o_ref[qi * ATT_TQ:(qi + 1) * ATT_TQ, :] = (
            jnp.where(lane < V_HEAD, outs[0], outs[1]).astype(o_ref.dtype))

    pending = []
    for st in steps:
        pending.append((st, scores(*st) if len(st) > 1 else None))
        if len(pending) > ATT_LOOKAHEAD:
            d, s = pending.pop(0)
            finalize(*d) if s is None else update(s, *d)
    for d, s in pending:
        finalize(*d) if s is None else update(s, *d)


def _prompt_attn_call(q, k, v, k_meta, v_meta):
    nb, seq, _ = q.shape
    assert ATT_TQ == ATT_TK and seq % ATT_TQ == 0
    grid = (nb, N_HEADS // 2)
    return pl.pallas_call(
        _prompt_attn_kernel,
        grid=grid,
        in_specs=[
            pl.BlockSpec((None, seq, 2 * SLOT), lambda b, p: (b, 0, p)),
            pl.BlockSpec((None, seq, 2 * SLOT), lambda b, p: (b, 0, p)),
            pl.BlockSpec((None, seq, SLOT), lambda b, p: (b, 0, p)),
            pl.BlockSpec((LANES, 2 * SLOT), lambda b, p: (0, p)),
            pl.BlockSpec((LANES, SLOT), lambda b, p: (0, p)),
        ],
        out_specs=pl.BlockSpec((None, seq, SLOT), lambda b, p: (b, 0, p)),
        out_shape=jax.ShapeDtypeStruct((nb, seq, ATTN_WIDTH), BF16),
        scratch_shapes=[pltpu.VMEM((seq // ATT_TQ, 2, ATT_TQ, LANES), F32)] * 3,
        compiler_params=pltpu.CompilerParams(
            dimension_semantics=("arbitrary", "arbitrary"),
            vmem_limit_bytes=VMEM_LIMIT),
        name="prompt_attn",
    )(q, k, v, k_meta, v_meta)


def _sample_attn_kernel(pt_ref, qabs_ref, qpe_ref, wukt_ref, wuv_ref, tlat_ref, tkpe_ref,
                        lat_hbm, kpe_hbm, o_ref,
                        latb_sc, lhs_sc, lat_buf, kpe_buf, sems):
    n_pages = lat_buf.shape[1]
    chunk_pages = (CHUNK_PAGES,) * (n_pages // CHUNK_PAGES)
    b = pl.program_id(0)
    last = pl.num_programs(0) - 1
    slot = b % 2
    nq = qabs_ref.shape[0]
    reps = nq // N_HEADS
    nkn = N_HEADS * QK_NOPE
    qpe = qpe_ref[...]

    ngroup = lat_buf.shape[1] // DMA_GROUP

    def group_copies(bb, sl, gg):
        copies = []
        for j in range(gg * DMA_GROUP, (gg + 1) * DMA_GROUP):
            page = pt_ref[bb, j]
            copies.append(pltpu.make_async_copy(lat_hbm.at[page], lat_buf.at[sl, j], sems.at[0, sl]))
            copies.append(pltpu.make_async_copy(kpe_hbm.at[page], kpe_buf.at[sl, j], sems.at[1, sl]))
        return copies

    @pl.when(b == 0)
    def _():
        lhs_sc[0:nkn, :] = wukt_ref[...]
        for gg in range(ngroup):
            for cp in group_copies(0, 0, gg):
                cp.start()

    nxt = jnp.minimum(b + 1, last)

    lhs_sc[nkn:nkn + nq, :] = qabs_ref[...]
    state = (jnp.full((nq, LANES), -jnp.inf, F32), jnp.zeros((nq, LANES), F32),
             jnp.zeros((nq, KV_LORA), F32))

    def scores(lat_b, kpe_t):
        nk = lat_b.shape[0]
        full = _dot_nt(lhs_sc[...], lat_b)
        knt = full[:nkn]
        nsum = jnp.sum((knt * knt).reshape(QK_NOPE, N_HEADS, nk), axis=0)
        rsum = jnp.sum(kpe_t * kpe_t, axis=0, keepdims=True)
        r = lax.rsqrt((nsum + rsum) * (1.0 / QK_HEAD) + EPS)
        s = full[nkn:] + _dot(qpe, kpe_t.astype(BF16))
        return s * jnp.concatenate([r] * reps, axis=0)

    for gg in range(ngroup):
        for cp in group_copies(b, slot, gg):
            cp.wait()

    tail_b = tlat_ref[...].astype(BF16)
    s = scores(tail_b, tkpe_ref[...])
    key = lax.broadcasted_iota(jnp.int32, s.shape, 1)
    qt = lax.broadcasted_iota(jnp.int32, s.shape, 0) // N_HEADS
    s_tail = jnp.where(key <= qt, s, -jnp.inf)

    def chunk_scores(p0, npg):
        parts = [s_tail] if p0 == 0 else []
        for j in range(p0, p0 + npg, 2):
            for jj in (j, j + 1):
                latb_sc[jj * PAGE_SIZE:(jj + 1) * PAGE_SIZE, :] = lat_buf[slot, jj].astype(BF16)
            kpe_t = jnp.concatenate([kpe_buf[slot, j], kpe_buf[slot, j + 1]], axis=1)
            parts.append(scores(latb_sc[j * PAGE_SIZE:(j + 2) * PAGE_SIZE, :], kpe_t))
            if (j + 2) % DMA_ISSUE_PAGES == 0 and (j + 2) // DMA_ISSUE_PAGES <= ngroup:
                for cp in group_copies(nxt, 1 - slot, (j + 2) // DMA_ISSUE_PAGES - 1):
                    cp.start()
        return jnp.concatenate(parts, axis=1)

    def chunk_update(state, s, p0, npg):
        rows = slice(p0 * PAGE_SIZE, (p0 + npg) * PAGE_SIZE)
        if p0 == 0:
            pv = lambda p: _dot(p[:, :TAIL_KEYS], tail_b) + _dot(p[:, TAIL_KEYS:], latb_sc[rows, :])
        else:
            pv = lambda p: _dot(p, latb_sc[rows, :])
        return _softmax_step(s, state, pv)

    starts = [sum(chunk_pages[:i]) for i in range(len(chunk_pages))]
    pending = None
    for p0, npg in zip(starts, chunk_pages):
        s = chunk_scores(p0, npg)
        if pending is not None:
            state = chunk_update(state, *pending)
        pending = (s, p0, npg)
    state = chunk_update(state, *pending)

    _, l_fin, acc_fin = state
    o_lat = (acc_fin / jnp.sum(l_fin, axis=-1, keepdims=True)).astype(BF16)
    full = _dot(o_lat, wuv_ref[...])
    row_h = lax.broadcasted_iota(jnp.int32, full.shape, 0) % N_HEADS
    col_h = lax.broadcasted_iota(jnp.int32, full.shape, 1) // V_HEAD
    own = jnp.where(row_h == col_h, full, 0.0)
    o_ref[...] = jnp.sum(own.reshape(reps, N_HEADS, ATTN_WIDTH), axis=1)

    @pl.when(b == last)
    def _():
        for gg in range(ngroup):
            for cp in group_copies(last, 1 - slot, gg):
                cp.wait()


def _sample_attn_call(page_table, qabs, qpe, w_ukt, w_uv, tail_lat, tail_kpe_t, cache_lat, cache_kpe_t):
    nseq, n_pages = page_table.shape
    nq = qabs.shape[1]
    assert n_pages % DMA_GROUP == 0 and DMA_GROUP % 2 == 0 and DMA_ISSUE_PAGES % 2 == 0
    assert DMA_ISSUE_PAGES <= DMA_GROUP
    assert n_pages % CHUNK_PAGES == 0 and CHUNK_PAGES % 2 == 0
    grid = (nseq,)

    per_seq = lambda rows, width: pl.BlockSpec((None, rows, width), lambda b, pt: (b, 0, 0))
    whole = lambda shape: pl.BlockSpec(shape, lambda b, pt: (0,) * len(shape),
                                       pipeline_mode=pl.Buffered(1))
    in_specs = [per_seq(nq, KV_LORA), per_seq(nq, QK_ROPE), whole(w_ukt.shape), whole(w_uv.shape),
                per_seq(TAIL_KEYS, KV_LORA), per_seq(QK_ROPE, TAIL_KEYS),
                pl.BlockSpec(memory_space=pl.ANY), pl.BlockSpec(memory_space=pl.ANY)]
    reps = nq // N_HEADS
    return pl.pallas_call(
        _sample_attn_kernel,
        grid_spec=pltpu.PrefetchScalarGridSpec(
            num_scalar_prefetch=1,
            grid=grid,
            in_specs=in_specs,
            out_specs=pl.BlockSpec((None, reps, ATTN_WIDTH), lambda b, pt: (b, 0, 0)),
            scratch_shapes=[pltpu.VMEM((n_pages * PAGE_SIZE, KV_LORA), BF16),
                            pltpu.VMEM((N_HEADS * QK_NOPE + nq, KV_LORA), BF16),
                            pltpu.VMEM((2, n_pages, PAGE_SIZE, KV_LORA), F32),
                            pltpu.VMEM((2, n_pages, QK_ROPE, PAGE_SIZE), F32),
                            pltpu.SemaphoreType.DMA((2, 2))],
        ),
        out_shape=jax.ShapeDtypeStruct((nseq, reps, ATTN_WIDTH), F32),
        compiler_params=pltpu.CompilerParams(dimension_semantics=("arbitrary",),
                                             vmem_limit_bytes=VMEM_LIMIT),
        name="sample_attn",
    )(page_table, qabs, qpe, w_ukt, w_uv, tail_lat, tail_kpe_t, cache_lat, cache_kpe_t)


def _merge_ffn_kernel(x_ref, attn_ref, convn_ref, g_attn_ref, w_o_ref, g_ffn_ref, w_up_ref,
                      w_down_ref, y_ref):
    an = _rms(attn_ref[...].astype(F32), g_attn_ref[...]).astype(BF16)
    x1 = x_ref[...] + (_dot(an, w_o_ref[0:ATTN_WIDTH, :]) + _dot(convn_ref[...], w_o_ref[ATTN_WIDTH:, :]))
    hf = _rms(x1, g_ffn_ref[...]).astype(BF16)
    ffn = None
    for c in range(D_FF // FF_CHUNK):
        cs = slice(c * FF_CHUNK, (c + 1) * FF_CHUNK)
        up = jnp.maximum(_dot(hf, w_up_ref[:, cs]), 0.0)
        part = _dot((up * up).astype(BF16), w_down_ref[cs, :])
        ffn = part if ffn is None else ffn + part
    y_ref[...] = x1 + ffn


def _merge_ffn_call(x, attn, convn, weights, rows):
    n = x.shape[0]
    row = lambda w: pl.BlockSpec((rows, w), lambda i: (i, 0))
    return pl.pallas_call(
        _merge_ffn_kernel,
        grid=(n // rows,),
        in_specs=[row(D_MODEL), row(ATTN_WIDTH), row(CONV_CH)] + [_const_spec(w.shape) for w in weights],
        out_specs=row(D_MODEL),
        out_shape=jax.ShapeDtypeStruct((n, D_MODEL), F32),
        compiler_params=pltpu.CompilerParams(dimension_semantics=("arbitrary",),
                                             vmem_limit_bytes=VMEM_LIMIT),
        name="merge_ffn",
    )(x, attn, convn, *weights)


def _slots(w, width):
    k = w.shape[0]
    w = w.reshape(k, N_HEADS, width)
    return jnp.pad(w, ((0, 0), (0, 0), (0, SLOT - width))).reshape(k, HEADS_W)


def _rot_partner(w):
    return jnp.concatenate([-w[..., HALF_ROPE:], w[..., :HALF_ROPE]], axis=-1)


def _rope_slot(w):
    return jnp.pad(w, ((0, 0), (QK_NOPE, SLOT - QK_HEAD)))


def _layer_weights(w_in, q_lora_g, kv_lora_g, w_uq, w_ukv, q_norm_g, k_norm_g, conv_w, conv_b,
                   conv_out_g, norm_mix_g):
    o1 = Q_LORA
    o2 = o1 + KV_LORA
    o3 = o2 + QK_ROPE
    w_kpe = w_in[:, o2:o3]
    w_in_p = jnp.concatenate(
        [w_in[:, :o2], w_in[:, o3:], _rope_slot(w_kpe), _rope_slot(_rot_partner(w_kpe))],
        axis=1).astype(BF16)
    uq = w_uq.reshape(Q_LORA, N_HEADS, QK_HEAD)
    w_uq_p = _slots(w_uq, QK_HEAD).astype(BF16)
    uq_rot = jnp.pad(_rot_partner(uq[..., QK_NOPE:]), ((0, 0), (0, 0), (QK_NOPE, SLOT - QK_HEAD)))
    w_uq_r = uq_rot.reshape(Q_LORA, HEADS_W).astype(BF16)
    ukv = w_ukv.reshape(KV_LORA, N_HEADS, QK_NOPE + V_HEAD)
    w_uk_p = _slots(ukv[..., :QK_NOPE].reshape(KV_LORA, N_HEADS * QK_NOPE), QK_NOPE).astype(BF16)
    w_uv = ukv[..., QK_NOPE:].reshape(KV_LORA, ATTN_WIDTH).astype(BF16)
    w_ukt = jnp.transpose(ukv[..., :QK_NOPE], (2, 1, 0)).reshape(N_HEADS * QK_NOPE, KV_LORA).astype(BF16)
    pad_g = lambda g: jnp.pad(g, (0, SLOT - QK_HEAD))[None, :]
    gq = pad_g(q_norm_g) * (ATTN_SCALE * LOG2_E)
    gk = pad_g(k_norm_g)
    mixer = (norm_mix_g[None, :], w_in_p, q_lora_g[None, :], w_uq_p, w_uq_r, kv_lora_g[None, :],
             w_uk_p, w_uv, gq, gk, conv_w, conv_b[None, :], conv_out_g[None, :])
    return mixer, w_ukt, w_uv


def _rope_slot_tables(pos):
    inv_freq = ROPE_THETA ** (-(jnp.arange(0, QK_ROPE, 2, dtype=F32) / QK_ROPE))
    ang = pos.astype(F32)[:, None] * inv_freq[None, :]
    n = pos.shape[0]
    cos2 = jnp.concatenate([jnp.cos(ang)] * 2, axis=1)
    sin2 = jnp.concatenate([jnp.sin(ang)] * 2, axis=1)
    cos_t = jnp.concatenate([jnp.ones((n, QK_NOPE), F32), cos2, jnp.zeros((n, SLOT - QK_HEAD), F32)], axis=1)
    sin_t = jnp.pad(sin2, ((0, 0), (QK_NOPE, SLOT - QK_HEAD)))
    return cos_t, sin_t


def kernel(x_prompt, x_sample, cache_kv_latent, cache_k_rope, state_conv, page_table, meta_tokens,
           norm_mix_g, w_in, q_lora_g, kv_lora_g, w_uq, w_ukv, q_norm_g, k_norm_g, conv_w, conv_b,
           attn_out_g, conv_out_g, w_o, norm_ffn_g, w_up, w_down):
    depth = w_in.shape[0]
    assert depth == 1, "the prompt and sample streams are chained for a single layer"
    nb, seq, _ = x_prompt.shape
    nseq, dec, _ = x_sample.shape
    past = page_table.shape[1] * PAGE_SIZE
    l = 0

    mixer_w, w_ukt, w_uv = _layer_weights(w_in[l], q_lora_g[l], kv_lora_g[l], w_uq[l], w_ukv[l],
                                          q_norm_g[l], k_norm_g[l], conv_w[l], conv_b[l],
                                          conv_out_g[l], norm_mix_g[l])
    ffn_w = (attn_out_g[l][None, :], w_o[l].astype(BF16), norm_ffn_g[l][None, :],
             w_up[l].astype(BF16), w_down[l].astype(BF16))

    ns = nseq * dec
    xs_rows = jnp.concatenate([x_sample.reshape(ns, D_MODEL), meta_tokens.astype(F32)], axis=0)
    pos_s = jnp.concatenate([jnp.tile(past + jnp.arange(dec), nseq), jnp.arange(N_META)])
    cos_s, sin_s = _rope_slot_tables(pos_s)
    t_in_seq = jnp.concatenate([jnp.tile(jnp.arange(dec), nseq), jnp.arange(N_META)])
    m1 = (t_in_seq >= 1).astype(F32)[:, None]
    m2 = (t_in_seq >= 2).astype(F32)[:, None]
    st = state_conv[l].astype(F32)
    zrow = jnp.zeros((nseq, 1, CONV_CH), F32)
    s1 = jnp.concatenate([st[:, 1:2], zrow, zrow, zrow], axis=1)[:, :dec]
    s2 = jnp.concatenate([st[:, 0:1], st[:, 1:2], zrow, zrow], axis=1)[:, :dec]
    zmeta = jnp.zeros((N_META, CONV_CH), F32)
    s1 = jnp.concatenate([s1.reshape(ns, CONV_CH), zmeta], axis=0)
    s2 = jnp.concatenate([s2.reshape(ns, CONV_CH), zmeta], axis=0)
    (qk_s, qabs_s, k_s, v_s, lat_s, kpe_s, convn_s, u_s) = _mixer_call(
        xs_rows, cos_s, sin_s, (m1, m2, s1, s2), mixer_w, sample_mode=True)

    pad_meta = lambda a: jnp.pad(a[ns:], ((0, LANES - N_META), (0, 0)))
    k_meta, v_meta = pad_meta(k_s), pad_meta(v_s)
    lat_meta, kpe_meta, u_meta = lat_s[ns:], kpe_s[ns:], u_s[ns:]

    cos_p, sin_p = _rope_slot_tables(N_META + jnp.arange(seq))
    q_p, k_p, v_p, lat_p, kpe_p, convn_p, utail_p = _mixer_call(
        x_prompt, cos_p, sin_p, u_meta[N_META - SUBLANES:], mixer_w, sample_mode=False)
    attn_p = _prompt_attn_call(q_p, k_p, v_p, k_meta, v_meta)
    y_prompt = _merge_ffn_call(x_prompt.reshape(nb * seq, D_MODEL), attn_p.reshape(nb * seq, ATTN_WIDTH),
                               convn_p.reshape(nb * seq, CONV_CH), ffn_w, ROW_TILE)
    y_prompt = y_prompt.reshape(nb, seq, D_MODEL)
    bcast = lambda a: jnp.broadcast_to(a[None], (nb,) + a.shape)
    new_lat_prompt = jnp.concatenate([bcast(lat_meta), lat_p], axis=1)[None]
    new_kpe_prompt = jnp.concatenate([bcast(kpe_meta), kpe_p], axis=1)[None]
    new_conv_prompt = utail_p[:, -1, SUBLANES - (CONV_W - 1):][None]

    nq = dec * N_HEADS
    qabs = qabs_s[:ns].reshape(nseq, nq, KV_LORA)
    qpe = qk_s[:ns].reshape(ns, N_HEADS, SLOT)[:, :, QK_NOPE:QK_HEAD].reshape(nseq, nq, QK_ROPE)
    assert dec <= TAIL_KEYS
    pad_tail = lambda a: jnp.pad(a[:ns].reshape(nseq, dec, -1), ((0, 0), (0, TAIL_KEYS - dec), (0, 0)))
    attn_s = _sample_attn_call(page_table, qabs, qpe, w_ukt, w_uv, pad_tail(lat_s),
                               jnp.swapaxes(pad_tail(kpe_s), 1, 2),
                               cache_kv_latent[l], jnp.swapaxes(cache_k_rope[l], 1, 2))
    y_sample = _merge_ffn_call(x_sample.reshape(ns, D_MODEL), attn_s.reshape(ns, ATTN_WIDTH),
                               convn_s[:ns], ffn_w, ns)
    y_sample = y_sample.reshape(nseq, dec, D_MODEL)
    new_lat_sample = lat_s[:ns].reshape(nseq, dec, KV_LORA)[None]
    new_kpe_sample = kpe_s[:ns].reshape(nseq, dec, QK_ROPE)[None]
    us = jnp.concatenate([st, u_s[:ns].reshape(nseq, dec, CONV_CH)], axis=1)
    new_conv_sample = us[:, -(CONV_W - 1):][None]

    return (y_prompt, y_sample, new_lat_prompt, new_kpe_prompt, new_conv_prompt,
            new_lat_sample, new_kpe_sample, new_conv_sample)
```

```python
import functools

import jax
import jax.numpy as jnp
from jax import lax
from jax.experimental import pallas as pl
from jax.experimental.pallas import tpu as pltpu

D_MODEL = 1024
N_META = 16
N_HEADS = 8
QK_NOPE = 64
QK_ROPE = 32
V_HEAD = 64
QK_HEAD = QK_NOPE + QK_ROPE
Q_LORA = 384
KV_LORA = 256
ATTN_WIDTH = N_HEADS * V_HEAD
CONV_CH = D_MODEL - ATTN_WIDTH
CONV_W = 3
D_FF = 4 * D_MODEL
ROPE_THETA = 10000.0
EPS = 1e-6
PAGE_SIZE = 128
ATTN_SCALE = QK_HEAD ** -0.5
LOG2_E = 1.4426950408889634

LANES = 128
SUBLANES = 8
SLOT = LANES
HEADS_W = N_HEADS * SLOT
HALF_ROPE = QK_ROPE // 2

C_CQ = 0
C_CKV = C_CQ + Q_LORA
C_GB = C_CKV + KV_LORA
C_GC = C_GB + CONV_CH
C_HC = C_GC + CONV_CH
C_KPE = C_HC + CONV_CH
C_KPR = C_KPE + SLOT
IN_W = C_KPR + SLOT

ROW_TILE = 512
ATT_TQ = 256
ATT_TK = 256
ATT_TK_WIDE = 2048
ATT_LOOKAHEAD = 4
FF_CHUNK = 1024
CHUNK_PAGES = 8
DMA_GROUP = 8
DMA_ISSUE_PAGES = 2
TAIL_KEYS = PAGE_SIZE
VMEM_LIMIT = 52 * 1024 * 1024

BF16 = jnp.bfloat16
F32 = jnp.float32

_NT = (((1,), (1,)), ((), ()))


def _dot(a, b):
    return jnp.dot(a, b, preferred_element_type=F32)


def _dot_nt(a, b):
    return lax.dot_general(a, b, _NT, preferred_element_type=F32)


def _rms(x, g):
    return x * lax.rsqrt(jnp.mean(x * x, axis=-1, keepdims=True) + EPS) * g


def _const_spec(shape):
    nd = len(shape)
    return pl.BlockSpec(shape, lambda *_: (0,) * nd, pipeline_mode=pl.Buffered(1))


def _mixer_kernel(*refs, rows, sample_mode):
    it = iter(refs)
    x_ref, cos_ref, sin_ref = next(it), next(it), next(it)
    if sample_mode:
        m1_ref, m2_ref, s1_ref, s2_ref = next(it), next(it), next(it), next(it)
    else:
        tail_in_ref = next(it)
    (g_mix_ref, w_in_ref, g_ql_ref, w_uq_ref, w_uqr_ref, g_kvl_ref, w_uk_ref, w_uv_ref,
     gq_ref, gk_ref, cw_ref, cb_ref, g_conv_ref) = (next(it) for _ in range(13))
    if sample_mode:
        (qk_ref, qabs_ref, k_ref, v_ref, lat_ref, kpe_ref, convn_ref, u_ref) = (next(it) for _ in range(8))
    else:
        (q_ref, k_ref, v_ref, lat_ref, kpe_ref, convn_ref, utail_ref) = (next(it) for _ in range(7))
    ubuf = next(it)

    if sample_mode:
        ubuf[0:SUBLANES, :] = jnp.zeros((SUBLANES, CONV_CH), F32)
    else:
        t = pl.program_id(1)

        @pl.when(t == 0)
        def _():
            ubuf[0:SUBLANES, :] = tail_in_ref[...]

        @pl.when(t != 0)
        def _():
            ubuf[0:SUBLANES, :] = ubuf[rows:rows + SUBLANES, :]

    hn = _rms(x_ref[...], g_mix_ref[...]).astype(BF16)
    cos = cos_ref[...]
    sin = sin_ref[...]

    cq = _dot(hn, w_in_ref[:, C_CQ:C_CKV])
    ckv = _dot(hn, w_in_ref[:, C_CKV:C_GB])
    zk = _dot(hn, w_in_ref[:, C_KPE:IN_W])
    gc = _dot(hn, w_in_ref[:, C_GC:C_HC])
    hc = _dot(hn, w_in_ref[:, C_HC:C_KPE])
    gb = _dot(hn, w_in_ref[:, C_GB:C_GC])

    cqn = _rms(cq, g_ql_ref[...]).astype(BF16)
    gq = gq_ref[...]
    gk = gk_ref[...]
    for h in range(N_HEADS):
        sl = slice(h * SLOT, (h + 1) * SLOT)
        if h % 2 == 0:
            sl2 = slice(h * SLOT, (h + 2) * SLOT)
            q2 = _dot(cqn, w_uq_ref[:, sl2])
            qr2 = _dot(cqn, w_uqr_ref[:, sl2])
        own = slice((h % 2) * SLOT, (h % 2 + 1) * SLOT)
        qh = q2[:, own] * cos + qr2[:, own] * sin
        ss = jnp.sum(qh * qh, axis=-1, keepdims=True)
        qh = qh * lax.rsqrt(ss * (1.0 / QK_HEAD) + EPS) * gq
        if sample_mode:
            qkh = (qh * gk).astype(BF16)
            qk_ref[:, sl] = qkh
            qabs_ref[:, h * KV_LORA:(h + 1) * KV_LORA] = _dot_nt(qkh, w_uk_ref[:, sl]).astype(BF16)
        else:
            q_ref[:, sl] = qh.astype(BF16)

    lat = _rms(ckv, g_kvl_ref[...])
    lat_ref[...] = lat
    lat_b = lat.astype(BF16)
    krot = zk[:, :SLOT] * cos + zk[:, SLOT:] * sin
    kpe_ref[...] = krot[:, QK_NOPE:QK_HEAD]
    ss_rot = jnp.sum(krot * krot, axis=-1, keepdims=True)
    for h in range(N_HEADS):
        sl = slice(h * SLOT, (h + 1) * SLOT)
        if h % 2 == 0:
            kn2 = _dot(lat_b, w_uk_ref[:, h * SLOT:(h + 2) * SLOT])
        kn = kn2[:, (h % 2) * SLOT:(h % 2 + 1) * SLOT]
        ss = jnp.sum(kn * kn, axis=-1, keepdims=True) + ss_rot
        kh = (kn + krot) * lax.rsqrt(ss * (1.0 / QK_HEAD) + EPS) * gk
        k_ref[:, sl] = kh.astype(BF16)
    v_ref[...] = _dot(lat_b, w_uv_ref[...]).astype(BF16)

    u = gc * hc
    ubuf[SUBLANES:SUBLANES + rows, :] = u
    u1 = ubuf[SUBLANES - 1:SUBLANES - 1 + rows, :]
    u2 = ubuf[SUBLANES - 2:SUBLANES - 2 + rows, :]
    if sample_mode:
        u1 = u1 * m1_ref[...] + s1_ref[...]
        u2 = u2 * m2_ref[...] + s2_ref[...]
        u_ref[...] = u
    else:
        utail_ref[...] = u[rows - SUBLANES:, :]
    y = cb_ref[...] + u2 * cw_ref[0:1, :] + u1 * cw_ref[1:2, :] + u * cw_ref[2:3, :]
    convn_ref[...] = _rms(gb * y, g_conv_ref[...]).astype(BF16)


def _mixer_call(x, cos_t, sin_t, conv_in, weights, *, sample_mode):
    wspecs = [_const_spec(w.shape) for w in weights]
    if sample_mode:
        rows = x.shape[0]
        grid = (1,)
        row = lambda w: pl.BlockSpec((rows, w), lambda i: (0, 0))
        m1, m2, s1, s2 = conv_in
        in_specs = [row(D_MODEL), row(SLOT), row(SLOT), row(1), row(1), row(CONV_CH), row(CONV_CH)]
        args = [x, cos_t, sin_t, m1, m2, s1, s2]
        widths = [(HEADS_W, BF16), (N_HEADS * KV_LORA, BF16), (HEADS_W, BF16), (ATTN_WIDTH, BF16),
                  (KV_LORA, F32), (QK_ROPE, F32), (CONV_CH, BF16), (CONV_CH, F32)]
        out_shape = [jax.ShapeDtypeStruct((rows, w), d) for w, d in widths]
        out_specs = [row(w) for w, _ in widths]
        sem = ("arbitrary",)
    else:
        nb, seq, _ = x.shape
        rows = ROW_TILE
        nt = seq // rows
        grid = (nb, nt)
        row3 = lambda w: pl.BlockSpec((None, rows, w), lambda b, t: (b, t, 0))
        tab = pl.BlockSpec((rows, SLOT), lambda b, t: (t, 0))
        in_specs = [row3(D_MODEL), tab, tab, _const_spec(conv_in.shape)]
        args = [x, cos_t, sin_t, conv_in]
        widths = [(HEADS_W, BF16), (HEADS_W, BF16), (ATTN_WIDTH, BF16), (KV_LORA, F32),
                  (QK_ROPE, F32), (CONV_CH, BF16)]
        out_shape = [jax.ShapeDtypeStruct((nb, seq, w), d) for w, d in widths]
        out_shape.append(jax.ShapeDtypeStruct((nb, nt, SUBLANES, CONV_CH), F32))
        out_specs = [row3(w) for w, _ in widths]
        out_specs.append(pl.BlockSpec((None, None, SUBLANES, CONV_CH), lambda b, t: (b, t, 0, 0)))
        sem = ("arbitrary", "arbitrary")
    return pl.pallas_call(
        functools.partial(_mixer_kernel, rows=rows, sample_mode=sample_mode),
        grid=grid,
        in_specs=in_specs + wspecs,
        out_specs=out_specs,
        out_shape=out_shape,
        scratch_shapes=[pltpu.VMEM((rows + 2 * SUBLANES, CONV_CH), F32)],
        compiler_params=pltpu.CompilerParams(dimension_semantics=sem, vmem_limit_bytes=VMEM_LIMIT),
        name="mixer_sample" if sample_mode else "mixer_prompt",
    )(*args, *weights)


def _softmax_step(s, state, pv):
    m_old, l_old, acc_old = state
    n = s.shape[1] // LANES
    m_new = jnp.maximum(m_old, jnp.max(s, axis=-1, keepdims=True))
    alpha = jnp.exp2(m_old - m_new)
    ps = [jnp.exp2(s[:, c * LANES:(c + 1) * LANES] - m_new) for c in range(n)]
    l_new = alpha * l_old + functools.reduce(lambda a, b: a + b, ps)
    p = ps[0] if n == 1 else jnp.concatenate(ps, axis=1)
    w = acc_old.shape[-1] // LANES
    alpha_w = alpha if w == 1 else jnp.concatenate([alpha] * w, axis=1)
    return m_new, l_new, alpha_w * acc_old + pv(p.astype(BF16))


def _softmax_update(s, m_ref, l_ref, acc_ref, pv):
    m_ref[...], l_ref[...], acc_ref[...] = _softmax_step(s, (m_ref[...], l_ref[...], acc_ref[...]), pv)


def _prompt_attn_kernel(q_ref, k_ref, v_ref, km_ref, vm_ref, o_ref, m_sc, l_sc, acc_sc):
    seq = q_ref.shape[0]
    half = ATT_TQ // 2
    lane = lax.broadcasted_iota(jnp.int32, (ATT_TQ, SLOT), 1)
    m_sc[...] = jnp.full(m_sc.shape, -jnp.inf, F32)
    l_sc[...] = jnp.zeros(l_sc.shape, F32)
    acc_sc[...] = jnp.zeros(acc_sc.shape, F32)

    meta = lane < N_META
    col_h = lax.broadcasted_iota(jnp.int32, (half, half), 1)
    row_h = lax.broadcasted_iota(jnp.int32, (half, half), 0)
    col_f = lax.broadcasted_iota(jnp.int32, (half, ATT_TK), 1)
    row_f = lax.broadcasted_iota(jnp.int32, (half, ATT_TK), 0)

    steps = []
    for qi in range(seq // ATT_TQ):
        for hh in range(2):
            steps.append((qi, hh, 0, ATT_TQ, None, meta))
        d0 = qi * ATT_TK
        for k0 in range(0, d0, ATT_TK_WIDE):
            for hh in range(2):
                steps.append((qi, hh, 0, ATT_TQ, slice(k0, min(k0 + ATT_TK_WIDE, d0)), None))
        for hh in range(2):
            steps.append((qi, hh, 0, half, slice(d0, d0 + half), col_h <= row_h))
            steps.append((qi, hh, half, half, slice(d0, d0 + ATT_TK), col_f <= row_f + half))
        steps.append((qi,))

    def scores(qi, hh, r0, nr, ks, mask):
        sl = slice(hh * SLOT, (hh + 1) * SLOT)
        k_blk = km_ref[:, sl] if ks is None else k_ref[ks, sl]
        s = _dot_nt(q_ref[pl.ds(qi * ATT_TQ + r0, nr), sl], k_blk)
        return s if mask is None else jnp.where(mask, s, -jnp.inf)

    def update(s, qi, hh, r0, nr, ks, mask):
        rows = pl.ds(r0, nr)
        v_blk = vm_ref[...] if ks is None else v_ref[ks, :]
        _softmax_update(s, m_sc.at[qi, hh, rows], l_sc.at[qi, hh, rows], acc_sc.at[qi, hh, rows],
                        lambda p: _dot(p, v_blk))

    def finalize(qi):
        outs = [acc_sc[qi, hh] / jnp.sum(l_sc[qi, hh], axis=-1, keepdims=True) for hh in range(2)]
        o_ref[qi * ATT_TQ:(qi + 1) * ATT_TQ, :] = (
            jnp.where(lane < V_HEAD, outs[0], outs[1]).astype(o_ref.dtype))

    pending = []
    for st in steps:
        pending.append((st, scores(*st) if len(st) > 1 else None))
        if len(pending) > ATT_LOOKAHEAD:
            d, s = pending.pop(0)
            finalize(*d) if s is None else update(s, *d)
    for d, s in pending:
        finalize(*d) if s is None else update(s, *d)


def _prompt_attn_call(q, k, v, k_meta, v_meta):
    nb, seq, _ = q.shape
    assert ATT_TQ == ATT_TK and seq % ATT_TQ == 0
    grid = (nb, N_HEADS // 2)
    return pl.pallas_call(
        _prompt_attn_kernel,
        grid=grid,
        in_specs=[
            pl.BlockSpec((None, seq, 2 * SLOT), lambda b, p: (b, 0, p)),
            pl.BlockSpec((None, seq, 2 * SLOT), lambda b, p: (b, 0, p)),
            pl.BlockSpec((None, seq, SLOT), lambda b, p: (b, 0, p)),
            pl.BlockSpec((LANES, 2 * SLOT), lambda b, p: (0, p)),
            pl.BlockSpec((LANES, SLOT), lambda b, p: (0, p)),
        ],
        out_specs=pl.BlockSpec((None, seq, SLOT), lambda b, p: (b, 0, p)),
        out_shape=jax.ShapeDtypeStruct((nb, seq, ATTN_WIDTH), BF16),
        scratch_shapes=[pltpu.VMEM((seq // ATT_TQ, 2, ATT_TQ, LANES), F32)] * 3,
        compiler_params=pltpu.CompilerParams(
            dimension_semantics=("arbitrary", "arbitrary"),
            vmem_limit_bytes=VMEM_LIMIT),
        name="prompt_attn",
    )(q, k, v, k_meta, v_meta)


def _sample_attn_kernel(pt_ref, qabs_ref, qpe_ref, wukt_ref, wuv_ref, tlat_ref, tkpe_ref,
                        lat_hbm, kpe_hbm, o_ref,
                        latb_sc, lhs_sc, lat_buf, kpe_buf, sems):
    n_pages = lat_buf.shape[1]
    chunk_pages = (CHUNK_PAGES,) * (n_pages // CHUNK_PAGES)
    b = pl.program_id(0)
    last = pl.num_programs(0) - 1
    slot = b % 2
    nq = qabs_ref.shape[0]
    reps = nq // N_HEADS
    nkn = N_HEADS * QK_NOPE
    qpe = qpe_ref[...]

    ngroup = lat_buf.shape[1] // DMA_GROUP

    def group_copies(bb, sl, gg):
        copies = []
        for j in range(gg * DMA_GROUP, (gg + 1) * DMA_GROUP):
            page = pt_ref[bb, j]
            copies.append(pltpu.make_async_copy(lat_hbm.at[page], lat_buf.at[sl, j], sems.at[0, sl]))
            copies.append(pltpu.make_async_copy(kpe_hbm.at[page], kpe_buf.at[sl, j], sems.at[1, sl]))
        return copies

    @pl.when(b == 0)
    def _():
        lhs_sc[0:nkn, :] = wukt_ref[...]
        for gg in range(ngroup):
            for cp in group_copies(0, 0, gg):
                cp.start()

    nxt = jnp.minimum(b + 1, last)

    lhs_sc[nkn:nkn + nq, :] = qabs_ref[...]
    state = (jnp.full((nq, LANES), -jnp.inf, F32), jnp.zeros((nq, LANES), F32),
             jnp.zeros((nq, KV_LORA), F32))

    def scores(lat_b, kpe_t):
        nk = lat_b.shape[0]
        full = _dot_nt(lhs_sc[...], lat_b)
        knt = full[:nkn]
        nsum = jnp.sum((knt * knt).reshape(QK_NOPE, N_HEADS, nk), axis=0)
        rsum = jnp.sum(kpe_t * kpe_t, axis=0, keepdims=True)
        r = lax.rsqrt((nsum + rsum) * (1.0 / QK_HEAD) + EPS)
        s = full[nkn:] + _dot(qpe, kpe_t.astype(BF16))
        return s * jnp.concatenate([r] * reps, axis=0)

    for gg in range(ngroup):
        for cp in group_copies(b, slot, gg):
            cp.wait()

    tail_b = tlat_ref[...].astype(BF16)
    s = scores(tail_b, tkpe_ref[...])
    key = lax.broadcasted_iota(jnp.int32, s.shape, 1)
    qt = lax.broadcasted_iota(jnp.int32, s.shape, 0) // N_HEADS
    s_tail = jnp.where(key <= qt, s, -jnp.inf)

    def chunk_scores(p0, npg):
        parts = [s_tail] if p0 == 0 else []
        for j in range(p0, p0 + npg, 2):
            for jj in (j, j + 1):
                latb_sc[jj * PAGE_SIZE:(jj + 1) * PAGE_SIZE, :] = lat_buf[slot, jj].astype(BF16)
            kpe_t = jnp.concatenate([kpe_buf[slot, j], kpe_buf[slot, j + 1]], axis=1)
            parts.append(scores(latb_sc[j * PAGE_SIZE:(j + 2) * PAGE_SIZE, :], kpe_t))
            if (j + 2) % DMA_ISSUE_PAGES == 0 and (j + 2) // DMA_ISSUE_PAGES <= ngroup:
                for cp in group_copies(nxt, 1 - slot, (j + 2) // DMA_ISSUE_PAGES - 1):
                    cp.start()
        return jnp.concatenate(parts, axis=1)

    def chunk_update(state, s, p0, npg):
        rows = slice(p0 * PAGE_SIZE, (p0 + npg) * PAGE_SIZE)
        if p0 == 0:
            pv = lambda p: _dot(p[:, :TAIL_KEYS], tail_b) + _dot(p[:, TAIL_KEYS:], latb_sc[rows, :])
        else:
            pv = lambda p: _dot(p, latb_sc[rows, :])
        return _softmax_step(s, state, pv)

    starts = [sum(chunk_pages[:i]) for i in range(len(chunk_pages))]
    pending = None
    for p0, npg in zip(starts, chunk_pages):
        s = chunk_scores(p0, npg)
        if pending is not None:
            state = chunk_update(state, *pending)
        pending = (s, p0, npg)
    state = chunk_update(state, *pending)

    _, l_fin, acc_fin = state
    o_lat = (acc_fin / jnp.sum(l_fin, axis=-1, keepdims=True)).astype(BF16)
    full = _dot(o_lat, wuv_ref[...])
    row_h = lax.broadcasted_iota(jnp.int32, full.shape, 0) % N_HEADS
    col_h = lax.broadcasted_iota(jnp.int32, full.shape, 1) // V_HEAD
    own = jnp.where(row_h == col_h, full, 0.0)
    o_ref[...] = jnp.sum(own.reshape(reps, N_HEADS, ATTN_WIDTH), axis=1)

    @pl.when(b == last)
    def _():
        for gg in range(ngroup):
            for cp in group_copies(last, 1 - slot, gg):
                cp.wait()


def _sample_attn_call(page_table, qabs, qpe, w_ukt, w_uv, tail_lat, tail_kpe_t, cache_lat, cache_kpe_t):
    nseq, n_pages = page_table.shape
    nq = qabs.shape[1]
    assert n_pages % DMA_GROUP == 0 and DMA_GROUP % 2 == 0 and DMA_ISSUE_PAGES % 2 == 0
    assert DMA_ISSUE_PAGES <= DMA_GROUP
    assert n_pages % CHUNK_PAGES == 0 and CHUNK_PAGES % 2 == 0
    grid = (nseq,)

    per_seq = lambda rows, width: pl.BlockSpec((None, rows, width), lambda b, pt: (b, 0, 0))
    whole = lambda shape: pl.BlockSpec(shape, lambda b, pt: (0,) * len(shape),
                                       pipeline_mode=pl.Buffered(1))
    in_specs = [per_seq(nq, KV_LORA), per_seq(nq, QK_ROPE), whole(w_ukt.shape), whole(w_uv.shape),
                per_seq(TAIL_KEYS, KV_LORA), per_seq(QK_ROPE, TAIL_KEYS),
                pl.BlockSpec(memory_space=pl.ANY), pl.BlockSpec(memory_space=pl.ANY)]
    reps = nq // N_HEADS
    return pl.pallas_call(
        _sample_attn_kernel,
        grid_spec=pltpu.PrefetchScalarGridSpec(
            num_scalar_prefetch=1,
            grid=grid,
            in_specs=in_specs,
            out_specs=pl.BlockSpec((None, reps, ATTN_WIDTH), lambda b, pt: (b, 0, 0)),
            scratch_shapes=[pltpu.VMEM((n_pages * PAGE_SIZE, KV_LORA), BF16),
                            pltpu.VMEM((N_HEADS * QK_NOPE + nq, KV_LORA), BF16),
                            pltpu.VMEM((2, n_pages, PAGE_SIZE, KV_LORA), F32),
                            pltpu.VMEM((2, n_pages, QK_ROPE, PAGE_SIZE), F32),
                            pltpu.SemaphoreType.DMA((2, 2))],
        ),
        out_shape=jax.ShapeDtypeStruct((nseq, reps, ATTN_WIDTH), F32),
        compiler_params=pltpu.CompilerParams(dimension_semantics=("arbitrary",),
                                             vmem_limit_bytes=VMEM_LIMIT),
        name="sample_attn",
    )(page_table, qabs, qpe, w_ukt, w_uv, tail_lat, tail_kpe_t, cache_lat, cache_kpe_t)


def _merge_ffn_kernel(x_ref, attn_ref, convn_ref, g_attn_ref, w_o_ref, g_ffn_ref, w_up_ref,
                      w_down_ref, y_ref):
    an = _rms(attn_ref[...].astype(F32), g_attn_ref[...]).astype(BF16)
    x1 = x_ref[...] + (_dot(an, w_o_ref[0:ATTN_WIDTH, :]) + _dot(convn_ref[...], w_o_ref[ATTN_WIDTH:, :]))
    hf = _rms(x1, g_ffn_ref[...]).astype(BF16)
    ffn = None
    for c in range(D_FF // FF_CHUNK):
        cs = slice(c * FF_CHUNK, (c + 1) * FF_CHUNK)
        up = jnp.maximum(_dot(hf, w_up_ref[:, cs]), 0.0)
        part = _dot((up * up).astype(BF16), w_down_ref[cs, :])
        ffn = part if ffn is None else ffn + part
    y_ref[...] = x1 + ffn


def _merge_ffn_call(x, attn, convn, weights, rows):
    n = x.shape[0]
    row = lambda w: pl.BlockSpec((rows, w), lambda i: (i, 0))
    return pl.pallas_call(
        _merge_ffn_kernel,
        grid=(n // rows,),
        in_specs=[row(D_MODEL), row(ATTN_WIDTH), row(CONV_CH)] + [_const_spec(w.shape) for w in weights],
        out_specs=row(D_MODEL),
        out_shape=jax.ShapeDtypeStruct((n, D_MODEL), F32),
        compiler_params=pltpu.CompilerParams(dimension_semantics=("arbitrary",),
                                             vmem_limit_bytes=VMEM_LIMIT),
        name="merge_ffn",
    )(x, attn, convn, *weights)


def _slots(w, width):
    k = w.shape[0]
    w = w.reshape(k, N_HEADS, width)
    return jnp.pad(w, ((0, 0), (0, 0), (0, SLOT - width))).reshape(k, HEADS_W)


def _rot_partner(w):
    return jnp.concatenate([-w[..., HALF_ROPE:], w[..., :HALF_ROPE]], axis=-1)


def _rope_slot(w):
    return jnp.pad(w, ((0, 0), (QK_NOPE, SLOT - QK_HEAD)))


def _layer_weights(w_in, q_lora_g, kv_lora_g, w_uq, w_ukv, q_norm_g, k_norm_g, conv_w, conv_b,
                   conv_out_g, norm_mix_g):
    o1 = Q_LORA
    o2 = o1 + KV_LORA
    o3 = o2 + QK_ROPE
    w_kpe = w_in[:, o2:o3]
    w_in_p = jnp.concatenate(
        [w_in[:, :o2], w_in[:, o3:], _rope_slot(w_kpe), _rope_slot(_rot_partner(w_kpe))],
        axis=1).astype(BF16)
    uq = w_uq.reshape(Q_LORA, N_HEADS, QK_HEAD)
    w_uq_p = _slots(w_uq, QK_HEAD).astype(BF16)
    uq_rot = jnp.pad(_rot_partner(uq[..., QK_NOPE:]), ((0, 0), (0, 0), (QK_NOPE, SLOT - QK_HEAD)))
    w_uq_r = uq_rot.reshape(Q_LORA, HEADS_W).astype(BF16)
    ukv = w_ukv.reshape(KV_LORA, N_HEADS, QK_NOPE + V_HEAD)
    w_uk_p = _slots(ukv[..., :QK_NOPE].reshape(KV_LORA, N_HEADS * QK_NOPE), QK_NOPE).astype(BF16)
    w_uv = ukv[..., QK_NOPE:].reshape(KV_LORA, ATTN_WIDTH).astype(BF16)
    w_ukt = jnp.transpose(ukv[..., :QK_NOPE], (2, 1, 0)).reshape(N_HEADS * QK_NOPE, KV_LORA).astype(BF16)
    pad_g = lambda g: jnp.pad(g, (0, SLOT - QK_HEAD))[None, :]
    gq = pad_g(q_norm_g) * (ATTN_SCALE * LOG2_E)
    gk = pad_g(k_norm_g)
    mixer = (norm_mix_g[None, :], w_in_p, q_lora_g[None, :], w_uq_p, w_uq_r, kv_lora_g[None, :],
             w_uk_p, w_uv, gq, gk, conv_w, conv_b[None, :], conv_out_g[None, :])
    return mixer, w_ukt, w_uv


def _rope_slot_tables(pos):
    inv_freq = ROPE_THETA ** (-(jnp.arange(0, QK_ROPE, 2, dtype=F32) / QK_ROPE))
    ang = pos.astype(F32)[:, None] * inv_freq[None, :]
    n = pos.shape[0]
    cos2 = jnp.concatenate([jnp.cos(ang)] * 2, axis=1)
    sin2 = jnp.concatenate([jnp.sin(ang)] * 2, axis=1)
    cos_t = jnp.concatenate([jnp.ones((n, QK_NOPE), F32), cos2, jnp.zeros((n, SLOT - QK_HEAD), F32)], axis=1)
    sin_t = jnp.pad(sin2, ((0, 0), (QK_NOPE, SLOT - QK_HEAD)))
    return cos_t, sin_t


def kernel(x_prompt, x_sample, cache_kv_latent, cache_k_rope, state_conv, page_table, meta_tokens,
           norm_mix_g, w_in, q_lora_g, kv_lora_g, w_uq, w_ukv, q_norm_g, k_norm_g, conv_w, conv_b,
           attn_out_g, conv_out_g, w_o, norm_ffn_g, w_up, w_down):
    depth = w_in.shape[0]
    assert depth == 1, "the prompt and sample streams are chained for a single layer"
    nb, seq, _ = x_prompt.shape
    nseq, dec, _ = x_sample.shape
    past = page_table.shape[1] * PAGE_SIZE
    l = 0

    mixer_w, w_ukt, w_uv = _layer_weights(w_in[l], q_lora_g[l], kv_lora_g[l], w_uq[l], w_ukv[l],
                                          q_norm_g[l], k_norm_g[l], conv_w[l], conv_b[l],
                                          conv_out_g[l], norm_mix_g[l])
    ffn_w = (attn_out_g[l][None, :], w_o[l].astype(BF16), norm_ffn_g[l][None, :],
             w_up[l].astype(BF16), w_down[l].astype(BF16))

    ns = nseq * dec
    xs_rows = jnp.concatenate([x_sample.reshape(ns, D_MODEL), meta_tokens.astype(F32)], axis=0)
    pos_s = jnp.concatenate([jnp.tile(past + jnp.arange(dec), nseq), jnp.arange(N_META)])
    cos_s, sin_s = _rope_slot_tables(pos_s)
    t_in_seq = jnp.concatenate([jnp.tile(jnp.arange(dec), nseq), jnp.arange(N_META)])
    m1 = (t_in_seq >= 1).astype(F32)[:, None]
    m2 = (t_in_seq >= 2).astype(F32)[:, None]
    st = state_conv[l].astype(F32)
    zrow = jnp.zeros((nseq, 1, CONV_CH), F32)
    s1 = jnp.concatenate([st[:, 1:2], zrow, zrow, zrow], axis=1)[:, :dec]
    s2 = jnp.concatenate([st[:, 0:1], st[:, 1:2], zrow, zrow], axis=1)[:, :dec]
    zmeta = jnp.zeros((N_META, CONV_CH), F32)
    s1 = jnp.concatenate([s1.reshape(ns, CONV_CH), zmeta], axis=0)
    s2 = jnp.concatenate([s2.reshape(ns, CONV_CH), zmeta], axis=0)
    (qk_s, qabs_s, k_s, v_s, lat_s, kpe_s, convn_s, u_s) = _mixer_call(
        xs_rows, cos_s, sin_s, (m1, m2, s1, s2), mixer_w, sample_mode=True)

    pad_meta = lambda a: jnp.pad(a[ns:], ((0, LANES - N_META), (0, 0)))
    k_meta, v_meta = pad_meta(k_s), pad_meta(v_s)
    lat_meta, kpe_meta, u_meta = lat_s[ns:], kpe_s[ns:], u_s[ns:]

    cos_p, sin_p = _rope_slot_tables(N_META + jnp.arange(seq))
    q_p, k_p, v_p, lat_p, kpe_p, convn_p, utail_p = _mixer_call(
        x_prompt, cos_p, sin_p, u_meta[N_META - SUBLANES:], mixer_w, sample_mode=False)
    attn_p = _prompt_attn_call(q_p, k_p, v_p, k_meta, v_meta)
    y_prompt = _merge_ffn_call(x_prompt.reshape(nb * seq, D_MODEL), attn_p.reshape(nb * seq, ATTN_WIDTH),
                               convn_p.reshape(nb * seq, CONV_CH), ffn_w, ROW_TILE)
    y_prompt = y_prompt.reshape(nb, seq, D_MODEL)
    bcast = lambda a: jnp.broadcast_to(a[None], (nb,) + a.shape)
    new_lat_prompt = jnp.concatenate([bcast(lat_meta), lat_p], axis=1)[None]
    new_kpe_prompt = jnp.concatenate([bcast(kpe_meta), kpe_p], axis=1)[None]
    new_conv_prompt = utail_p[:, -1, SUBLANES - (CONV_W - 1):][None]

    nq = dec * N_HEADS
    qabs = qabs_s[:ns].reshape(nseq, nq, KV_LORA)
    qpe = qk_s[:ns].reshape(ns, N_HEADS, SLOT)[:, :, QK_NOPE:QK_HEAD].reshape(nseq, nq, QK_ROPE)
    assert dec <= TAIL_KEYS
    pad_tail = lambda a: jnp.pad(a[:ns].reshape(nseq, dec, -1), ((0, 0), (0, TAIL_KEYS - dec), (0, 0)))
    attn_s = _sample_attn_call(page_table, qabs, qpe, w_ukt, w_uv, pad_tail(lat_s),
                               jnp.swapaxes(pad_tail(kpe_s), 1, 2),
                               cache_kv_latent[l], jnp.swapaxes(cache_k_rope[l], 1, 2))
    y_sample = _merge_ffn_call(x_sample.reshape(ns, D_MODEL), attn_s.reshape(ns, ATTN_WIDTH),
                               convn_s[:ns], ffn_w, ns)
    y_sample = y_sample.reshape(nseq, dec, D_MODEL)
    new_lat_sample = lat_s[:ns].reshape(nseq, dec, KV_LORA)[None]
    new_kpe_sample = kpe_s[:ns].reshape(nseq, dec, QK_ROPE)[None]
    us = jnp.concatenate([st, u_s[:ns].reshape(nseq, dec, CONV_CH)], axis=1)
    new_conv_sample = us[:, -(CONV_W - 1):][None]

    return (y_prompt, y_sample, new_lat_prompt, new_kpe_prompt, new_conv_prompt,
            new_lat_sample, new_kpe_sample, new_conv_sample)
```

```python
import functools

import jax
import jax.numpy as jnp
import numpy as np
from jax import lax
from jax.experimental import pallas as pl
from jax.experimental.pallas import tpu as pltpu

D_MODEL = 1024
N_META = 16
N_HEADS = 8
QK_NOPE = 64
QK_ROPE = 32
V_HEAD = 64
QK_HEAD = QK_NOPE + QK_ROPE
Q_LORA = 384
KV_LORA = 256
ATTN_WIDTH = N_HEADS * V_HEAD
CONV_CH = D_MODEL - ATTN_WIDTH
CONV_W = 3
D_FF = 4 * D_MODEL
ROPE_THETA = 10000.0
EPS = 1e-6
PAGE_SIZE = 128
ATTN_SCALE = QK_HEAD ** -0.5
LOG2_E = 1.4426950408889634

LANES = 128
SUBLANES = 8
SLOT = LANES
HEADS_W = N_HEADS * SLOT
HALF_ROPE = QK_ROPE // 2

C_CQ = 0
C_CKV = C_CQ + Q_LORA
C_GB = C_CKV + KV_LORA
C_GC = C_GB + CONV_CH
C_HC = C_GC + CONV_CH
C_KPE = C_HC + CONV_CH
C_KPR = C_KPE + SLOT
IN_W = C_KPR + SLOT

ROW_TILE = 512
ATT_TQ = 256
ATT_TK = 256
ATT_TK_WIDE = 2048
ATT_LOOKAHEAD = 4
FF_CHUNK = 1024
CHUNK_PAGES = 8
DMA_GROUP = 8
DMA_ISSUE_PAGES = 2
TAIL_KEYS = PAGE_SIZE
VMEM_LIMIT = 52 * 1024 * 1024

BF16 = jnp.bfloat16
F32 = jnp.float32

_NT = (((1,), (1,)), ((), ()))


def _dot(a, b):
    return jnp.dot(a, b, preferred_element_type=F32)


def _dot_nt(a, b):
    return lax.dot_general(a, b, _NT, preferred_element_type=F32)


def _rms(x, g):
    return x * lax.rsqrt(jnp.mean(x * x, axis=-1, keepdims=True) + EPS) * g


def _const_spec(shape):
    nd = len(shape)
    return pl.BlockSpec(shape, lambda *_: (0,) * nd, pipeline_mode=pl.Buffered(1))


def _mixer_kernel(*refs, rows, sample_mode):
    it = iter(refs)
    x_ref, cos_ref, sin_ref = next(it), next(it), next(it)
    if sample_mode:
        m1_ref, m2_ref, s1_ref, s2_ref = next(it), next(it), next(it), next(it)
    else:
        tail_in_ref, lat_meta_ref = next(it), next(it)
    (g_mix_ref, w_in_ref, g_ql_ref, w_uq_ref, w_uqr_ref, g_kvl_ref, w_uk_ref, w_uv_ref,
     gq_ref, gk_ref, cw_ref, cb_ref, g_conv_ref) = (next(it) for _ in range(13))
    if sample_mode:
        (qk_ref, qabs_ref, k_ref, v_ref, lat_ref, kpe_ref, convn_ref, u_ref) = (next(it) for _ in range(8))
    else:
        (q_ref, k_ref, v_ref, lat_hbm, kpe_ref, convn_ref, utail_ref) = (next(it) for _ in range(7))
    ubuf = next(it)

    if sample_mode:
        ubuf[0:SUBLANES, :] = jnp.zeros((SUBLANES, CONV_CH), F32)
    else:
        lat_stage, lat_sems = next(it), next(it)
        b = pl.program_id(0)
        t = pl.program_id(1)
        lat_copy = pltpu.make_async_copy(
            lat_stage, lat_hbm.at[b, pl.ds(N_META + t * rows, rows)], lat_sems.at[0])

        @pl.when(t == 0)
        def _():
            ubuf[0:SUBLANES, :] = tail_in_ref[...]
            meta_copy = pltpu.make_async_copy(lat_meta_ref, lat_hbm.at[b, pl.ds(0, N_META)], lat_sems.at[1])
            meta_copy.start()
            meta_copy.wait()

        @pl.when(t != 0)
        def _():
            ubuf[0:SUBLANES, :] = ubuf[rows:rows + SUBLANES, :]

    hn = _rms(x_ref[...], g_mix_ref[...]).astype(BF16)
    cos = cos_ref[...]
    sin = sin_ref[...]

    cq = _dot(hn, w_in_ref[:, C_CQ:C_CKV])
    ckv = _dot(hn, w_in_ref[:, C_CKV:C_GB])
    zk = _dot(hn, w_in_ref[:, C_KPE:IN_W])
    gc = _dot(hn, w_in_ref[:, C_GC:C_HC])
    hc = _dot(hn, w_in_ref[:, C_HC:C_KPE])
    gb = _dot(hn, w_in_ref[:, C_GB:C_GC])

    cqn = _rms(cq, g_ql_ref[...]).astype(BF16)
    gq = gq_ref[...]
    gk = gk_ref[...]
    for h in range(N_HEADS):
        sl = slice(h * SLOT, (h + 1) * SLOT)
        if h % 2 == 0:
            sl2 = slice(h * SLOT, (h + 2) * SLOT)
            q2 = _dot(cqn, w_uq_ref[:, sl2])
            qr2 = _dot(cqn, w_uqr_ref[:, sl2])
        own = slice((h % 2) * SLOT, (h % 2 + 1) * SLOT)
        qh = q2[:, own] * cos + qr2[:, own] * sin
        ss = jnp.sum(qh * qh, axis=-1, keepdims=True)
        qh = qh * lax.rsqrt(ss * (1.0 / QK_HEAD) + EPS) * gq
        if sample_mode:
            qkh = (qh * gk).astype(BF16)
            qk_ref[:, sl] = qkh
            qabs_ref[:, h * KV_LORA:(h + 1) * KV_LORA] = _dot_nt(qkh, w_uk_ref[:, sl]).astype(BF16)
        else:
            q_ref[:, sl] = qh.astype(BF16)

    lat = _rms(ckv, g_kvl_ref[...])
    if sample_mode:
        lat_ref[...] = lat
    else:
        lat_stage[...] = lat
        lat_copy.start()
    lat_b = lat.astype(BF16)
    krot = zk[:, :SLOT] * cos + zk[:, SLOT:] * sin
    kpe_ref[...] = krot[:, QK_NOPE:QK_HEAD]
    ss_rot = jnp.sum(krot * krot, axis=-1, keepdims=True)
    for h in range(N_HEADS):
        sl = slice(h * SLOT, (h + 1) * SLOT)
        if h % 2 == 0:
            kn2 = _dot(lat_b, w_uk_ref[:, h * SLOT:(h + 2) * SLOT])
        kn = kn2[:, (h % 2) * SLOT:(h % 2 + 1) * SLOT]
        ss = jnp.sum(kn * kn, axis=-1, keepdims=True) + ss_rot
        kh = (kn + krot) * lax.rsqrt(ss * (1.0 / QK_HEAD) + EPS) * gk
        k_ref[:, sl] = kh.astype(BF16)
    v_ref[...] = _dot(lat_b, w_uv_ref[...]).astype(BF16)

    u = gc * hc
    ubuf[SUBLANES:SUBLANES + rows, :] = u
    u1 = ubuf[SUBLANES - 1:SUBLANES - 1 + rows, :]
    u2 = ubuf[SUBLANES - 2:SUBLANES - 2 + rows, :]
    if sample_mode:
        u1 = u1 * m1_ref[...] + s1_ref[...]
        u2 = u2 * m2_ref[...] + s2_ref[...]
        u_ref[...] = u
    else:
        utail_ref[...] = u[rows - SUBLANES:, :]
    y = cb_ref[...] + u2 * cw_ref[0:1, :] + u1 * cw_ref[1:2, :] + u * cw_ref[2:3, :]
    convn_ref[...] = _rms(gb * y, g_conv_ref[...]).astype(BF16)
    if not sample_mode:
        lat_copy.wait()


def _mixer_call(x, cos_t, sin_t, conv_in, weights, *, sample_mode):
    wspecs = [_const_spec(w.shape) for w in weights]
    if sample_mode:
        rows = x.shape[0]
        grid = (1,)
        row = lambda w: pl.BlockSpec((rows, w), lambda i: (0, 0))
        m1, m2, s1, s2 = conv_in
        in_specs = [row(D_MODEL), row(SLOT), row(SLOT), row(1), row(1), row(CONV_CH), row(CONV_CH)]
        args = [x, cos_t, sin_t, m1, m2, s1, s2]
        widths = [(HEADS_W, BF16), (N_HEADS * KV_LORA, BF16), (HEADS_W, BF16), (ATTN_WIDTH, BF16),
                  (KV_LORA, F32), (QK_ROPE, F32), (CONV_CH, BF16), (CONV_CH, F32)]
        out_shape = [jax.ShapeDtypeStruct((rows, w), d) for w, d in widths]
        out_specs = [row(w) for w, _ in widths]
        sem = ("arbitrary",)
        scratch = []
    else:
        nb, seq, _ = x.shape
        rows = ROW_TILE
        nt = seq // rows
        grid = (nb, nt)
        row3 = lambda w: pl.BlockSpec((None, rows, w), lambda b, t: (b, t, 0))
        tab = pl.BlockSpec((rows, SLOT), lambda b, t: (t, 0))
        u_tail, lat_meta = conv_in
        in_specs = [row3(D_MODEL), tab, tab, _const_spec(u_tail.shape), _const_spec(lat_meta.shape)]
        args = [x, cos_t, sin_t, u_tail, lat_meta]
        widths = [(HEADS_W, BF16), (HEADS_W, BF16), (ATTN_WIDTH, BF16), (KV_LORA, F32),
                  (QK_ROPE, F32), (CONV_CH, BF16)]
        out_shape = [jax.ShapeDtypeStruct((nb, seq, w), d) for w, d in widths]
        out_shape.append(jax.ShapeDtypeStruct((nb, nt, SUBLANES, CONV_CH), F32))
        out_specs = [row3(w) for w, _ in widths]
        out_specs.append(pl.BlockSpec((None, None, SUBLANES, CONV_CH), lambda b, t: (b, t, 0, 0)))
        out_shape[3] = jax.ShapeDtypeStruct((nb, N_META + seq, KV_LORA), F32)
        out_specs[3] = pl.BlockSpec(memory_space=pl.ANY)
        sem = ("arbitrary", "arbitrary")
        scratch = [pltpu.VMEM((rows, KV_LORA), F32), pltpu.SemaphoreType.DMA((2,))]
    return pl.pallas_call(
        functools.partial(_mixer_kernel, rows=rows, sample_mode=sample_mode),
        grid=grid,
        in_specs=in_specs + wspecs,
        out_specs=out_specs,
        out_shape=out_shape,
        scratch_shapes=[pltpu.VMEM((rows + 2 * SUBLANES, CONV_CH), F32)] + scratch,
        compiler_params=pltpu.CompilerParams(dimension_semantics=sem, vmem_limit_bytes=VMEM_LIMIT),
        name="mixer_sample" if sample_mode else "mixer_prompt",
    )(*args, *weights)


def _softmax_step(s, state, pv):
    m_old, l_old, acc_old = state
    n = s.shape[1] // LANES
    m_new = jnp.maximum(m_old, jnp.max(s, axis=-1, keepdims=True))
    alpha = jnp.exp2(m_old - m_new)
    ps = [jnp.exp2(s[:, c * LANES:(c + 1) * LANES] - m_new) for c in range(n)]
    l_new = alpha * l_old + functools.reduce(lambda a, b: a + b, ps)
    p = ps[0] if n == 1 else jnp.concatenate(ps, axis=1)
    w = acc_old.shape[-1] // LANES
    alpha_w = alpha if w == 1 else jnp.concatenate([alpha] * w, axis=1)
    return m_new, l_new, alpha_w * acc_old + pv(p.astype(BF16))


def _softmax_update(s, m_ref, l_ref, acc_ref, pv):
    m_ref[...], l_ref[...], acc_ref[...] = _softmax_step(s, (m_ref[...], l_ref[...], acc_ref[...]), pv)


def _prompt_attn_kernel(q_ref, k_ref, v_ref, km_ref, vm_ref, o_ref, m_sc, l_sc, acc_sc):
    seq = q_ref.shape[0]
    half = ATT_TQ // 2
    lane = lax.broadcasted_iota(jnp.int32, (ATT_TQ, SLOT), 1)
    m_sc[...] = jnp.full(m_sc.shape, -jnp.inf, F32)
    l_sc[...] = jnp.zeros(l_sc.shape, F32)
    acc_sc[...] = jnp.zeros(acc_sc.shape, F32)

    meta = lane < N_META
    col_h = lax.broadcasted_iota(jnp.int32, (half, half), 1)
    row_h = lax.broadcasted_iota(jnp.int32, (half, half), 0)
    col_f = lax.broadcasted_iota(jnp.int32, (half, ATT_TK), 1)
    row_f = lax.broadcasted_iota(jnp.int32, (half, ATT_TK), 0)

    steps = []
    for qi in range(seq // ATT_TQ):
        for hh in range(2):
            steps.append((qi, hh, 0, ATT_TQ, None, meta))
        d0 = qi * ATT_TK
        for k0 in range(0, d0, ATT_TK_WIDE):
            for hh in range(2):
                steps.append((qi, hh, 0, ATT_TQ, slice(k0, min(k0 + ATT_TK_WIDE, d0)), None))
        for hh in range(2):
            steps.append((qi, hh, 0, half, slice(d0, d0 + half), col_h <= row_h))
            steps.append((qi, hh, half, half, slice(d0, d0 + ATT_TK), col_f <= row_f + half))
        steps.append((qi,))

    def scores(qi, hh, r0, nr, ks, mask):
        sl = slice(hh * SLOT, (hh + 1) * SLOT)
        k_blk = km_ref[:, sl] if ks is None else k_ref[ks, sl]
        s = _dot_nt(q_ref[pl.ds(qi * ATT_TQ + r0, nr), sl], k_blk)
        return s if mask is None else jnp.where(mask, s, -jnp.inf)

    def update(s, qi, hh, r0, nr, ks, mask):
        rows = pl.ds(r0, nr)
        v_blk = vm_ref[...] if ks is None else v_ref[ks, :]
        _softmax_update(s, m_sc.at[qi, hh, rows], l_sc.at[qi, hh, rows], acc_sc.at[qi, hh, rows],
                        lambda p: _dot(p, v_blk))

    def finalize(qi):
        outs = [acc_sc[qi, hh] / jnp.sum(l_sc[qi, hh], axis=-1, keepdims=True) for hh in range(2)]
        o_ref[qi * ATT_TQ:(qi + 1) * ATT_TQ, :] = (
            jnp.where(lane < V_HEAD, outs[0], outs[1]).astype(o_ref.dtype))

    pending = []
    for st in steps:
        pending.append((st, scores(*st) if len(st) > 1 else None))
        if len(pending) > ATT_LOOKAHEAD:
            d, s = pending.pop(0)
            finalize(*d) if s is None else update(s, *d)
    for d, s in pending:
        finalize(*d) if s is None else update(s, *d)


def _prompt_attn_call(q, k, v, k_meta, v_meta):
    nb, seq, _ = q.shape
    assert ATT_TQ == ATT_TK and seq % ATT_TQ == 0
    grid = (nb, N_HEADS // 2)
    return pl.pallas_call(
        _prompt_attn_kernel,
        grid=grid,
        in_specs=[
            pl.BlockSpec((None, seq, 2 * SLOT), lambda b, p: (b, 0, p)),
            pl.BlockSpec((None, seq, 2 * SLOT), lambda b, p: (b, 0, p)),
            pl.BlockSpec((None, seq, SLOT), lambda b, p: (b, 0, p)),
            pl.BlockSpec((LANES, 2 * SLOT), lambda b, p: (0, p)),
            pl.BlockSpec((LANES, SLOT), lambda b, p: (0, p)),
        ],
        out_specs=pl.BlockSpec((None, seq, SLOT), lambda b, p: (b, 0, p)),
        out_shape=jax.ShapeDtypeStruct((nb, seq, ATTN_WIDTH), BF16),
        scratch_shapes=[pltpu.VMEM((seq // ATT_TQ, 2, ATT_TQ, LANES), F32)] * 3,
        compiler_params=pltpu.CompilerParams(
            dimension_semantics=("arbitrary", "arbitrary"),
            vmem_limit_bytes=VMEM_LIMIT),
        name="prompt_attn",
    )(q, k, v, k_meta, v_meta)


def _sample_attn_kernel(pt_ref, qabs_ref, qpe_ref, wukt_ref, wuv_ref, tlat_ref, tkpe_ref,
                        lat_hbm, kpe_hbm, o_ref,
                        latb_sc, lhs_sc, lat_buf, kpe_buf, sems):
    n_pages = lat_buf.shape[1]
    chunk_pages = (CHUNK_PAGES,) * (n_pages // CHUNK_PAGES)
    b = pl.program_id(0)
    last = pl.num_programs(0) - 1
    slot = b % 2
    nq = qabs_ref.shape[0]
    reps = nq // N_HEADS
    nkn = N_HEADS * QK_NOPE
    qpe = qpe_ref[...]

    ngroup = lat_buf.shape[1] // DMA_GROUP

    def group_copies(bb, sl, gg):
        copies = []
        for j in range(gg * DMA_GROUP, (gg + 1) * DMA_GROUP):
            page = pt_ref[bb, j]
            copies.append(pltpu.make_async_copy(lat_hbm.at[page], lat_buf.at[sl, j], sems.at[0, sl]))
            copies.append(pltpu.make_async_copy(kpe_hbm.at[page], kpe_buf.at[sl, j], sems.at[1, sl]))
        return copies

    @pl.when(b == 0)
    def _():
        lhs_sc[0:nkn, :] = wukt_ref[...]
        for gg in range(ngroup):
            for cp in group_copies(0, 0, gg):
                cp.start()

    nxt = jnp.minimum(b + 1, last)

    lhs_sc[nkn:nkn + nq, :] = qabs_ref[...]
    state = (jnp.full((nq, LANES), -jnp.inf, F32), jnp.zeros((nq, LANES), F32),
             jnp.zeros((nq, KV_LORA), F32))

    def scores(lat_b, kpe_t):
        nk = lat_b.shape[0]
        full = _dot_nt(lhs_sc[...], lat_b)
        knt = full[:nkn]
        nsum = jnp.sum((knt * knt).reshape(QK_NOPE, N_HEADS, nk), axis=0)
        rsum = jnp.sum(kpe_t * kpe_t, axis=0, keepdims=True)
        r = lax.rsqrt((nsum + rsum) * (1.0 / QK_HEAD) + EPS)
        s = full[nkn:] + _dot(qpe, kpe_t.astype(BF16))
        return s * jnp.concatenate([r] * reps, axis=0)

    for gg in range(ngroup):
        for cp in group_copies(b, slot, gg):
            cp.wait()

    tail_b = tlat_ref[...].astype(BF16)
    s = scores(tail_b, tkpe_ref[...])
    key = lax.broadcasted_iota(jnp.int32, s.shape, 1)
    qt = lax.broadcasted_iota(jnp.int32, s.shape, 0) // N_HEADS
    s_tail = jnp.where(key <= qt, s, -jnp.inf)

    def chunk_scores(p0, npg):
        parts = [s_tail] if p0 == 0 else []
        for j in range(p0, p0 + npg, 2):
            for jj in (j, j + 1):
                latb_sc[jj * PAGE_SIZE:(jj + 1) * PAGE_SIZE, :] = lat_buf[slot, jj].astype(BF16)
            kpe_t = jnp.concatenate([kpe_buf[slot, j], kpe_buf[slot, j + 1]], axis=1)
            parts.append(scores(latb_sc[j * PAGE_SIZE:(j + 2) * PAGE_SIZE, :], kpe_t))
            if (j + 2) % DMA_ISSUE_PAGES == 0 and (j + 2) // DMA_ISSUE_PAGES <= ngroup:
                for cp in group_copies(nxt, 1 - slot, (j + 2) // DMA_ISSUE_PAGES - 1):
                    cp.start()
        return jnp.concatenate(parts, axis=1)

    def chunk_update(state, s, p0, npg):
        rows = slice(p0 * PAGE_SIZE, (p0 + npg) * PAGE_SIZE)
        if p0 == 0:
            pv = lambda p: _dot(p[:, :TAIL_KEYS], tail_b) + _dot(p[:, TAIL_KEYS:], latb_sc[rows, :])
        else:
            pv = lambda p: _dot(p, latb_sc[rows, :])
        return _softmax_step(s, state, pv)

    starts = [sum(chunk_pages[:i]) for i in range(len(chunk_pages))]
    pending = None
    for p0, npg in zip(starts, chunk_pages):
        s = chunk_scores(p0, npg)
        if pending is not None:
            state = chunk_update(state, *pending)
        pending = (s, p0, npg)
    state = chunk_update(state, *pending)

    _, l_fin, acc_fin = state
    o_lat = (acc_fin / jnp.sum(l_fin, axis=-1, keepdims=True)).astype(BF16)
    full = _dot(o_lat, wuv_ref[...])
    row_h = lax.broadcasted_iota(jnp.int32, full.shape, 0) % N_HEADS
    col_h = lax.broadcasted_iota(jnp.int32, full.shape, 1) // V_HEAD
    own = jnp.where(row_h == col_h, full, 0.0)
    o_ref[...] = jnp.sum(own.reshape(reps, N_HEADS, ATTN_WIDTH), axis=1)

    @pl.when(b == last)
    def _():
        for gg in range(ngroup):
            for cp in group_copies(last, 1 - slot, gg):
                cp.wait()


def _sample_attn_call(page_table, qabs, qpe, w_ukt, w_uv, tail_lat, tail_kpe_t, cache_lat, cache_kpe_t):
    nseq, n_pages = page_table.shape
    nq = qabs.shape[1]
    assert n_pages % DMA_GROUP == 0 and DMA_GROUP % 2 == 0 and DMA_ISSUE_PAGES % 2 == 0
    assert DMA_ISSUE_PAGES <= DMA_GROUP
    assert n_pages % CHUNK_PAGES == 0 and CHUNK_PAGES % 2 == 0
    grid = (nseq,)

    per_seq = lambda rows, width: pl.BlockSpec((None, rows, width), lambda b, pt: (b, 0, 0))
    whole = lambda shape: pl.BlockSpec(shape, lambda b, pt: (0,) * len(shape),
                                       pipeline_mode=pl.Buffered(1))
    in_specs = [per_seq(nq, KV_LORA), per_seq(nq, QK_ROPE), whole(w_ukt.shape), whole(w_uv.shape),
                per_seq(TAIL_KEYS, KV_LORA), per_seq(QK_ROPE, TAIL_KEYS),
                pl.BlockSpec(memory_space=pl.ANY), pl.BlockSpec(memory_space=pl.ANY)]
    reps = nq // N_HEADS
    return pl.pallas_call(
        _sample_attn_kernel,
        grid_spec=pltpu.PrefetchScalarGridSpec(
            num_scalar_prefetch=1,
            grid=grid,
            in_specs=in_specs,
            out_specs=pl.BlockSpec((None, reps, ATTN_WIDTH), lambda b, pt: (b, 0, 0)),
            scratch_shapes=[pltpu.VMEM((n_pages * PAGE_SIZE, KV_LORA), BF16),
                            pltpu.VMEM((N_HEADS * QK_NOPE + nq, KV_LORA), BF16),
                            pltpu.VMEM((2, n_pages, PAGE_SIZE, KV_LORA), F32),
                            pltpu.VMEM((2, n_pages, QK_ROPE, PAGE_SIZE), F32),
                            pltpu.SemaphoreType.DMA((2, 2))],
        ),
        out_shape=jax.ShapeDtypeStruct((nseq, reps, ATTN_WIDTH), F32),
        compiler_params=pltpu.CompilerParams(dimension_semantics=("arbitrary",),
                                             vmem_limit_bytes=VMEM_LIMIT),
        name="sample_attn",
    )(page_table, qabs, qpe, w_ukt, w_uv, tail_lat, tail_kpe_t, cache_lat, cache_kpe_t)


def _merge_ffn_kernel(x_ref, attn_ref, convn_ref, g_attn_ref, w_o_ref, g_ffn_ref, w_up_ref,
                      w_down_ref, y_ref):
    an = _rms(attn_ref[...].astype(F32), g_attn_ref[...]).astype(BF16)
    x1 = x_ref[...] + (_dot(an, w_o_ref[0:ATTN_WIDTH, :]) + _dot(convn_ref[...], w_o_ref[ATTN_WIDTH:, :]))
    hf = _rms(x1, g_ffn_ref[...]).astype(BF16)
    ffn = None
    for c in range(D_FF // FF_CHUNK):
        cs = slice(c * FF_CHUNK, (c + 1) * FF_CHUNK)
        up = jnp.maximum(_dot(hf, w_up_ref[:, cs]), 0.0)
        part = _dot((up * up).astype(BF16), w_down_ref[cs, :])
        ffn = part if ffn is None else ffn + part
    y_ref[...] = x1 + ffn


def _merge_ffn_call(x, attn, convn, weights, rows):
    n = x.shape[0]
    row = lambda w: pl.BlockSpec((rows, w), lambda i: (i, 0))
    return pl.pallas_call(
        _merge_ffn_kernel,
        grid=(n // rows,),
        in_specs=[row(D_MODEL), row(ATTN_WIDTH), row(CONV_CH)] + [_const_spec(w.shape) for w in weights],
        out_specs=row(D_MODEL),
        out_shape=jax.ShapeDtypeStruct((n, D_MODEL), F32),
        compiler_params=pltpu.CompilerParams(dimension_semantics=("arbitrary",),
                                             vmem_limit_bytes=VMEM_LIMIT),
        name="merge_ffn",
    )(x, attn, convn, *weights)


def _slots(w, width):
    k = w.shape[0]
    w = w.reshape(k, N_HEADS, width)
    return jnp.pad(w, ((0, 0), (0, 0), (0, SLOT - width))).reshape(k, HEADS_W)


def _rot_partner(w):
    return jnp.concatenate([-w[..., HALF_ROPE:], w[..., :HALF_ROPE]], axis=-1)


def _rope_slot(w):
    return jnp.pad(w, ((0, 0), (QK_NOPE, SLOT - QK_HEAD)))


def _layer_weights(w_in, q_lora_g, kv_lora_g, w_uq, w_ukv, q_norm_g, k_norm_g, conv_w, conv_b,
                   conv_out_g, norm_mix_g):
    o1 = Q_LORA
    o2 = o1 + KV_LORA
    o3 = o2 + QK_ROPE
    w_kpe = w_in[:, o2:o3]
    w_in_p = jnp.concatenate(
        [w_in[:, :o2], w_in[:, o3:], _rope_slot(w_kpe), _rope_slot(_rot_partner(w_kpe))],
        axis=1).astype(BF16)
    uq = w_uq.reshape(Q_LORA, N_HEADS, QK_HEAD)
    w_uq_p = _slots(w_uq, QK_HEAD).astype(BF16)
    uq_rot = jnp.pad(_rot_partner(uq[..., QK_NOPE:]), ((0, 0), (0, 0), (QK_NOPE, SLOT - QK_HEAD)))
    w_uq_r = uq_rot.reshape(Q_LORA, HEADS_W).astype(BF16)
    ukv = w_ukv.reshape(KV_LORA, N_HEADS, QK_NOPE + V_HEAD)
    w_uk_p = _slots(ukv[..., :QK_NOPE].reshape(KV_LORA, N_HEADS * QK_NOPE), QK_NOPE).astype(BF16)
    w_uv = ukv[..., QK_NOPE:].reshape(KV_LORA, ATTN_WIDTH).astype(BF16)
    w_ukt = jnp.transpose(ukv[..., :QK_NOPE], (2, 1, 0)).reshape(N_HEADS * QK_NOPE, KV_LORA).astype(BF16)
    pad_g = lambda g: jnp.pad(g, (0, SLOT - QK_HEAD))[None, :]
    gq = pad_g(q_norm_g) * (ATTN_SCALE * LOG2_E)
    gk = pad_g(k_norm_g)
    mixer = (norm_mix_g[None, :], w_in_p, q_lora_g[None, :], w_uq_p, w_uq_r, kv_lora_g[None, :],
             w_uk_p, w_uv, gq, gk, conv_w, conv_b[None, :], conv_out_g[None, :])
    return mixer, w_ukt, w_uv


def _rope_slot_tables(pos):
    inv_freq = ROPE_THETA ** (-(np.arange(0, QK_ROPE, 2, dtype=np.float64) / QK_ROPE))
    ang = pos.astype(np.float64)[:, None] * inv_freq[None, :]
    n = pos.shape[0]
    cos2 = np.concatenate([np.cos(ang)] * 2, axis=1)
    sin2 = np.concatenate([np.sin(ang)] * 2, axis=1)
    cos_t = np.concatenate([np.ones((n, QK_NOPE)), cos2, np.zeros((n, SLOT - QK_HEAD))], axis=1)
    sin_t = np.pad(sin2, ((0, 0), (QK_NOPE, SLOT - QK_HEAD)))
    return jnp.asarray(cos_t, F32), jnp.asarray(sin_t, F32)


def kernel(x_prompt, x_sample, cache_kv_latent, cache_k_rope, state_conv, page_table, meta_tokens,
           norm_mix_g, w_in, q_lora_g, kv_lora_g, w_uq, w_ukv, q_norm_g, k_norm_g, conv_w, conv_b,
           attn_out_g, conv_out_g, w_o, norm_ffn_g, w_up, w_down):
    depth = w_in.shape[0]
    assert depth == 1, "the prompt and sample streams are chained for a single layer"
    nb, seq, _ = x_prompt.shape
    nseq, dec, _ = x_sample.shape
    past = page_table.shape[1] * PAGE_SIZE
    l = 0

    mixer_w, w_ukt, w_uv = _layer_weights(w_in[l], q_lora_g[l], kv_lora_g[l], w_uq[l], w_ukv[l],
                                          q_norm_g[l], k_norm_g[l], conv_w[l], conv_b[l],
                                          conv_out_g[l], norm_mix_g[l])
    ffn_w = (attn_out_g[l][None, :], w_o[l].astype(BF16), norm_ffn_g[l][None, :],
             w_up[l].astype(BF16), w_down[l].astype(BF16))

    ns = nseq * dec
    xs_rows = jnp.concatenate([x_sample.reshape(ns, D_MODEL), meta_tokens.astype(F32)], axis=0)
    t_in_seq = np.concatenate([np.tile(np.arange(dec), nseq), np.arange(N_META)])
    pos_s = np.concatenate([np.tile(past + np.arange(dec), nseq), np.arange(N_META)])
    cos_s, sin_s = _rope_slot_tables(pos_s)
    m1 = jnp.asarray((t_in_seq >= 1)[:, None], F32)
    m2 = jnp.asarray((t_in_seq >= 2)[:, None], F32)
    st = state_conv[l].astype(F32)
    zrow = jnp.zeros((nseq, 1, CONV_CH), F32)
    s1 = jnp.concatenate([st[:, 1:2], zrow, zrow, zrow], axis=1)[:, :dec]
    s2 = jnp.concatenate([st[:, 0:1], st[:, 1:2], zrow, zrow], axis=1)[:, :dec]
    zmeta = jnp.zeros((N_META, CONV_CH), F32)
    s1 = jnp.concatenate([s1.reshape(ns, CONV_CH), zmeta], axis=0)
    s2 = jnp.concatenate([s2.reshape(ns, CONV_CH), zmeta], axis=0)
    (qk_s, qabs_s, k_s, v_s, lat_s, kpe_s, convn_s, u_s) = _mixer_call(
        xs_rows, cos_s, sin_s, (m1, m2, s1, s2), mixer_w, sample_mode=True)

    pad_meta = lambda a: jnp.pad(a[ns:], ((0, LANES - N_META), (0, 0)))
    k_meta, v_meta = pad_meta(k_s), pad_meta(v_s)
    lat_meta, kpe_meta, u_meta = lat_s[ns:], kpe_s[ns:], u_s[ns:]

    cos_p, sin_p = _rope_slot_tables(N_META + np.arange(seq))
    q_p, k_p, v_p, lat_p, kpe_p, convn_p, utail_p = _mixer_call(
        x_prompt, cos_p, sin_p, (u_meta[N_META - SUBLANES:], lat_meta), mixer_w, sample_mode=False)
    attn_p = _prompt_attn_call(q_p, k_p, v_p, k_meta, v_meta)
    y_prompt = _merge_ffn_call(x_prompt.reshape(nb * seq, D_MODEL), attn_p.reshape(nb * seq, ATTN_WIDTH),
                               convn_p.reshape(nb * seq, CONV_CH), ffn_w, ROW_TILE)
    y_prompt = y_prompt.reshape(nb, seq, D_MODEL)
    bcast = lambda a: jnp.broadcast_to(a[None], (nb,) + a.shape)
    new_lat_prompt = lat_p[None]
    new_kpe_prompt = jnp.concatenate([bcast(kpe_meta), kpe_p], axis=1)[None]
    new_conv_prompt = utail_p[:, -1, SUBLANES - (CONV_W - 1):][None]

    nq = dec * N_HEADS
    qabs = qabs_s[:ns].reshape(nseq, nq, KV_LORA)
    qpe = qk_s[:ns].reshape(ns, N_HEADS, SLOT)[:, :, QK_NOPE:QK_HEAD].reshape(nseq, nq, QK_ROPE)
    assert dec <= TAIL_KEYS
    pad_tail = lambda a: jnp.pad(a[:ns].reshape(nseq, dec, -1), ((0, 0), (0, TAIL_KEYS - dec), (0, 0)))
    attn_s = _sample_attn_call(page_table, qabs, qpe, w_ukt, w_uv, pad_tail(lat_s),
                               jnp.swapaxes(pad_tail(kpe_s), 1, 2),
                               cache_kv_latent[l], jnp.swapaxes(cache_k_rope[l], 1, 2))
    y_sample = _merge_ffn_call(x_sample.reshape(ns, D_MODEL), attn_s.reshape(ns, ATTN_WIDTH),
                               convn_s[:ns], ffn_w, ns)
    y_sample = y_sample.reshape(nseq, dec, D_MODEL)
    new_lat_sample = lat_s[:ns].reshape(nseq, dec, KV_LORA)[None]
    new_kpe_sample = kpe_s[:ns].reshape(nseq, dec, QK_ROPE)[None]
    us = jnp.concatenate([st, u_s[:ns].reshape(nseq, dec, CONV_CH)], axis=1)
    new_conv_sample = us[:, -(CONV_W - 1):][None]

    return (y_prompt, y_sample, new_lat_prompt, new_kpe_prompt, new_conv_prompt,
            new_lat_sample, new_kpe_sample, new_conv_sample)
```

```python
import functools

import jax
import jax.numpy as jnp
import numpy as np
from jax import lax
from jax.experimental import pallas as pl
from jax.experimental.pallas import tpu as pltpu

D_MODEL = 1024
N_META = 16
N_HEADS = 8
QK_NOPE = 64
QK_ROPE = 32
V_HEAD = 64
QK_HEAD = QK_NOPE + QK_ROPE
Q_LORA = 384
KV_LORA = 256
ATTN_WIDTH = N_HEADS * V_HEAD
CONV_CH = D_MODEL - ATTN_WIDTH
CONV_W = 3
D_FF = 4 * D_MODEL
ROPE_THETA = 10000.0
EPS = 1e-6
PAGE_SIZE = 128
ATTN_SCALE = QK_HEAD ** -0.5
LOG2_E = 1.4426950408889634

LANES = 128
SUBLANES = 8
SLOT = LANES
HEADS_W = N_HEADS * SLOT
HALF_ROPE = QK_ROPE // 2

C_CQ = 0
C_CKV = C_CQ + Q_LORA
C_GB = C_CKV + KV_LORA
C_GC = C_GB + CONV_CH
C_HC = C_GC + CONV_CH
C_KPE = C_HC + CONV_CH
C_KPR = C_KPE + SLOT
IN_W = C_KPR + SLOT

ROW_TILE = 512
ATT_TQ = 256
ATT_TK = 256
ATT_TK_WIDE = 2048
ATT_LOOKAHEAD = 4
FF_CHUNK = 1024
CHUNK_PAGES = 8
DMA_GROUP = 8
DMA_ISSUE_PAGES = 2
TAIL_KEYS = PAGE_SIZE
VMEM_LIMIT = 52 * 1024 * 1024

BF16 = jnp.bfloat16
F32 = jnp.float32

_NT = (((1,), (1,)), ((), ()))


def _dot(a, b):
    return jnp.dot(a, b, preferred_element_type=F32)


def _dot_nt(a, b):
    return lax.dot_general(a, b, _NT, preferred_element_type=F32)


def _rms(x, g):
    return x * lax.rsqrt(jnp.mean(x * x, axis=-1, keepdims=True) + EPS) * g


def _const_spec(shape):
    nd = len(shape)
    return pl.BlockSpec(shape, lambda *_: (0,) * nd, pipeline_mode=pl.Buffered(1))


def _mixer_kernel(*refs, rows, sample_mode):
    it = iter(refs)
    x_ref, cos_ref, sin_ref = next(it), next(it), next(it)
    if sample_mode:
        m1_ref, m2_ref, s1_ref, s2_ref = next(it), next(it), next(it), next(it)
    else:
        tail_in_ref, lat_meta_ref = next(it), next(it)
    (g_mix_ref, w_in_ref, g_ql_ref, w_uq_ref, w_uqr_ref, g_kvl_ref, w_uk_ref, w_uv_ref,
     gq_ref, gk_ref, cw_ref, cb_ref, g_conv_ref) = (next(it) for _ in range(13))
    if sample_mode:
        (qk_ref, qabs_ref, k_ref, v_ref, lat_ref, kpe_ref, convn_ref, u_ref) = (next(it) for _ in range(8))
    else:
        (q_ref, k_ref, v_ref, lat_hbm, kpe_ref, convn_ref, utail_ref) = (next(it) for _ in range(7))
    ubuf = next(it)

    if sample_mode:
        ubuf[0:SUBLANES, :] = jnp.zeros((SUBLANES, CONV_CH), F32)
    else:
        lat_stage, lat_sems = next(it), next(it)
        b = pl.program_id(0)
        t = pl.program_id(1)
        lat_copy = pltpu.make_async_copy(
            lat_stage, lat_hbm.at[b, pl.ds(N_META + t * rows, rows)], lat_sems.at[0])
        meta_copy = pltpu.make_async_copy(lat_meta_ref, lat_hbm.at[b, pl.ds(0, N_META)], lat_sems.at[1])
        first_step = jnp.logical_and(b == 0, t == 0)
        last_step = jnp.logical_and(b == pl.num_programs(0) - 1, t == pl.num_programs(1) - 1)

        @pl.when(jnp.logical_not(first_step))
        def _():
            lat_copy.wait()

        @pl.when(t == 0)
        def _():
            ubuf[0:SUBLANES, :] = tail_in_ref[...]
            meta_copy.start()

        @pl.when(t != 0)
        def _():
            ubuf[0:SUBLANES, :] = ubuf[rows:rows + SUBLANES, :]

    hn = _rms(x_ref[...], g_mix_ref[...]).astype(BF16)
    cos = cos_ref[...]
    sin = sin_ref[...]

    cq = _dot(hn, w_in_ref[:, C_CQ:C_CKV])
    ckv = _dot(hn, w_in_ref[:, C_CKV:C_GB])
    zk = _dot(hn, w_in_ref[:, C_KPE:IN_W])
    gc = _dot(hn, w_in_ref[:, C_GC:C_HC])
    hc = _dot(hn, w_in_ref[:, C_HC:C_KPE])
    gb = _dot(hn, w_in_ref[:, C_GB:C_GC])

    cqn = _rms(cq, g_ql_ref[...]).astype(BF16)
    gq = gq_ref[...]
    gk = gk_ref[...]
    for h in range(N_HEADS):
        sl = slice(h * SLOT, (h + 1) * SLOT)
        if h % 2 == 0:
            sl2 = slice(h * SLOT, (h + 2) * SLOT)
            q2 = _dot(cqn, w_uq_ref[:, sl2])
            qr2 = _dot(cqn, w_uqr_ref[:, sl2])
        own = slice((h % 2) * SLOT, (h % 2 + 1) * SLOT)
        qh = q2[:, own] * cos + qr2[:, own] * sin
        ss = jnp.sum(qh * qh, axis=-1, keepdims=True)
        qh = qh * lax.rsqrt(ss * (1.0 / QK_HEAD) + EPS) * gq
        if sample_mode:
            qkh = (qh * gk).astype(BF16)
            qk_ref[:, sl] = qkh
            qabs_ref[:, h * KV_LORA:(h + 1) * KV_LORA] = _dot_nt(qkh, w_uk_ref[:, sl]).astype(BF16)
        else:
            q_ref[:, sl] = qh.astype(BF16)

    lat = _rms(ckv, g_kvl_ref[...])
    if sample_mode:
        lat_ref[...] = lat
    else:
        lat_stage[...] = lat
    lat_b = lat.astype(BF16)
    krot = zk[:, :SLOT] * cos + zk[:, SLOT:] * sin
    kpe_ref[...] = krot[:, QK_NOPE:QK_HEAD]
    ss_rot = jnp.sum(krot * krot, axis=-1, keepdims=True)
    for h in range(N_HEADS):
        sl = slice(h * SLOT, (h + 1) * SLOT)
        if h % 2 == 0:
            kn2 = _dot(lat_b, w_uk_ref[:, h * SLOT:(h + 2) * SLOT])
        kn = kn2[:, (h % 2) * SLOT:(h % 2 + 1) * SLOT]
        ss = jnp.sum(kn * kn, axis=-1, keepdims=True) + ss_rot
        kh = (kn + krot) * lax.rsqrt(ss * (1.0 / QK_HEAD) + EPS) * gk
        k_ref[:, sl] = kh.astype(BF16)
    v_ref[...] = _dot(lat_b, w_uv_ref[...]).astype(BF16)

    u = gc * hc
    ubuf[SUBLANES:SUBLANES + rows, :] = u
    u1 = ubuf[SUBLANES - 1:SUBLANES - 1 + rows, :]
    u2 = ubuf[SUBLANES - 2:SUBLANES - 2 + rows, :]
    if sample_mode:
        u1 = u1 * m1_ref[...] + s1_ref[...]
        u2 = u2 * m2_ref[...] + s2_ref[...]
        u_ref[...] = u
    else:
        utail_ref[...] = u[rows - SUBLANES:, :]
    y = cb_ref[...] + u2 * cw_ref[0:1, :] + u1 * cw_ref[1:2, :] + u * cw_ref[2:3, :]
    convn_ref[...] = _rms(gb * y, g_conv_ref[...]).astype(BF16)
    if not sample_mode:
        lat_copy.start()

        @pl.when(t == 0)
        def _():
            meta_copy.wait()

        @pl.when(last_step)
        def _():
            lat_copy.wait()


def _mixer_call(x, cos_t, sin_t, conv_in, weights, *, sample_mode):
    wspecs = [_const_spec(w.shape) for w in weights]
    if sample_mode:
        rows = x.shape[0]
        grid = (1,)
        row = lambda w: pl.BlockSpec((rows, w), lambda i: (0, 0))
        m1, m2, s1, s2 = conv_in
        in_specs = [row(D_MODEL), row(SLOT), row(SLOT), row(1), row(1), row(CONV_CH), row(CONV_CH)]
        args = [x, cos_t, sin_t, m1, m2, s1, s2]
        widths = [(HEADS_W, BF16), (N_HEADS * KV_LORA, BF16), (HEADS_W, BF16), (ATTN_WIDTH, BF16),
                  (KV_LORA, F32), (QK_ROPE, F32), (CONV_CH, BF16), (CONV_CH, F32)]
        out_shape = [jax.ShapeDtypeStruct((rows, w), d) for w, d in widths]
        out_specs = [row(w) for w, _ in widths]
        sem = ("arbitrary",)
        scratch = []
    else:
        nb, seq, _ = x.shape
        rows = ROW_TILE
        nt = seq // rows
        grid = (nb, nt)
        row3 = lambda w: pl.BlockSpec((None, rows, w), lambda b, t: (b, t, 0))
        tab = pl.BlockSpec((rows, SLOT), lambda b, t: (t, 0))
        u_tail, lat_meta = conv_in
        in_specs = [row3(D_MODEL), tab, tab, _const_spec(u_tail.shape), _const_spec(lat_meta.shape)]
        args = [x, cos_t, sin_t, u_tail, lat_meta]
        widths = [(HEADS_W, BF16), (HEADS_W, BF16), (ATTN_WIDTH, BF16), (KV_LORA, F32),
                  (QK_ROPE, F32), (CONV_CH, BF16)]
        out_shape = [jax.ShapeDtypeStruct((nb, seq, w), d) for w, d in widths]
        out_shape.append(jax.ShapeDtypeStruct((nb, nt, SUBLANES, CONV_CH), F32))
        out_specs = [row3(w) for w, _ in widths]
        out_specs.append(pl.BlockSpec((None, None, SUBLANES, CONV_CH), lambda b, t: (b, t, 0, 0)))
        out_shape[3] = jax.ShapeDtypeStruct((nb, N_META + seq, KV_LORA), F32)
        out_specs[3] = pl.BlockSpec(memory_space=pl.ANY)
        sem = ("arbitrary", "arbitrary")
        scratch = [pltpu.VMEM((rows, KV_LORA), F32), pltpu.SemaphoreType.DMA((2,))]
    return pl.pallas_call(
        functools.partial(_mixer_kernel, rows=rows, sample_mode=sample_mode),
        grid=grid,
        in_specs=in_specs + wspecs,
        out_specs=out_specs,
        out_shape=out_shape,
        scratch_shapes=[pltpu.VMEM((rows + 2 * SUBLANES, CONV_CH), F32)] + scratch,
        compiler_params=pltpu.CompilerParams(dimension_semantics=sem, vmem_limit_bytes=VMEM_LIMIT),
        name="mixer_sample" if sample_mode else "mixer_prompt",
    )(*args, *weights)


def _softmax_step(s, state, pv):
    m_old, l_old, acc_old = state
    n = s.shape[1] // LANES
    m_new = jnp.maximum(m_old, jnp.max(s, axis=-1, keepdims=True))
    alpha = jnp.exp2(m_old - m_new)
    ps = [jnp.exp2(s[:, c * LANES:(c + 1) * LANES] - m_new) for c in range(n)]
    l_new = alpha * l_old + functools.reduce(lambda a, b: a + b, ps)
    p = ps[0] if n == 1 else jnp.concatenate(ps, axis=1)
    w = acc_old.shape[-1] // LANES
    alpha_w = alpha if w == 1 else jnp.concatenate([alpha] * w, axis=1)
    return m_new, l_new, alpha_w * acc_old + pv(p.astype(BF16))


def _softmax_update(s, m_ref, l_ref, acc_ref, pv):
    m_ref[...], l_ref[...], acc_ref[...] = _softmax_step(s, (m_ref[...], l_ref[...], acc_ref[...]), pv)


def _prompt_attn_kernel(q_ref, k_ref, v_ref, km_ref, vm_ref, o_ref, m_sc, l_sc, acc_sc):
    seq = q_ref.shape[0]
    half = ATT_TQ // 2
    lane = lax.broadcasted_iota(jnp.int32, (ATT_TQ, SLOT), 1)
    m_sc[...] = jnp.full(m_sc.shape, -jnp.inf, F32)
    l_sc[...] = jnp.zeros(l_sc.shape, F32)
    acc_sc[...] = jnp.zeros(acc_sc.shape, F32)

    meta = lane < N_META
    col_h = lax.broadcasted_iota(jnp.int32, (half, half), 1)
    row_h = lax.broadcasted_iota(jnp.int32, (half, half), 0)
    col_f = lax.broadcasted_iota(jnp.int32, (half, ATT_TK), 1)
    row_f = lax.broadcasted_iota(jnp.int32, (half, ATT_TK), 0)

    steps = []
    for qi in range(seq // ATT_TQ):
        for hh in range(2):
            steps.append((qi, hh, 0, ATT_TQ, None, meta))
        d0 = qi * ATT_TK
        for k0 in range(0, d0, ATT_TK_WIDE):
            for hh in range(2):
                steps.append((qi, hh, 0, ATT_TQ, slice(k0, min(k0 + ATT_TK_WIDE, d0)), None))
        for hh in range(2):
            steps.append((qi, hh, 0, half, slice(d0, d0 + half), col_h <= row_h))
            steps.append((qi, hh, half, half, slice(d0, d0 + ATT_TK), col_f <= row_f + half))
        steps.append((qi,))

    def scores(qi, hh, r0, nr, ks, mask):
        sl = slice(hh * SLOT, (hh + 1) * SLOT)
        k_blk = km_ref[:, sl] if ks is None else k_ref[ks, sl]
        s = _dot_nt(q_ref[pl.ds(qi * ATT_TQ + r0, nr), sl], k_blk)
        return s if mask is None else jnp.where(mask, s, -jnp.inf)

    def update(s, qi, hh, r0, nr, ks, mask):
        rows = pl.ds(r0, nr)
        v_blk = vm_ref[...] if ks is None else v_ref[ks, :]
        _softmax_update(s, m_sc.at[qi, hh, rows], l_sc.at[qi, hh, rows], acc_sc.at[qi, hh, rows],
                        lambda p: _dot(p, v_blk))

    def finalize(qi):
        outs = [acc_sc[qi, hh] / jnp.sum(l_sc[qi, hh], axis=-1, keepdims=True) for hh in range(2)]
        o_ref[qi * ATT_TQ:(qi + 1) * ATT_TQ, :] = (
            jnp.where(lane < V_HEAD, outs[0], outs[1]).astype(o_ref.dtype))

    pending = []
    for st in steps:
        pending.append((st, scores(*st) if len(st) > 1 else None))
        if len(pending) > ATT_LOOKAHEAD:
            d, s = pending.pop(0)
            finalize(*d) if s is None else update(s, *d)
    for d, s in pending:
        finalize(*d) if s is None else update(s, *d)


def _prompt_attn_call(q, k, v, k_meta, v_meta):
    nb, seq, _ = q.shape
    assert ATT_TQ == ATT_TK and seq % ATT_TQ == 0
    grid = (nb, N_HEADS // 2)
    return pl.pallas_call(
        _prompt_attn_kernel,
        grid=grid,
        in_specs=[
            pl.BlockSpec((None, seq, 2 * SLOT), lambda b, p: (b, 0, p)),
            pl.BlockSpec((None, seq, 2 * SLOT), lambda b, p: (b, 0, p)),
            pl.BlockSpec((None, seq, SLOT), lambda b, p: (b, 0, p)),
            pl.BlockSpec((LANES, 2 * SLOT), lambda b, p: (0, p)),
            pl.BlockSpec((LANES, SLOT), lambda b, p: (0, p)),
        ],
        out_specs=pl.BlockSpec((None, seq, SLOT), lambda b, p: (b, 0, p)),
        out_shape=jax.ShapeDtypeStruct((nb, seq, ATTN_WIDTH), BF16),
        scratch_shapes=[pltpu.VMEM((seq // ATT_TQ, 2, ATT_TQ, LANES), F32)] * 3,
        compiler_params=pltpu.CompilerParams(
            dimension_semantics=("arbitrary", "arbitrary"),
            vmem_limit_bytes=VMEM_LIMIT),
        name="prompt_attn",
    )(q, k, v, k_meta, v_meta)


def _sample_attn_kernel(pt_ref, qabs_ref, qpe_ref, wukt_ref, wuv_ref, tlat_ref, tkpe_ref,
                        lat_hbm, kpe_hbm, o_ref,
                        latb_sc, lhs_sc, lat_buf, kpe_buf, sems):
    n_pages = lat_buf.shape[1]
    chunk_pages = (CHUNK_PAGES,) * (n_pages // CHUNK_PAGES)
    b = pl.program_id(0)
    last = pl.num_programs(0) - 1
    slot = b % 2
    nq = qabs_ref.shape[0]
    reps = nq // N_HEADS
    nkn = N_HEADS * QK_NOPE
    qpe = qpe_ref[...]

    ngroup = lat_buf.shape[1] // DMA_GROUP

    def group_copies(bb, sl, gg):
        copies = []
        for j in range(gg * DMA_GROUP, (gg + 1) * DMA_GROUP):
            page = pt_ref[bb, j]
            copies.append(pltpu.make_async_copy(lat_hbm.at[page], lat_buf.at[sl, j], sems.at[0, sl]))
            copies.append(pltpu.make_async_copy(kpe_hbm.at[page], kpe_buf.at[sl, j], sems.at[1, sl]))
        return copies

    @pl.when(b == 0)
    def _():
        lhs_sc[0:nkn, :] = wukt_ref[...]
        for gg in range(ngroup):
            for cp in group_copies(0, 0, gg):
                cp.start()

    nxt = jnp.minimum(b + 1, last)

    lhs_sc[nkn:nkn + nq, :] = qabs_ref[...]
    state = (jnp.full((nq, LANES), -jnp.inf, F32), jnp.zeros((nq, LANES), F32),
             jnp.zeros((nq, KV_LORA), F32))

    def scores(lat_b, kpe_t):
        nk = lat_b.shape[0]
        full = _dot_nt(lhs_sc[...], lat_b)
        knt = full[:nkn]
        nsum = jnp.sum((knt * knt).reshape(QK_NOPE, N_HEADS, nk), axis=0)
        rsum = jnp.sum(kpe_t * kpe_t, axis=0, keepdims=True)
        r = lax.rsqrt((nsum + rsum) * (1.0 / QK_HEAD) + EPS)
        s = full[nkn:] + _dot(qpe, kpe_t.astype(BF16))
        return s * jnp.concatenate([r] * reps, axis=0)

    for gg in range(ngroup):
        for cp in group_copies(b, slot, gg):
            cp.wait()

    tail_b = tlat_ref[...].astype(BF16)
    s = scores(tail_b, tkpe_ref[...])
    key = lax.broadcasted_iota(jnp.int32, s.shape, 1)
    qt = lax.broadcasted_iota(jnp.int32, s.shape, 0) // N_HEADS
    s_tail = jnp.where(key <= qt, s, -jnp.inf)

    def chunk_scores(p0, npg):
        parts = [s_tail] if p0 == 0 else []
        for j in range(p0, p0 + npg, 2):
            for jj in (j, j + 1):
                latb_sc[jj * PAGE_SIZE:(jj + 1) * PAGE_SIZE, :] = lat_buf[slot, jj].astype(BF16)
            kpe_t = jnp.concatenate([kpe_buf[slot, j], kpe_buf[slot, j + 1]], axis=1)
            parts.append(scores(latb_sc[j * PAGE_SIZE:(j + 2) * PAGE_SIZE, :], kpe_t))
            if (j + 2) % DMA_ISSUE_PAGES == 0 and (j + 2) // DMA_ISSUE_PAGES <= ngroup:
                for cp in group_copies(nxt, 1 - slot, (j + 2) // DMA_ISSUE_PAGES - 1):
                    cp.start()
        return jnp.concatenate(parts, axis=1)

    def chunk_update(state, s, p0, npg):
        rows = slice(p0 * PAGE_SIZE, (p0 + npg) * PAGE_SIZE)
        if p0 == 0:
            pv = lambda p: _dot(p[:, :TAIL_KEYS], tail_b) + _dot(p[:, TAIL_KEYS:], latb_sc[rows, :])
        else:
            pv = lambda p: _dot(p, latb_sc[rows, :])
        return _softmax_step(s, state, pv)

    starts = [sum(chunk_pages[:i]) for i in range(len(chunk_pages))]
    pending = None
    for p0, npg in zip(starts, chunk_pages):
        s = chunk_scores(p0, npg)
        if pending is not None:
            state = chunk_update(state, *pending)
        pending = (s, p0, npg)
    state = chunk_update(state, *pending)

    _, l_fin, acc_fin = state
    o_lat = (acc_fin / jnp.sum(l_fin, axis=-1, keepdims=True)).astype(BF16)
    full = _dot(o_lat, wuv_ref[...])
    row_h = lax.broadcasted_iota(jnp.int32, full.shape, 0) % N_HEADS
    col_h = lax.broadcasted_iota(jnp.int32, full.shape, 1) // V_HEAD
    own = jnp.where(row_h == col_h, full, 0.0)
    o_ref[...] = jnp.sum(own.reshape(reps, N_HEADS, ATTN_WIDTH), axis=1)

    @pl.when(b == last)
    def _():
        for gg in range(ngroup):
            for cp in group_copies(last, 1 - slot, gg):
                cp.wait()


def _sample_attn_call(page_table, qabs, qpe, w_ukt, w_uv, tail_lat, tail_kpe_t, cache_lat, cache_kpe_t):
    nseq, n_pages = page_table.shape
    nq = qabs.shape[1]
    assert n_pages % DMA_GROUP == 0 and DMA_GROUP % 2 == 0 and DMA_ISSUE_PAGES % 2 == 0
    assert DMA_ISSUE_PAGES <= DMA_GROUP
    assert n_pages % CHUNK_PAGES == 0 and CHUNK_PAGES % 2 == 0
    grid = (nseq,)

    per_seq = lambda rows, width: pl.BlockSpec((None, rows, width), lambda b, pt: (b, 0, 0))
    whole = lambda shape: pl.BlockSpec(shape, lambda b, pt: (0,) * len(shape),
                                       pipeline_mode=pl.Buffered(1))
    in_specs = [per_seq(nq, KV_LORA), per_seq(nq, QK_ROPE), whole(w_ukt.shape), whole(w_uv.shape),
                per_seq(TAIL_KEYS, KV_LORA), per_seq(QK_ROPE, TAIL_KEYS),
                pl.BlockSpec(memory_space=pl.ANY), pl.BlockSpec(memory_space=pl.ANY)]
    reps = nq // N_HEADS
    return pl.pallas_call(
        _sample_attn_kernel,
        grid_spec=pltpu.PrefetchScalarGridSpec(
            num_scalar_prefetch=1,
            grid=grid,
            in_specs=in_specs,
            out_specs=pl.BlockSpec((None, reps, ATTN_WIDTH), lambda b, pt: (b, 0, 0)),
            scratch_shapes=[pltpu.VMEM((n_pages * PAGE_SIZE, KV_LORA), BF16),
                            pltpu.VMEM((N_HEADS * QK_NOPE + nq, KV_LORA), BF16),
                            pltpu.VMEM((2, n_pages, PAGE_SIZE, KV_LORA), F32),
                            pltpu.VMEM((2, n_pages, QK_ROPE, PAGE_SIZE), F32),
                            pltpu.SemaphoreType.DMA((2, 2))],
        ),
        out_shape=jax.ShapeDtypeStruct((nseq, reps, ATTN_WIDTH), F32),
        compiler_params=pltpu.CompilerParams(dimension_semantics=("arbitrary",),
                                             vmem_limit_bytes=VMEM_LIMIT),
        name="sample_attn",
    )(page_table, qabs, qpe, w_ukt, w_uv, tail_lat, tail_kpe_t, cache_lat, cache_kpe_t)


def _merge_ffn_kernel(x_ref, attn_ref, convn_ref, g_attn_ref, w_o_ref, g_ffn_ref, w_up_ref,
                      w_down_ref, y_ref):
    an = _rms(attn_ref[...].astype(F32), g_attn_ref[...]).astype(BF16)
    x1 = x_ref[...] + (_dot(an, w_o_ref[0:ATTN_WIDTH, :]) + _dot(convn_ref[...], w_o_ref[ATTN_WIDTH:, :]))
    hf = _rms(x1, g_ffn_ref[...]).astype(BF16)
    ffn = None
    for c in range(D_FF // FF_CHUNK):
        cs = slice(c * FF_CHUNK, (c + 1) * FF_CHUNK)
        up = jnp.maximum(_dot(hf, w_up_ref[:, cs]), 0.0)
        part = _dot((up * up).astype(BF16), w_down_ref[cs, :])
        ffn = part if ffn is None else ffn + part
    y_ref[...] = x1 + ffn


def _merge_ffn_call(x, attn, convn, weights, rows):
    n = x.shape[0]
    row = lambda w: pl.BlockSpec((rows, w), lambda i: (i, 0))
    return pl.pallas_call(
        _merge_ffn_kernel,
        grid=(n // rows,),
        in_specs=[row(D_MODEL), row(ATTN_WIDTH), row(CONV_CH)] + [_const_spec(w.shape) for w in weights],
        out_specs=row(D_MODEL),
        out_shape=jax.ShapeDtypeStruct((n, D_MODEL), F32),
        compiler_params=pltpu.CompilerParams(dimension_semantics=("arbitrary",),
                                             vmem_limit_bytes=VMEM_LIMIT),
        name="merge_ffn",
    )(x, attn, convn, *weights)


def _slots(w, width):
    k = w.shape[0]
    w = w.reshape(k, N_HEADS, width)
    return jnp.pad(w, ((0, 0), (0, 0), (0, SLOT - width))).reshape(k, HEADS_W)


def _rot_partner(w):
    return jnp.concatenate([-w[..., HALF_ROPE:], w[..., :HALF_ROPE]], axis=-1)


def _rope_slot(w):
    return jnp.pad(w, ((0, 0), (QK_NOPE, SLOT - QK_HEAD)))


def _layer_weights(w_in, q_lora_g, kv_lora_g, w_uq, w_ukv, q_norm_g, k_norm_g, conv_w, conv_b,
                   conv_out_g, norm_mix_g):
    o1 = Q_LORA
    o2 = o1 + KV_LORA
    o3 = o2 + QK_ROPE
    w_kpe = w_in[:, o2:o3]
    w_in_p = jnp.concatenate(
        [w_in[:, :o2], w_in[:, o3:], _rope_slot(w_kpe), _rope_slot(_rot_partner(w_kpe))],
        axis=1).astype(BF16)
    uq = w_uq.reshape(Q_LORA, N_HEADS, QK_HEAD)
    w_uq_p = _slots(w_uq, QK_HEAD).astype(BF16)
    uq_rot = jnp.pad(_rot_partner(uq[..., QK_NOPE:]), ((0, 0), (0, 0), (QK_NOPE, SLOT - QK_HEAD)))
    w_uq_r = uq_rot.reshape(Q_LORA, HEADS_W).astype(BF16)
    ukv = w_ukv.reshape(KV_LORA, N_HEADS, QK_NOPE + V_HEAD)
    w_uk_p = _slots(ukv[..., :QK_NOPE].reshape(KV_LORA, N_HEADS * QK_NOPE), QK_NOPE).astype(BF16)
    w_uv = ukv[..., QK_NOPE:].reshape(KV_LORA, ATTN_WIDTH).astype(BF16)
    w_ukt = jnp.transpose(ukv[..., :QK_NOPE], (2, 1, 0)).reshape(N_HEADS * QK_NOPE, KV_LORA).astype(BF16)
    pad_g = lambda g: jnp.pad(g, (0, SLOT - QK_HEAD))[None, :]
    gq = pad_g(q_norm_g) * (ATTN_SCALE * LOG2_E)
    gk = pad_g(k_norm_g)
    mixer = (norm_mix_g[None, :], w_in_p, q_lora_g[None, :], w_uq_p, w_uq_r, kv_lora_g[None, :],
             w_uk_p, w_uv, gq, gk, conv_w, conv_b[None, :], conv_out_g[None, :])
    return mixer, w_ukt, w_uv


def _rope_slot_tables(pos):
    inv_freq = ROPE_THETA ** (-(np.arange(0, QK_ROPE, 2, dtype=np.float64) / QK_ROPE))
    ang = pos.astype(np.float64)[:, None] * inv_freq[None, :]
    n = pos.shape[0]
    cos2 = np.concatenate([np.cos(ang)] * 2, axis=1)
    sin2 = np.concatenate([np.sin(ang)] * 2, axis=1)
    cos_t = np.concatenate([np.ones((n, QK_NOPE)), cos2, np.zeros((n, SLOT - QK_HEAD))], axis=1)
    sin_t = np.pad(sin2, ((0, 0), (QK_NOPE, SLOT - QK_HEAD)))
    return jnp.asarray(cos_t, F32), jnp.asarray(sin_t, F32)


def kernel(x_prompt, x_sample, cache_kv_latent, cache_k_rope, state_conv, page_table, meta_tokens,
           norm_mix_g, w_in, q_lora_g, kv_lora_g, w_uq, w_ukv, q_norm_g, k_norm_g, conv_w, conv_b,
           attn_out_g, conv_out_g, w_o, norm_ffn_g, w_up, w_down):
    depth = w_in.shape[0]
    assert depth == 1, "the prompt and sample streams are chained for a single layer"
    nb, seq, _ = x_prompt.shape
    nseq, dec, _ = x_sample.shape
    past = page_table.shape[1] * PAGE_SIZE
    l = 0

    mixer_w, w_ukt, w_uv = _layer_weights(w_in[l], q_lora_g[l], kv_lora_g[l], w_uq[l], w_ukv[l],
                                          q_norm_g[l], k_norm_g[l], conv_w[l], conv_b[l],
                                          conv_out_g[l], norm_mix_g[l])
    ffn_w = (attn_out_g[l][None, :], w_o[l].astype(BF16), norm_ffn_g[l][None, :],
             w_up[l].astype(BF16), w_down[l].astype(BF16))

    ns = nseq * dec
    xs_rows = jnp.concatenate([x_sample.reshape(ns, D_MODEL), meta_tokens.astype(F32)], axis=0)
    t_in_seq = np.concatenate([np.tile(np.arange(dec), nseq), np.arange(N_META)])
    pos_s = np.concatenate([np.tile(past + np.arange(dec), nseq), np.arange(N_META)])
    cos_s, sin_s = _rope_slot_tables(pos_s)
    m1 = jnp.asarray((t_in_seq >= 1)[:, None], F32)
    m2 = jnp.asarray((t_in_seq >= 2)[:, None], F32)
    st = state_conv[l].astype(F32)
    zrow = jnp.zeros((nseq, 1, CONV_CH), F32)
    s1 = jnp.concatenate([st[:, 1:2], zrow, zrow, zrow], axis=1)[:, :dec]
    s2 = jnp.concatenate([st[:, 0:1], st[:, 1:2], zrow, zrow], axis=1)[:, :dec]
    zmeta = jnp.zeros((N_META, CONV_CH), F32)
    s1 = jnp.concatenate([s1.reshape(ns, CONV_CH), zmeta], axis=0)
    s2 = jnp.concatenate([s2.reshape(ns, CONV_CH), zmeta], axis=0)
    (qk_s, qabs_s, k_s, v_s, lat_s, kpe_s, convn_s, u_s) = _mixer_call(
        xs_rows, cos_s, sin_s, (m1, m2, s1, s2), mixer_w, sample_mode=True)

    pad_meta = lambda a: jnp.pad(a[ns:], ((0, LANES - N_META), (0, 0)))
    k_meta, v_meta = pad_meta(k_s), pad_meta(v_s)
    lat_meta, kpe_meta, u_meta = lat_s[ns:], kpe_s[ns:], u_s[ns:]

    cos_p, sin_p = _rope_slot_tables(N_META + np.arange(seq))
    q_p, k_p, v_p, lat_p, kpe_p, convn_p, utail_p = _mixer_call(
        x_prompt, cos_p, sin_p, (u_meta[N_META - SUBLANES:], lat_meta), mixer_w, sample_mode=False)
    attn_p = _prompt_attn_call(q_p, k_p, v_p, k_meta, v_meta)
    y_prompt = _merge_ffn_call(x_prompt.reshape(nb * seq, D_MODEL), attn_p.reshape(nb * seq, ATTN_WIDTH),
                               convn_p.reshape(nb * seq, CONV_CH), ffn_w, ROW_TILE)
    y_prompt = y_prompt.reshape(nb, seq, D_MODEL)
    bcast = lambda a: jnp.broadcast_to(a[None], (nb,) + a.shape)
    new_lat_prompt = lat_p[None]
    new_kpe_prompt = jnp.concatenate([bcast(kpe_meta), kpe_p], axis=1)[None]
    new_conv_prompt = utail_p[:, -1, SUBLANES - (CONV_W - 1):][None]

    nq = dec * N_HEADS
    qabs = qabs_s[:ns].reshape(nseq, nq, KV_LORA)
    qpe = qk_s[:ns].reshape(ns, N_HEADS, SLOT)[:, :, QK_NOPE:QK_HEAD].reshape(nseq, nq, QK_ROPE)
    assert dec <= TAIL_KEYS
    pad_tail = lambda a: jnp.pad(a[:ns].reshape(nseq, dec, -1), ((0, 0), (0, TAIL_KEYS - dec), (0, 0)))
    attn_s = _sample_attn_call(page_table, qabs, qpe, w_ukt, w_uv, pad_tail(lat_s),
                               jnp.swapaxes(pad_tail(kpe_s), 1, 2),
                               cache_kv_latent[l], jnp.swapaxes(cache_k_rope[l], 1, 2))
    y_sample = _merge_ffn_call(x_sample.reshape(ns, D_MODEL), attn_s.reshape(ns, ATTN_WIDTH),
                               convn_s[:ns], ffn_w, ns)
    y_sample = y_sample.reshape(nseq, dec, D_MODEL)
    new_lat_sample = lat_s[:ns].reshape(nseq, dec, KV_LORA)[None]
    new_kpe_sample = kpe_s[:ns].reshape(nseq, dec, QK_ROPE)[None]
    us = jnp.concatenate([st, u_s[:ns].reshape(nseq, dec, CONV_CH)], axis=1)
    new_conv_sample = us[:, -(CONV_W - 1):][None]

    return (y_prompt, y_sample, new_lat_prompt, new_kpe_prompt, new_conv_prompt,
            new_lat_sample, new_kpe_sample, new_conv_sample)
```

```python
import functools

import jax
import jax.numpy as jnp
import numpy as np
from jax import lax
from jax.experimental import pallas as pl
from jax.experimental.pallas import tpu as pltpu

D_MODEL = 1024
N_META = 16
N_HEADS = 8
QK_NOPE = 64
QK_ROPE = 32
V_HEAD = 64
QK_HEAD = QK_NOPE + QK_ROPE
Q_LORA = 384
KV_LORA = 256
ATTN_WIDTH = N_HEADS * V_HEAD
CONV_CH = D_MODEL - ATTN_WIDTH
CONV_W = 3
D_FF = 4 * D_MODEL
ROPE_THETA = 10000.0
EPS = 1e-6
PAGE_SIZE = 128
ATTN_SCALE = QK_HEAD ** -0.5
LOG2_E = 1.4426950408889634

LANES = 128
SUBLANES = 8
SLOT = LANES
HEADS_W = N_HEADS * SLOT
HALF_ROPE = QK_ROPE // 2

C_CQ = 0
C_CKV = C_CQ + Q_LORA
C_GB = C_CKV + KV_LORA
C_GC = C_GB + CONV_CH
C_HC = C_GC + CONV_CH
C_KPE = C_HC + CONV_CH
C_KPR = C_KPE + SLOT
IN_W = C_KPR + SLOT

ROW_TILE = 512
ATT_TQ = 256
ATT_TK = 256
ATT_TK_WIDE = 2048
ATT_LOOKAHEAD = 4
FF_CHUNK = 1024
CHUNK_PAGES = 8
DMA_GROUP = 8
DMA_ISSUE_PAGES = 2
TAIL_KEYS = PAGE_SIZE
VMEM_LIMIT = 52 * 1024 * 1024

BF16 = jnp.bfloat16
F32 = jnp.float32

_NT = (((1,), (1,)), ((), ()))


def _dot(a, b):
    return jnp.dot(a, b, preferred_element_type=F32)


def _dot_nt(a, b):
    return lax.dot_general(a, b, _NT, preferred_element_type=F32)


def _rms(x, g):
    return x * lax.rsqrt(jnp.mean(x * x, axis=-1, keepdims=True) + EPS) * g


def _const_spec(shape):
    nd = len(shape)
    return pl.BlockSpec(shape, lambda *_: (0,) * nd, pipeline_mode=pl.Buffered(1))


def _mixer_kernel(*refs, rows, sample_mode):
    it = iter(refs)
    x_ref, cos_ref, sin_ref = next(it), next(it), next(it)
    if sample_mode:
        m1_ref, m2_ref, s1_ref, s2_ref = next(it), next(it), next(it), next(it)
    else:
        tail_in_ref, lat_meta_ref = next(it), next(it)
    (g_mix_ref, w_in_ref, g_ql_ref, w_uq_ref, w_uqr_ref, g_kvl_ref, w_uk_ref, w_uv_ref,
     gq_ref, gk_ref, cw_ref, cb_ref, g_conv_ref) = (next(it) for _ in range(13))
    if sample_mode:
        (qk_ref, qabs_ref, k_ref, v_ref, lat_ref, kpe_ref, convn_ref, u_ref) = (next(it) for _ in range(8))
    else:
        (q_ref, k_ref, v_ref, lat_hbm, kpe_ref, convn_ref, utail_ref) = (next(it) for _ in range(7))
    ubuf = next(it)

    if sample_mode:
        ubuf[0:SUBLANES, :] = jnp.zeros((SUBLANES, CONV_CH), F32)
    else:
        lat_stage, lat_sems = next(it), next(it)
        b = pl.program_id(0)
        t = pl.program_id(1)
        lat_copy = pltpu.make_async_copy(
            lat_stage, lat_hbm.at[b, pl.ds(N_META + t * rows, rows)], lat_sems.at[0])
        meta_copy = pltpu.make_async_copy(lat_meta_ref, lat_hbm.at[b, pl.ds(0, N_META)], lat_sems.at[1])
        first_step = jnp.logical_and(b == 0, t == 0)
        last_step = jnp.logical_and(b == pl.num_programs(0) - 1, t == pl.num_programs(1) - 1)

        @pl.when(jnp.logical_not(first_step))
        def _():
            lat_copy.wait()

        @pl.when(t == 0)
        def _():
            ubuf[0:SUBLANES, :] = tail_in_ref[...]
            meta_copy.start()

        @pl.when(t != 0)
        def _():
            ubuf[0:SUBLANES, :] = ubuf[rows:rows + SUBLANES, :]

    hn = _rms(x_ref[...], g_mix_ref[...]).astype(BF16)
    cos = cos_ref[...]
    sin = sin_ref[...]

    cq = _dot(hn, w_in_ref[:, C_CQ:C_CKV])
    ckv = _dot(hn, w_in_ref[:, C_CKV:C_GB])
    zk = _dot(hn, w_in_ref[:, C_KPE:IN_W])
    gc = _dot(hn, w_in_ref[:, C_GC:C_HC])
    hc = _dot(hn, w_in_ref[:, C_HC:C_KPE])
    gb = _dot(hn, w_in_ref[:, C_GB:C_GC])

    cqn = _rms(cq, g_ql_ref[...]).astype(BF16)
    gq = gq_ref[...]
    gk = gk_ref[...]
    for h in range(N_HEADS):
        sl = slice(h * SLOT, (h + 1) * SLOT)
        if h % 2 == 0:
            sl2 = slice(h * SLOT, (h + 2) * SLOT)
            q2 = _dot(cqn, w_uq_ref[:, sl2])
            qr2 = _dot(cqn, w_uqr_ref[:, sl2])
        own = slice((h % 2) * SLOT, (h % 2 + 1) * SLOT)
        qh = q2[:, own] * cos + qr2[:, own] * sin
        ss = jnp.sum(qh * qh, axis=-1, keepdims=True)
        qh = qh * lax.rsqrt(ss * (1.0 / QK_HEAD) + EPS) * gq
        if sample_mode:
            qkh = (qh * gk).astype(BF16)
            qk_ref[:, sl] = qkh
            qabs_ref[:, h * KV_LORA:(h + 1) * KV_LORA] = _dot_nt(qkh, w_uk_ref[:, sl]).astype(BF16)
        else:
            q_ref[:, sl] = qh.astype(BF16)

    lat = _rms(ckv, g_kvl_ref[...])
    if sample_mode:
        lat_ref[...] = lat
    else:
        lat_stage[...] = lat
    lat_b = lat.astype(BF16)
    krot = zk[:, :SLOT] * cos + zk[:, SLOT:] * sin
    kpe_ref[...] = krot[:, QK_NOPE:QK_HEAD]
    ss_rot = jnp.sum(krot * krot, axis=-1, keepdims=True)
    for h in range(N_HEADS):
        sl = slice(h * SLOT, (h + 1) * SLOT)
        if h % 2 == 0:
            kn2 = _dot(lat_b, w_uk_ref[:, h * SLOT:(h + 2) * SLOT])
        kn = kn2[:, (h % 2) * SLOT:(h % 2 + 1) * SLOT]
        ss = jnp.sum(kn * kn, axis=-1, keepdims=True) + ss_rot
        kh = (kn + krot) * lax.rsqrt(ss * (1.0 / QK_HEAD) + EPS) * gk
        k_ref[:, sl] = kh.astype(BF16)
    vlane = lax.broadcasted_iota(jnp.int32, (1, HEADS_W), 1)
    ones_half = ((vlane // V_HEAD) % 2 != (vlane // SLOT) % 2).astype(F32)
    v_ref[...] = (_dot(lat_b, w_uv_ref[...]) + ones_half).astype(BF16)

    u = gc * hc
    ubuf[SUBLANES:SUBLANES + rows, :] = u
    u1 = ubuf[SUBLANES - 1:SUBLANES - 1 + rows, :]
    u2 = ubuf[SUBLANES - 2:SUBLANES - 2 + rows, :]
    if sample_mode:
        u1 = u1 * m1_ref[...] + s1_ref[...]
        u2 = u2 * m2_ref[...] + s2_ref[...]
        u_ref[...] = u
    else:
        utail_ref[...] = u[rows - SUBLANES:, :]
    y = cb_ref[...] + u2 * cw_ref[0:1, :] + u1 * cw_ref[1:2, :] + u * cw_ref[2:3, :]
    convn_ref[...] = _rms(gb * y, g_conv_ref[...]).astype(BF16)
    if not sample_mode:
        lat_copy.start()

        @pl.when(t == 0)
        def _():
            meta_copy.wait()

        @pl.when(last_step)
        def _():
            lat_copy.wait()


def _mixer_call(x, cos_t, sin_t, conv_in, weights, *, sample_mode):
    wspecs = [_const_spec(w.shape) for w in weights]
    if sample_mode:
        rows = x.shape[0]
        grid = (1,)
        row = lambda w: pl.BlockSpec((rows, w), lambda i: (0, 0))
        m1, m2, s1, s2 = conv_in
        in_specs = [row(D_MODEL), row(SLOT), row(SLOT), row(1), row(1), row(CONV_CH), row(CONV_CH)]
        args = [x, cos_t, sin_t, m1, m2, s1, s2]
        widths = [(HEADS_W, BF16), (N_HEADS * KV_LORA, BF16), (HEADS_W, BF16), (HEADS_W, BF16),
                  (KV_LORA, F32), (QK_ROPE, F32), (CONV_CH, BF16), (CONV_CH, F32)]
        out_shape = [jax.ShapeDtypeStruct((rows, w), d) for w, d in widths]
        out_specs = [row(w) for w, _ in widths]
        sem = ("arbitrary",)
        scratch = []
    else:
        nb, seq, _ = x.shape
        rows = ROW_TILE
        nt = seq // rows
        grid = (nb, nt)
        row3 = lambda w: pl.BlockSpec((None, rows, w), lambda b, t: (b, t, 0))
        tab = pl.BlockSpec((rows, SLOT), lambda b, t: (t, 0))
        u_tail, lat_meta = conv_in
        in_specs = [row3(D_MODEL), tab, tab, _const_spec(u_tail.shape), _const_spec(lat_meta.shape)]
        args = [x, cos_t, sin_t, u_tail, lat_meta]
        widths = [(HEADS_W, BF16), (HEADS_W, BF16), (HEADS_W, BF16), (KV_LORA, F32),
                  (QK_ROPE, F32), (CONV_CH, BF16)]
        out_shape = [jax.ShapeDtypeStruct((nb, seq, w), d) for w, d in widths]
        out_shape.append(jax.ShapeDtypeStruct((nb, nt, SUBLANES, CONV_CH), F32))
        out_specs = [row3(w) for w, _ in widths]
        out_specs.append(pl.BlockSpec((None, None, SUBLANES, CONV_CH), lambda b, t: (b, t, 0, 0)))
        out_shape[3] = jax.ShapeDtypeStruct((nb, N_META + seq, KV_LORA), F32)
        out_specs[3] = pl.BlockSpec(memory_space=pl.ANY)
        sem = ("arbitrary", "arbitrary")
        scratch = [pltpu.VMEM((rows, KV_LORA), F32), pltpu.SemaphoreType.DMA((2,))]
    return pl.pallas_call(
        functools.partial(_mixer_kernel, rows=rows, sample_mode=sample_mode),
        grid=grid,
        in_specs=in_specs + wspecs,
        out_specs=out_specs,
        out_shape=out_shape,
        scratch_shapes=[pltpu.VMEM((rows + 2 * SUBLANES, CONV_CH), F32)] + scratch,
        compiler_params=pltpu.CompilerParams(dimension_semantics=sem, vmem_limit_bytes=VMEM_LIMIT),
        name="mixer_sample" if sample_mode else "mixer_prompt",
    )(*args, *weights)


def _softmax_step(s, state, pv):
    m_old, l_old, acc_old = state
    n = s.shape[1] // LANES
    m_new = jnp.maximum(m_old, jnp.max(s, axis=-1, keepdims=True))
    alpha = jnp.exp2(m_old - m_new)
    ps = [jnp.exp2(s[:, c * LANES:(c + 1) * LANES] - m_new) for c in range(n)]
    l_new = None if l_old is None else alpha * l_old + functools.reduce(lambda a, b: a + b, ps)
    p = ps[0] if n == 1 else jnp.concatenate(ps, axis=1)
    w = acc_old.shape[-1] // LANES
    alpha_w = alpha if w == 1 else jnp.concatenate([alpha] * w, axis=1)
    return m_new, l_new, alpha_w * acc_old + pv(p.astype(BF16))


def _prompt_attn_kernel(q_ref, k_ref, v_ref, km_ref, vm_ref, o_ref, m_sc, acc_sc):
    seq = q_ref.shape[0]
    half = ATT_TQ // 2
    lane = lax.broadcasted_iota(jnp.int32, (ATT_TQ, SLOT), 1)
    m_sc[...] = jnp.full(m_sc.shape, -jnp.inf, F32)
    acc_sc[...] = jnp.zeros(acc_sc.shape, F32)

    meta = lane < N_META
    col_h = lax.broadcasted_iota(jnp.int32, (half, half), 1)
    row_h = lax.broadcasted_iota(jnp.int32, (half, half), 0)
    col_f = lax.broadcasted_iota(jnp.int32, (half, ATT_TK), 1)
    row_f = lax.broadcasted_iota(jnp.int32, (half, ATT_TK), 0)

    steps = []
    for qi in range(seq // ATT_TQ):
        for hh in range(2):
            steps.append((qi, hh, 0, ATT_TQ, None, meta))
        d0 = qi * ATT_TK
        for k0 in range(0, d0, ATT_TK_WIDE):
            for hh in range(2):
                steps.append((qi, hh, 0, ATT_TQ, slice(k0, min(k0 + ATT_TK_WIDE, d0)), None))
        for hh in range(2):
            steps.append((qi, hh, 0, half, slice(d0, d0 + half), col_h <= row_h))
            steps.append((qi, hh, half, half, slice(d0, d0 + ATT_TK), col_f <= row_f + half))
        steps.append((qi,))

    def scores(qi, hh, r0, nr, ks, mask):
        sl = slice(hh * SLOT, (hh + 1) * SLOT)
        k_blk = km_ref[:, sl] if ks is None else k_ref[ks, sl]
        s = _dot_nt(q_ref[pl.ds(qi * ATT_TQ + r0, nr), sl], k_blk)
        return s if mask is None else jnp.where(mask, s, -jnp.inf)

    def update(s, qi, hh, r0, nr, ks, mask):
        rows = pl.ds(r0, nr)
        sl = slice(hh * SLOT, (hh + 1) * SLOT)
        v_blk = vm_ref[:, sl] if ks is None else v_ref[ks, sl]
        m_ref, acc_ref = m_sc.at[qi, hh, rows], acc_sc.at[qi, hh, rows]
        m_ref[...], _, acc_ref[...] = _softmax_step(s, (m_ref[...], None, acc_ref[...]),
                                                    lambda p: _dot(p, v_blk))

    def finalize(qi):
        a, b = acc_sc[qi, 0], acc_sc[qi, 1]
        sums = pltpu.roll(jnp.where(lane < V_HEAD, b, a), V_HEAD, 1)
        o_ref[qi * ATT_TQ:(qi + 1) * ATT_TQ, :] = (
            jnp.where(lane < V_HEAD, a, b) / sums).astype(o_ref.dtype)

    pending = []
    for st in steps:
        pending.append((st, scores(*st) if len(st) > 1 else None))
        if len(pending) > ATT_LOOKAHEAD:
            d, s = pending.pop(0)
            finalize(*d) if s is None else update(s, *d)
    for d, s in pending:
        finalize(*d) if s is None else update(s, *d)


def _prompt_attn_call(q, k, v, k_meta, v_meta):
    nb, seq, _ = q.shape
    assert ATT_TQ == ATT_TK and seq % ATT_TQ == 0
    grid = (nb, N_HEADS // 2)
    return pl.pallas_call(
        _prompt_attn_kernel,
        grid=grid,
        in_specs=[
            pl.BlockSpec((None, seq, 2 * SLOT), lambda b, p: (b, 0, p)),
            pl.BlockSpec((None, seq, 2 * SLOT), lambda b, p: (b, 0, p)),
            pl.BlockSpec((None, seq, 2 * SLOT), lambda b, p: (b, 0, p)),
            pl.BlockSpec((LANES, 2 * SLOT), lambda b, p: (0, p)),
            pl.BlockSpec((LANES, 2 * SLOT), lambda b, p: (0, p)),
        ],
        out_specs=pl.BlockSpec((None, seq, SLOT), lambda b, p: (b, 0, p)),
        out_shape=jax.ShapeDtypeStruct((nb, seq, ATTN_WIDTH), BF16),
        scratch_shapes=[pltpu.VMEM((seq // ATT_TQ, 2, ATT_TQ, LANES), F32)] * 2,
        compiler_params=pltpu.CompilerParams(
            dimension_semantics=("arbitrary", "arbitrary"),
            vmem_limit_bytes=VMEM_LIMIT),
        name="prompt_attn",
    )(q, k, v, k_meta, v_meta)


def _sample_attn_kernel(pt_ref, qabs_ref, qpe_ref, wukt_ref, wuv_ref, tlat_ref, tkpe_ref,
                        lat_hbm, kpe_hbm, o_ref,
                        latb_sc, lhs_sc, lat_buf, kpe_buf, sems):
    n_pages = lat_buf.shape[1]
    chunk_pages = (CHUNK_PAGES,) * (n_pages // CHUNK_PAGES)
    b = pl.program_id(0)
    last = pl.num_programs(0) - 1
    slot = b % 2
    nq = qabs_ref.shape[0]
    reps = nq // N_HEADS
    nkn = N_HEADS * QK_NOPE
    qpe = qpe_ref[...]

    ngroup = lat_buf.shape[1] // DMA_GROUP

    def group_copies(bb, sl, gg):
        copies = []
        for j in range(gg * DMA_GROUP, (gg + 1) * DMA_GROUP):
            page = pt_ref[bb, j]
            copies.append(pltpu.make_async_copy(lat_hbm.at[page], lat_buf.at[sl, j], sems.at[0, sl]))
            copies.append(pltpu.make_async_copy(kpe_hbm.at[page], kpe_buf.at[sl, j], sems.at[1, sl]))
        return copies

    @pl.when(b == 0)
    def _():
        lhs_sc[0:nkn, :] = wukt_ref[...]
        for gg in range(ngroup):
            for cp in group_copies(0, 0, gg):
                cp.start()

    nxt = jnp.minimum(b + 1, last)

    lhs_sc[nkn:nkn + nq, :] = qabs_ref[...]
    state = (jnp.full((nq, LANES), -jnp.inf, F32), jnp.zeros((nq, LANES), F32),
             jnp.zeros((nq, KV_LORA), F32))

    def scores(lat_b, kpe_t):
        nk = lat_b.shape[0]
        full = _dot_nt(lhs_sc[...], lat_b)
        knt = full[:nkn]
        nsum = jnp.sum((knt * knt).reshape(QK_NOPE, N_HEADS, nk), axis=0)
        rsum = jnp.sum(kpe_t * kpe_t, axis=0, keepdims=True)
        r = lax.rsqrt((nsum + rsum) * (1.0 / QK_HEAD) + EPS)
        s = full[nkn:] + _dot(qpe, kpe_t.astype(BF16))
        return s * jnp.concatenate([r] * reps, axis=0)

    for gg in range(ngroup):
        for cp in group_copies(b, slot, gg):
            cp.wait()

    tail_b = tlat_ref[...].astype(BF16)
    s = scores(tail_b, tkpe_ref[...])
    key = lax.broadcasted_iota(jnp.int32, s.shape, 1)
    qt = lax.broadcasted_iota(jnp.int32, s.shape, 0) // N_HEADS
    s_tail = jnp.where(key <= qt, s, -jnp.inf)

    def chunk_scores(p0, npg):
        parts = [s_tail] if p0 == 0 else []
        for j in range(p0, p0 + npg, 2):
            for jj in (j, j + 1):
                latb_sc[jj * PAGE_SIZE:(jj + 1) * PAGE_SIZE, :] = lat_buf[slot, jj].astype(BF16)
            kpe_t = jnp.concatenate([kpe_buf[slot, j], kpe_buf[slot, j + 1]], axis=1)
            parts.append(scores(latb_sc[j * PAGE_SIZE:(j + 2) * PAGE_SIZE, :], kpe_t))
            if (j + 2) % DMA_ISSUE_PAGES == 0 and (j + 2) // DMA_ISSUE_PAGES <= ngroup:
                for cp in group_copies(nxt, 1 - slot, (j + 2) // DMA_ISSUE_PAGES - 1):
                    cp.start()
        return jnp.concatenate(parts, axis=1)

    def chunk_update(state, s, p0, npg):
        rows = slice(p0 * PAGE_SIZE, (p0 + npg) * PAGE_SIZE)
        if p0 == 0:
            pv = lambda p: _dot(p[:, :TAIL_KEYS], tail_b) + _dot(p[:, TAIL_KEYS:], latb_sc[rows, :])
        else:
            pv = lambda p: _dot(p, latb_sc[rows, :])
        return _softmax_step(s, state, pv)

    starts = [sum(chunk_pages[:i]) for i in range(len(chunk_pages))]
    pending = None
    for p0, npg in zip(starts, chunk_pages):
        s = chunk_scores(p0, npg)
        if pending is not None:
            state = chunk_update(state, *pending)
        pending = (s, p0, npg)
    state = chunk_update(state, *pending)

    _, l_fin, acc_fin = state
    o_lat = (acc_fin / jnp.sum(l_fin, axis=-1, keepdims=True)).astype(BF16)
    full = _dot(o_lat, wuv_ref[...])
    row_h = lax.broadcasted_iota(jnp.int32, full.shape, 0) % N_HEADS
    col_h = lax.broadcasted_iota(jnp.int32, full.shape, 1) // V_HEAD
    own = jnp.where(row_h == col_h, full, 0.0)
    o_ref[...] = jnp.sum(own.reshape(reps, N_HEADS, ATTN_WIDTH), axis=1)

    @pl.when(b == last)
    def _():
        for gg in range(ngroup):
            for cp in group_copies(last, 1 - slot, gg):
                cp.wait()


def _sample_attn_call(page_table, qabs, qpe, w_ukt, w_uv, tail_lat, tail_kpe_t, cache_lat, cache_kpe_t):
    nseq, n_pages = page_table.shape
    nq = qabs.shape[1]
    assert n_pages % DMA_GROUP == 0 and DMA_GROUP % 2 == 0 and DMA_ISSUE_PAGES % 2 == 0
    assert DMA_ISSUE_PAGES <= DMA_GROUP
    assert n_pages % CHUNK_PAGES == 0 and CHUNK_PAGES % 2 == 0
    grid = (nseq,)

    per_seq = lambda rows, width: pl.BlockSpec((None, rows, width), lambda b, pt: (b, 0, 0))
    whole = lambda shape: pl.BlockSpec(shape, lambda b, pt: (0,) * len(shape),
                                       pipeline_mode=pl.Buffered(1))
    in_specs = [per_seq(nq, KV_LORA), per_seq(nq, QK_ROPE), whole(w_ukt.shape), whole(w_uv.shape),
                per_seq(TAIL_KEYS, KV_LORA), per_seq(QK_ROPE, TAIL_KEYS),
                pl.BlockSpec(memory_space=pl.ANY), pl.BlockSpec(memory_space=pl.ANY)]
    reps = nq // N_HEADS
    return pl.pallas_call(
        _sample_attn_kernel,
        grid_spec=pltpu.PrefetchScalarGridSpec(
            num_scalar_prefetch=1,
            grid=grid,
            in_specs=in_specs,
            out_specs=pl.BlockSpec((None, reps, ATTN_WIDTH), lambda b, pt: (b, 0, 0)),
            scratch_shapes=[pltpu.VMEM((n_pages * PAGE_SIZE, KV_LORA), BF16),
                            pltpu.VMEM((N_HEADS * QK_NOPE + nq, KV_LORA), BF16),
                            pltpu.VMEM((2, n_pages, PAGE_SIZE, KV_LORA), F32),
                            pltpu.VMEM((2, n_pages, QK_ROPE, PAGE_SIZE), F32),
                            pltpu.SemaphoreType.DMA((2, 2))],
        ),
        out_shape=jax.ShapeDtypeStruct((nseq, reps, ATTN_WIDTH), F32),
        compiler_params=pltpu.CompilerParams(dimension_semantics=("arbitrary",),
                                             vmem_limit_bytes=VMEM_LIMIT),
        name="sample_attn",
    )(page_table, qabs, qpe, w_ukt, w_uv, tail_lat, tail_kpe_t, cache_lat, cache_kpe_t)


def _merge_ffn_kernel(x_ref, attn_ref, convn_ref, g_attn_ref, w_o_ref, g_ffn_ref, w_up_ref,
                      w_down_ref, y_ref):
    an = _rms(attn_ref[...].astype(F32), g_attn_ref[...]).astype(BF16)
    x1 = x_ref[...] + (_dot(an, w_o_ref[0:ATTN_WIDTH, :]) + _dot(convn_ref[...], w_o_ref[ATTN_WIDTH:, :]))
    hf = _rms(x1, g_ffn_ref[...]).astype(BF16)
    ffn = None
    for c in range(D_FF // FF_CHUNK):
        cs = slice(c * FF_CHUNK, (c + 1) * FF_CHUNK)
        up = jnp.maximum(_dot(hf, w_up_ref[:, cs]), 0.0)
        part = _dot((up * up).astype(BF16), w_down_ref[cs, :])
        ffn = part if ffn is None else ffn + part
    y_ref[...] = x1 + ffn


def _merge_ffn_call(x, attn, convn, weights, rows):
    n = x.shape[0]
    row = lambda w: pl.BlockSpec((rows, w), lambda i: (i, 0))
    return pl.pallas_call(
        _merge_ffn_kernel,
        grid=(n // rows,),
        in_specs=[row(D_MODEL), row(ATTN_WIDTH), row(CONV_CH)] + [_const_spec(w.shape) for w in weights],
        out_specs=row(D_MODEL),
        out_shape=jax.ShapeDtypeStruct((n, D_MODEL), F32),
        compiler_params=pltpu.CompilerParams(dimension_semantics=("arbitrary",),
                                             vmem_limit_bytes=VMEM_LIMIT),
        name="merge_ffn",
    )(x, attn, convn, *weights)


def _slots(w, width):
    k = w.shape[0]
    w = w.reshape(k, N_HEADS, width)
    return jnp.pad(w, ((0, 0), (0, 0), (0, SLOT - width))).reshape(k, HEADS_W)


def _rot_partner(w):
    return jnp.concatenate([-w[..., HALF_ROPE:], w[..., :HALF_ROPE]], axis=-1)


def _rope_slot(w):
    return jnp.pad(w, ((0, 0), (QK_NOPE, SLOT - QK_HEAD)))


def _layer_weights(w_in, q_lora_g, kv_lora_g, w_uq, w_ukv, q_norm_g, k_norm_g, conv_w, conv_b,
                   conv_out_g, norm_mix_g):
    o1 = Q_LORA
    o2 = o1 + KV_LORA
    o3 = o2 + QK_ROPE
    w_kpe = w_in[:, o2:o3]
    w_in_p = jnp.concatenate(
        [w_in[:, :o2], w_in[:, o3:], _rope_slot(w_kpe), _rope_slot(_rot_partner(w_kpe))],
        axis=1).astype(BF16)
    uq = w_uq.reshape(Q_LORA, N_HEADS, QK_HEAD)
    w_uq_p = _slots(w_uq, QK_HEAD).astype(BF16)
    uq_rot = jnp.pad(_rot_partner(uq[..., QK_NOPE:]), ((0, 0), (0, 0), (QK_NOPE, SLOT - QK_HEAD)))
    w_uq_r = uq_rot.reshape(Q_LORA, HEADS_W).astype(BF16)
    ukv = w_ukv.reshape(KV_LORA, N_HEADS, QK_NOPE + V_HEAD)
    w_uk_p = _slots(ukv[..., :QK_NOPE].reshape(KV_LORA, N_HEADS * QK_NOPE), QK_NOPE).astype(BF16)
    uv = ukv[..., QK_NOPE:]
    w_uv = uv.reshape(KV_LORA, ATTN_WIDTH).astype(BF16)
    odd_head = (np.arange(N_HEADS) % 2 == 1)[None, :, None]
    w_uv_slots = jnp.where(odd_head, jnp.concatenate([jnp.zeros_like(uv), uv], axis=-1),
                           jnp.concatenate([uv, jnp.zeros_like(uv)], axis=-1))
    w_uv_slots = w_uv_slots.reshape(KV_LORA, HEADS_W).astype(BF16)
    w_ukt = jnp.transpose(ukv[..., :QK_NOPE], (2, 1, 0)).reshape(N_HEADS * QK_NOPE, KV_LORA).astype(BF16)
    pad_g = lambda g: jnp.pad(g, (0, SLOT - QK_HEAD))[None, :]
    gq = pad_g(q_norm_g) * (ATTN_SCALE * LOG2_E)
    gk = pad_g(k_norm_g)
    mixer = (norm_mix_g[None, :], w_in_p, q_lora_g[None, :], w_uq_p, w_uq_r, kv_lora_g[None, :],
             w_uk_p, w_uv_slots, gq, gk, conv_w, conv_b[None, :], conv_out_g[None, :])
    return mixer, w_ukt, w_uv


def _rope_slot_tables(pos):
    inv_freq = ROPE_THETA ** (-(np.arange(0, QK_ROPE, 2, dtype=np.float64) / QK_ROPE))
    ang = pos.astype(np.float64)[:, None] * inv_freq[None, :]
    n = pos.shape[0]
    cos2 = np.concatenate([np.cos(ang)] * 2, axis=1)
    sin2 = np.concatenate([np.sin(ang)] * 2, axis=1)
    cos_t = np.concatenate([np.ones((n, QK_NOPE)), cos2, np.zeros((n, SLOT - QK_HEAD))], axis=1)
    sin_t = np.pad(sin2, ((0, 0), (QK_NOPE, SLOT - QK_HEAD)))
    return jnp.asarray(cos_t, F32), jnp.asarray(sin_t, F32)


def kernel(x_prompt, x_sample, cache_kv_latent, cache_k_rope, state_conv, page_table, meta_tokens,
           norm_mix_g, w_in, q_lora_g, kv_lora_g, w_uq, w_ukv, q_norm_g, k_norm_g, conv_w, conv_b,
           attn_out_g, conv_out_g, w_o, norm_ffn_g, w_up, w_down):
    depth = w_in.shape[0]
    assert depth == 1, "the prompt and sample streams are chained for a single layer"
    nb, seq, _ = x_prompt.shape
    nseq, dec, _ = x_sample.shape
    past = page_table.shape[1] * PAGE_SIZE
    l = 0

    mixer_w, w_ukt, w_uv = _layer_weights(w_in[l], q_lora_g[l], kv_lora_g[l], w_uq[l], w_ukv[l],
                                          q_norm_g[l], k_norm_g[l], conv_w[l], conv_b[l],
                                          conv_out_g[l], norm_mix_g[l])
    ffn_w = (attn_out_g[l][None, :], w_o[l].astype(BF16), norm_ffn_g[l][None, :],
             w_up[l].astype(BF16), w_down[l].astype(BF16))

    ns = nseq * dec
    xs_rows = jnp.concatenate([x_sample.reshape(ns, D_MODEL), meta_tokens.astype(F32)], axis=0)
    t_in_seq = np.concatenate([np.tile(np.arange(dec), nseq), np.arange(N_META)])
    pos_s = np.concatenate([np.tile(past + np.arange(dec), nseq), np.arange(N_META)])
    cos_s, sin_s = _rope_slot_tables(pos_s)
    m1 = jnp.asarray((t_in_seq >= 1)[:, None], F32)
    m2 = jnp.asarray((t_in_seq >= 2)[:, None], F32)
    st = state_conv[l].astype(F32)
    zrow = jnp.zeros((nseq, 1, CONV_CH), F32)
    s1 = jnp.concatenate([st[:, 1:2], zrow, zrow, zrow], axis=1)[:, :dec]
    s2 = jnp.concatenate([st[:, 0:1], st[:, 1:2], zrow, zrow], axis=1)[:, :dec]
    zmeta = jnp.zeros((N_META, CONV_CH), F32)
    s1 = jnp.concatenate([s1.reshape(ns, CONV_CH), zmeta], axis=0)
    s2 = jnp.concatenate([s2.reshape(ns, CONV_CH), zmeta], axis=0)
    (qk_s, qabs_s, k_s, v_s, lat_s, kpe_s, convn_s, u_s) = _mixer_call(
        xs_rows, cos_s, sin_s, (m1, m2, s1, s2), mixer_w, sample_mode=True)

    pad_meta = lambda a: jnp.pad(a[ns:], ((0, LANES - N_META), (0, 0)))
    k_meta, v_meta = pad_meta(k_s), pad_meta(v_s)
    lat_meta, kpe_meta, u_meta = lat_s[ns:], kpe_s[ns:], u_s[ns:]

    cos_p, sin_p = _rope_slot_tables(N_META + np.arange(seq))
    q_p, k_p, v_p, lat_p, kpe_p, convn_p, utail_p = _mixer_call(
        x_prompt, cos_p, sin_p, (u_meta[N_META - SUBLANES:], lat_meta), mixer_w, sample_mode=False)
    attn_p = _prompt_attn_call(q_p, k_p, v_p, k_meta, v_meta)
    y_prompt = _merge_ffn_call(x_prompt.reshape(nb * seq, D_MODEL), attn_p.reshape(nb * seq, ATTN_WIDTH),
                               convn_p.reshape(nb * seq, CONV_CH), ffn_w, ROW_TILE)
    y_prompt = y_prompt.reshape(nb, seq, D_MODEL)
    bcast = lambda a: jnp.broadcast_to(a[None], (nb,) + a.shape)
    new_lat_prompt = lat_p[None]
    new_kpe_prompt = jnp.concatenate([bcast(kpe_meta), kpe_p], axis=1)[None]
    new_conv_prompt = utail_p[:, -1, SUBLANES - (CONV_W - 1):][None]

    nq = dec * N_HEADS
    qabs = qabs_s[:ns].reshape(nseq, nq, KV_LORA)
    qpe = qk_s[:ns].reshape(ns, N_HEADS, SLOT)[:, :, QK_NOPE:QK_HEAD].reshape(nseq, nq, QK_ROPE)
    assert dec <= TAIL_KEYS
    pad_tail = lambda a: jnp.pad(a[:ns].reshape(nseq, dec, -1), ((0, 0), (0, TAIL_KEYS - dec), (0, 0)))
    attn_s = _sample_attn_call(page_table, qabs, qpe, w_ukt, w_uv, pad_tail(lat_s),
                               jnp.swapaxes(pad_tail(kpe_s), 1, 2),
                               cache_kv_latent[l], jnp.swapaxes(cache_k_rope[l], 1, 2))
    y_sample = _merge_ffn_call(x_sample.reshape(ns, D_MODEL), attn_s.reshape(ns, ATTN_WIDTH),
                               convn_s[:ns], ffn_w, ns)
    y_sample = y_sample.reshape(nseq, dec, D_MODEL)
    new_lat_sample = lat_s[:ns].reshape(nseq, dec, KV_LORA)[None]
    new_kpe_sample = kpe_s[:ns].reshape(nseq, dec, QK_ROPE)[None]
    us = jnp.concatenate([st, u_s[:ns].reshape(nseq, dec, CONV_CH)], axis=1)
    new_conv_sample = us[:, -(CONV_W - 1):][None]

    return (y_prompt, y_sample, new_lat_prompt, new_kpe_prompt, new_conv_prompt,
            new_lat_sample, new_kpe_sample, new_conv_sample)
```

```python
import functools

import jax
import jax.numpy as jnp
import numpy as np
from jax import lax
from jax.experimental import pallas as pl
from jax.experimental.pallas import tpu as pltpu

D_MODEL = 1024
N_META = 16
N_HEADS = 8
QK_NOPE = 64
QK_ROPE = 32
V_HEAD = 64
QK_HEAD = QK_NOPE + QK_ROPE
Q_LORA = 384
KV_LORA = 256
ATTN_WIDTH = N_HEADS * V_HEAD
CONV_CH = D_MODEL - ATTN_WIDTH
CONV_W = 3
D_FF = 4 * D_MODEL
ROPE_THETA = 10000.0
EPS = 1e-6
PAGE_SIZE = 128
ATTN_SCALE = QK_HEAD ** -0.5
LOG2_E = 1.4426950408889634

LANES = 128
SUBLANES = 8
SLOT = LANES
HEADS_W = N_HEADS * SLOT
HALF_ROPE = QK_ROPE // 2

C_CQ = 0
C_CKV = C_CQ + Q_LORA
C_GB = C_CKV + KV_LORA
C_GC = C_GB + CONV_CH
C_HC = C_GC + CONV_CH
C_KPE = C_HC + CONV_CH
C_KPR = C_KPE + SLOT
IN_W = C_KPR + SLOT

ROW_TILE = 512
ATT_TQ = 256
ATT_TK = 256
ATT_TK_WIDE = 2048
ATT_LOOKAHEAD = 4
FF_CHUNK = 1024
CHUNK_PAGES = 8
GATHER_SLOTS = 3
DMA_GROUP = 8
DMA_ISSUE_PAGES = 2
TAIL_KEYS = PAGE_SIZE
VMEM_LIMIT = 52 * 1024 * 1024

BF16 = jnp.bfloat16
F32 = jnp.float32

_NT = (((1,), (1,)), ((), ()))


def _dot(a, b):
    return jnp.dot(a, b, preferred_element_type=F32)


def _dot_nt(a, b):
    return lax.dot_general(a, b, _NT, preferred_element_type=F32)


def _rms(x, g):
    return x * lax.rsqrt(jnp.mean(x * x, axis=-1, keepdims=True) + EPS) * g


def _const_spec(shape):
    nd = len(shape)
    return pl.BlockSpec(shape, lambda *_: (0,) * nd, pipeline_mode=pl.Buffered(1))


def _mixer_kernel(*refs, rows, sample_mode):
    it = iter(refs)
    x_ref, cos_ref, sin_ref = next(it), next(it), next(it)
    if sample_mode:
        m1_ref, m2_ref, s1_ref, s2_ref = next(it), next(it), next(it), next(it)
    else:
        tail_in_ref, lat_meta_ref = next(it), next(it)
    (g_mix_ref, w_in_ref, g_ql_ref, w_uq_ref, w_uqr_ref, g_kvl_ref, w_uk_ref, w_uv_ref,
     gq_ref, gk_ref, cw_ref, cb_ref, g_conv_ref) = (next(it) for _ in range(13))
    if sample_mode:
        (qk_ref, qabs_ref, k_ref, v_ref, lat_ref, kpe_ref, convn_ref, u_ref) = (next(it) for _ in range(8))
    else:
        (q_ref, k_ref, v_ref, lat_hbm, kpe_ref, convn_ref, utail_ref) = (next(it) for _ in range(7))
    ubuf = next(it)

    if sample_mode:
        ubuf[0:SUBLANES, :] = jnp.zeros((SUBLANES, CONV_CH), F32)
    else:
        lat_stage, lat_sems = next(it), next(it)
        b = pl.program_id(0)
        t = pl.program_id(1)
        lat_copy = pltpu.make_async_copy(
            lat_stage, lat_hbm.at[b, pl.ds(N_META + t * rows, rows)], lat_sems.at[0])
        meta_copy = pltpu.make_async_copy(lat_meta_ref, lat_hbm.at[b, pl.ds(0, N_META)], lat_sems.at[1])
        first_step = jnp.logical_and(b == 0, t == 0)
        last_step = jnp.logical_and(b == pl.num_programs(0) - 1, t == pl.num_programs(1) - 1)

        @pl.when(jnp.logical_not(first_step))
        def _():
            lat_copy.wait()

        @pl.when(t == 0)
        def _():
            ubuf[0:SUBLANES, :] = tail_in_ref[...]
            meta_copy.start()

        @pl.when(t != 0)
        def _():
            ubuf[0:SUBLANES, :] = ubuf[rows:rows + SUBLANES, :]

    hn = _rms(x_ref[...], g_mix_ref[...]).astype(BF16)
    cos = cos_ref[...]
    sin = sin_ref[...]

    cq = _dot(hn, w_in_ref[:, C_CQ:C_CKV])
    ckv = _dot(hn, w_in_ref[:, C_CKV:C_GB])
    zk = _dot(hn, w_in_ref[:, C_KPE:IN_W])
    gc = _dot(hn, w_in_ref[:, C_GC:C_HC])
    hc = _dot(hn, w_in_ref[:, C_HC:C_KPE])
    gb = _dot(hn, w_in_ref[:, C_GB:C_GC])

    cqn = _rms(cq, g_ql_ref[...]).astype(BF16)
    gq = gq_ref[...]
    gk = gk_ref[...]
    for h in range(N_HEADS):
        sl = slice(h * SLOT, (h + 1) * SLOT)
        if h % 2 == 0:
            sl2 = slice(h * SLOT, (h + 2) * SLOT)
            q2 = _dot(cqn, w_uq_ref[:, sl2])
            qr2 = _dot(cqn, w_uqr_ref[:, sl2])
        own = slice((h % 2) * SLOT, (h % 2 + 1) * SLOT)
        qh = q2[:, own] * cos + qr2[:, own] * sin
        ss = jnp.sum(qh * qh, axis=-1, keepdims=True)
        qh = qh * lax.rsqrt(ss * (1.0 / QK_HEAD) + EPS) * gq
        if sample_mode:
            qkh = (qh * gk).astype(BF16)
            qk_ref[:, sl] = qkh
            qabs_ref[:, h * KV_LORA:(h + 1) * KV_LORA] = _dot_nt(qkh, w_uk_ref[:, sl]).astype(BF16)
        else:
            q_ref[:, sl] = qh.astype(BF16)

    lat = _rms(ckv, g_kvl_ref[...])
    if sample_mode:
        lat_ref[...] = lat
    else:
        lat_stage[...] = lat
    lat_b = lat.astype(BF16)
    krot = zk[:, :SLOT] * cos + zk[:, SLOT:] * sin
    kpe_ref[...] = krot[:, QK_NOPE:QK_HEAD]
    ss_rot = jnp.sum(krot * krot, axis=-1, keepdims=True)
    for h in range(N_HEADS):
        sl = slice(h * SLOT, (h + 1) * SLOT)
        if h % 2 == 0:
            kn2 = _dot(lat_b, w_uk_ref[:, h * SLOT:(h + 2) * SLOT])
        kn = kn2[:, (h % 2) * SLOT:(h % 2 + 1) * SLOT]
        ss = jnp.sum(kn * kn, axis=-1, keepdims=True) + ss_rot
        kh = (kn + krot) * lax.rsqrt(ss * (1.0 / QK_HEAD) + EPS) * gk
        k_ref[:, sl] = kh.astype(BF16)
    vlane = lax.broadcasted_iota(jnp.int32, (1, HEADS_W), 1)
    ones_half = ((vlane // V_HEAD) % 2 != (vlane // SLOT) % 2).astype(F32)
    v_ref[...] = (_dot(lat_b, w_uv_ref[...]) + ones_half).astype(BF16)

    u = gc * hc
    ubuf[SUBLANES:SUBLANES + rows, :] = u
    u1 = ubuf[SUBLANES - 1:SUBLANES - 1 + rows, :]
    u2 = ubuf[SUBLANES - 2:SUBLANES - 2 + rows, :]
    if sample_mode:
        u1 = u1 * m1_ref[...] + s1_ref[...]
        u2 = u2 * m2_ref[...] + s2_ref[...]
        u_ref[...] = u
    else:
        utail_ref[...] = u[rows - SUBLANES:, :]
    y = cb_ref[...] + u2 * cw_ref[0:1, :] + u1 * cw_ref[1:2, :] + u * cw_ref[2:3, :]
    convn_ref[...] = _rms(gb * y, g_conv_ref[...]).astype(BF16)
    if not sample_mode:
        lat_copy.start()

        @pl.when(t == 0)
        def _():
            meta_copy.wait()

        @pl.when(last_step)
        def _():
            lat_copy.wait()


def _mixer_call(x, cos_t, sin_t, conv_in, weights, *, sample_mode):
    wspecs = [_const_spec(w.shape) for w in weights]
    if sample_mode:
        rows = x.shape[0]
        grid = (1,)
        row = lambda w: pl.BlockSpec((rows, w), lambda i: (0, 0))
        m1, m2, s1, s2 = conv_in
        in_specs = [row(D_MODEL), row(SLOT), row(SLOT), row(1), row(1), row(CONV_CH), row(CONV_CH)]
        args = [x, cos_t, sin_t, m1, m2, s1, s2]
        widths = [(HEADS_W, BF16), (N_HEADS * KV_LORA, BF16), (HEADS_W, BF16), (HEADS_W, BF16),
                  (KV_LORA, F32), (QK_ROPE, F32), (CONV_CH, BF16), (CONV_CH, F32)]
        out_shape = [jax.ShapeDtypeStruct((rows, w), d) for w, d in widths]
        out_specs = [row(w) for w, _ in widths]
        sem = ("arbitrary",)
        scratch = []
    else:
        nb, seq, _ = x.shape
        rows = ROW_TILE
        nt = seq // rows
        grid = (nb, nt)
        row3 = lambda w: pl.BlockSpec((None, rows, w), lambda b, t: (b, t, 0))
        tab = pl.BlockSpec((rows, SLOT), lambda b, t: (t, 0))
        u_tail, lat_meta = conv_in
        in_specs = [row3(D_MODEL), tab, tab, _const_spec(u_tail.shape), _const_spec(lat_meta.shape)]
        args = [x, cos_t, sin_t, u_tail, lat_meta]
        widths = [(HEADS_W, BF16), (HEADS_W, BF16), (HEADS_W, BF16), (KV_LORA, F32),
                  (QK_ROPE, F32), (CONV_CH, BF16)]
        out_shape = [jax.ShapeDtypeStruct((nb, seq, w), d) for w, d in widths]
        out_shape.append(jax.ShapeDtypeStruct((nb, nt, SUBLANES, CONV_CH), F32))
        out_specs = [row3(w) for w, _ in widths]
        out_specs.append(pl.BlockSpec((None, None, SUBLANES, CONV_CH), lambda b, t: (b, t, 0, 0)))
        out_shape[3] = jax.ShapeDtypeStruct((nb, N_META + seq, KV_LORA), F32)
        out_specs[3] = pl.BlockSpec(memory_space=pl.ANY)
        sem = ("arbitrary", "arbitrary")
        scratch = [pltpu.VMEM((rows, KV_LORA), F32), pltpu.SemaphoreType.DMA((2,))]
    return pl.pallas_call(
        functools.partial(_mixer_kernel, rows=rows, sample_mode=sample_mode),
        grid=grid,
        in_specs=in_specs + wspecs,
        out_specs=out_specs,
        out_shape=out_shape,
        scratch_shapes=[pltpu.VMEM((rows + 2 * SUBLANES, CONV_CH), F32)] + scratch,
        compiler_params=pltpu.CompilerParams(dimension_semantics=sem, vmem_limit_bytes=VMEM_LIMIT),
        name="mixer_sample" if sample_mode else "mixer_prompt",
    )(*args, *weights)


def _softmax_step(s, state, pv):
    m_old, l_old, acc_old = state
    n = s.shape[1] // LANES
    m_new = jnp.maximum(m_old, jnp.max(s, axis=-1, keepdims=True))
    alpha = jnp.exp2(m_old - m_new)
    ps = [jnp.exp2(s[:, c * LANES:(c + 1) * LANES] - m_new) for c in range(n)]
    l_new = None if l_old is None else alpha * l_old + functools.reduce(lambda a, b: a + b, ps)
    p = ps[0] if n == 1 else jnp.concatenate(ps, axis=1)
    w = acc_old.shape[-1] // LANES
    alpha_w = alpha if w == 1 else jnp.concatenate([alpha] * w, axis=1)
    return m_new, l_new, alpha_w * acc_old + pv(p.astype(BF16))


def _prompt_attn_kernel(q_ref, k_ref, v_ref, km_ref, vm_ref, o_ref, m_sc, acc_sc):
    seq = q_ref.shape[0]
    half = ATT_TQ // 2
    lane = lax.broadcasted_iota(jnp.int32, (ATT_TQ, SLOT), 1)
    m_sc[...] = jnp.full(m_sc.shape, -jnp.inf, F32)
    acc_sc[...] = jnp.zeros(acc_sc.shape, F32)

    meta = lane < N_META
    col_h = lax.broadcasted_iota(jnp.int32, (half, half), 1)
    row_h = lax.broadcasted_iota(jnp.int32, (half, half), 0)
    col_f = lax.broadcasted_iota(jnp.int32, (half, ATT_TK), 1)
    row_f = lax.broadcasted_iota(jnp.int32, (half, ATT_TK), 0)

    steps = []
    for qi in range(seq // ATT_TQ):
        for hh in range(2):
            steps.append((qi, hh, 0, ATT_TQ, None, meta))
        d0 = qi * ATT_TK
        for k0 in range(0, d0, ATT_TK_WIDE):
            for hh in range(2):
                steps.append((qi, hh, 0, ATT_TQ, slice(k0, min(k0 + ATT_TK_WIDE, d0)), None))
        for hh in range(2):
            steps.append((qi, hh, 0, half, slice(d0, d0 + half), col_h <= row_h))
            steps.append((qi, hh, half, half, slice(d0, d0 + ATT_TK), col_f <= row_f + half))
        steps.append((qi,))

    def scores(qi, hh, r0, nr, ks, mask):
        sl = slice(hh * SLOT, (hh + 1) * SLOT)
        k_blk = km_ref[:, sl] if ks is None else k_ref[ks, sl]
        s = _dot_nt(q_ref[pl.ds(qi * ATT_TQ + r0, nr), sl], k_blk)
        return s if mask is None else jnp.where(mask, s, -jnp.inf)

    def update(s, qi, hh, r0, nr, ks, mask):
        rows = pl.ds(r0, nr)
        sl = slice(hh * SLOT, (hh + 1) * SLOT)
        v_blk = vm_ref[:, sl] if ks is None else v_ref[ks, sl]
        m_ref, acc_ref = m_sc.at[qi, hh, rows], acc_sc.at[qi, hh, rows]
        m_ref[...], _, acc_ref[...] = _softmax_step(s, (m_ref[...], None, acc_ref[...]),
                                                    lambda p: _dot(p, v_blk))

    def finalize(qi):
        a, b = acc_sc[qi, 0], acc_sc[qi, 1]
        sums = pltpu.roll(jnp.where(lane < V_HEAD, b, a), V_HEAD, 1)
        o_ref[qi * ATT_TQ:(qi + 1) * ATT_TQ, :] = (
            jnp.where(lane < V_HEAD, a, b) / sums).astype(o_ref.dtype)

    pending = []
    for st in steps:
        pending.append((st, scores(*st) if len(st) > 1 else None))
        if len(pending) > ATT_LOOKAHEAD:
            d, s = pending.pop(0)
            finalize(*d) if s is None else update(s, *d)
    for d, s in pending:
        finalize(*d) if s is None else update(s, *d)


def _prompt_attn_call(q, k, v, k_meta, v_meta):
    nb, seq, _ = q.shape
    assert ATT_TQ == ATT_TK and seq % ATT_TQ == 0
    grid = (nb, N_HEADS // 2)
    return pl.pallas_call(
        _prompt_attn_kernel,
        grid=grid,
        in_specs=[
            pl.BlockSpec((None, seq, 2 * SLOT), lambda b, p: (b, 0, p)),
            pl.BlockSpec((None, seq, 2 * SLOT), lambda b, p: (b, 0, p)),
            pl.BlockSpec((None, seq, 2 * SLOT), lambda b, p: (b, 0, p)),
            pl.BlockSpec((LANES, 2 * SLOT), lambda b, p: (0, p)),
            pl.BlockSpec((LANES, 2 * SLOT), lambda b, p: (0, p)),
        ],
        out_specs=pl.BlockSpec((None, seq, SLOT), lambda b, p: (b, 0, p)),
        out_shape=jax.ShapeDtypeStruct((nb, seq, ATTN_WIDTH), BF16),
        scratch_shapes=[pltpu.VMEM((seq // ATT_TQ, 2, ATT_TQ, LANES), F32)] * 2,
        compiler_params=pltpu.CompilerParams(
            dimension_semantics=("arbitrary", "arbitrary"),
            vmem_limit_bytes=VMEM_LIMIT),
        name="prompt_attn",
    )(q, k, v, k_meta, v_meta)


def _sample_attn_kernel(pt_ref, qabs_ref, qpe_ref, wukt_ref, wuv_ref, tlat_ref, tkpe_ref,
                        lat_hbm, kpe_hbm, o_ref,
                        latb_sc, lhs_sc, lat_buf, kpe_buf, sems):
    n_pages = lat_buf.shape[1]
    chunk_pages = (CHUNK_PAGES,) * (n_pages // CHUNK_PAGES)
    b = pl.program_id(0)
    last = pl.num_programs(0) - 1
    ahead = GATHER_SLOTS - 1
    slot = b % GATHER_SLOTS
    nslot = (b + ahead) % GATHER_SLOTS
    nq = qabs_ref.shape[0]
    reps = nq // N_HEADS
    nkn = N_HEADS * QK_NOPE
    qpe = qpe_ref[...]

    ngroup = lat_buf.shape[1] // DMA_GROUP

    def group_copies(bb, sl, gg):
        copies = []
        for j in range(gg * DMA_GROUP, (gg + 1) * DMA_GROUP):
            page = pt_ref[bb, j]
            copies.append(pltpu.make_async_copy(lat_hbm.at[page], lat_buf.at[sl, j], sems.at[0, sl]))
            copies.append(pltpu.make_async_copy(kpe_hbm.at[page], kpe_buf.at[sl, j], sems.at[1, sl]))
        return copies

    @pl.when(b == 0)
    def _():
        lhs_sc[0:nkn, :] = wukt_ref[...]
        for first in range(ahead):
            for gg in range(ngroup):
                for cp in group_copies(jnp.minimum(first, last), first, gg):
                    cp.start()

    nxt = jnp.minimum(b + ahead, last)

    lhs_sc[nkn:nkn + nq, :] = qabs_ref[...]
    state = (jnp.full((nq, LANES), -jnp.inf, F32), jnp.zeros((nq, LANES), F32),
             jnp.zeros((nq, KV_LORA), F32))

    def scores(lat_b, kpe_t):
        nk = lat_b.shape[0]
        full = _dot_nt(lhs_sc[...], lat_b)
        knt = full[:nkn]
        nsum = jnp.sum((knt * knt).reshape(QK_NOPE, N_HEADS, nk), axis=0)
        rsum = jnp.sum(kpe_t * kpe_t, axis=0, keepdims=True)
        r = lax.rsqrt((nsum + rsum) * (1.0 / QK_HEAD) + EPS)
        s = full[nkn:] + _dot(qpe, kpe_t.astype(BF16))
        return s * jnp.concatenate([r] * reps, axis=0)

    for gg in range(ngroup):
        for cp in group_copies(b, slot, gg):
            cp.wait()

    tail_b = tlat_ref[...].astype(BF16)
    s = scores(tail_b, tkpe_ref[...])
    key = lax.broadcasted_iota(jnp.int32, s.shape, 1)
    qt = lax.broadcasted_iota(jnp.int32, s.shape, 0) // N_HEADS
    s_tail = jnp.where(key <= qt, s, -jnp.inf)

    def chunk_scores(p0, npg):
        parts = [s_tail] if p0 == 0 else []
        for j in range(p0, p0 + npg, 2):
            for jj in (j, j + 1):
                latb_sc[jj * PAGE_SIZE:(jj + 1) * PAGE_SIZE, :] = lat_buf[slot, jj].astype(BF16)
            kpe_t = jnp.concatenate([kpe_buf[slot, j], kpe_buf[slot, j + 1]], axis=1)
            parts.append(scores(latb_sc[j * PAGE_SIZE:(j + 2) * PAGE_SIZE, :], kpe_t))
            if (j + 2) % DMA_ISSUE_PAGES == 0 and (j + 2) // DMA_ISSUE_PAGES <= ngroup:
                for cp in group_copies(nxt, nslot, (j + 2) // DMA_ISSUE_PAGES - 1):
                    cp.start()
        return jnp.concatenate(parts, axis=1)

    def chunk_update(state, s, p0, npg):
        rows = slice(p0 * PAGE_SIZE, (p0 + npg) * PAGE_SIZE)
        if p0 == 0:
            pv = lambda p: _dot(p[:, :TAIL_KEYS], tail_b) + _dot(p[:, TAIL_KEYS:], latb_sc[rows, :])
        else:
            pv = lambda p: _dot(p, latb_sc[rows, :])
        return _softmax_step(s, state, pv)

    starts = [sum(chunk_pages[:i]) for i in range(len(chunk_pages))]
    pending = None
    for p0, npg in zip(starts, chunk_pages):
        s = chunk_scores(p0, npg)
        if pending is not None:
            state = chunk_update(state, *pending)
        pending = (s, p0, npg)
    state = chunk_update(state, *pending)

    _, l_fin, acc_fin = state
    o_lat = (acc_fin / jnp.sum(l_fin, axis=-1, keepdims=True)).astype(BF16)
    full = _dot(o_lat, wuv_ref[...])
    row_h = lax.broadcasted_iota(jnp.int32, full.shape, 0) % N_HEADS
    col_h = lax.broadcasted_iota(jnp.int32, full.shape, 1) // V_HEAD
    own = jnp.where(row_h == col_h, full, 0.0)
    o_ref[...] = jnp.sum(own.reshape(reps, N_HEADS, ATTN_WIDTH), axis=1)

    @pl.when(b == last)
    def _():
        for extra in range(1, GATHER_SLOTS):
            for gg in range(ngroup):
                for cp in group_copies(last, (slot + extra) % GATHER_SLOTS, gg):
                    cp.wait()


def _sample_attn_call(page_table, qabs, qpe, w_ukt, w_uv, tail_lat, tail_kpe_t, cache_lat, cache_kpe_t):
    nseq, n_pages = page_table.shape
    nq = qabs.shape[1]
    assert n_pages % DMA_GROUP == 0 and DMA_GROUP % 2 == 0 and DMA_ISSUE_PAGES % 2 == 0
    assert DMA_ISSUE_PAGES <= DMA_GROUP
    assert n_pages % CHUNK_PAGES == 0 and CHUNK_PAGES % 2 == 0
    grid = (nseq,)

    per_seq = lambda rows, width: pl.BlockSpec((None, rows, width), lambda b, pt: (b, 0, 0))
    whole = lambda shape: pl.BlockSpec(shape, lambda b, pt: (0,) * len(shape),
                                       pipeline_mode=pl.Buffered(1))
    in_specs = [per_seq(nq, KV_LORA), per_seq(nq, QK_ROPE), whole(w_ukt.shape), whole(w_uv.shape),
                per_seq(TAIL_KEYS, KV_LORA), per_seq(QK_ROPE, TAIL_KEYS),
                pl.BlockSpec(memory_space=pl.ANY), pl.BlockSpec(memory_space=pl.ANY)]
    reps = nq // N_HEADS
    return pl.pallas_call(
        _sample_attn_kernel,
        grid_spec=pltpu.PrefetchScalarGridSpec(
            num_scalar_prefetch=1,
            grid=grid,
            in_specs=in_specs,
            out_specs=pl.BlockSpec((None, reps, ATTN_WIDTH), lambda b, pt: (b, 0, 0)),
            scratch_shapes=[pltpu.VMEM((n_pages * PAGE_SIZE, KV_LORA), BF16),
                            pltpu.VMEM((N_HEADS * QK_NOPE + nq, KV_LORA), BF16),
                            pltpu.VMEM((GATHER_SLOTS, n_pages, PAGE_SIZE, KV_LORA), F32),
                            pltpu.VMEM((GATHER_SLOTS, n_pages, QK_ROPE, PAGE_SIZE), F32),
                            pltpu.SemaphoreType.DMA((2, GATHER_SLOTS))],
        ),
        out_shape=jax.ShapeDtypeStruct((nseq, reps, ATTN_WIDTH), F32),
        compiler_params=pltpu.CompilerParams(dimension_semantics=("arbitrary",),
                                             vmem_limit_bytes=VMEM_LIMIT),
        name="sample_attn",
    )(page_table, qabs, qpe, w_ukt, w_uv, tail_lat, tail_kpe_t, cache_lat, cache_kpe_t)


def _merge_ffn_kernel(x_ref, attn_ref, convn_ref, g_attn_ref, w_o_ref, g_ffn_ref, w_up_ref,
                      w_down_ref, y_ref):
    an = _rms(attn_ref[...].astype(F32), g_attn_ref[...]).astype(BF16)
    x1 = x_ref[...] + (_dot(an, w_o_ref[0:ATTN_WIDTH, :]) + _dot(convn_ref[...], w_o_ref[ATTN_WIDTH:, :]))
    hf = _rms(x1, g_ffn_ref[...]).astype(BF16)
    ffn = None
    for c in range(D_FF // FF_CHUNK):
        cs = slice(c * FF_CHUNK, (c + 1) * FF_CHUNK)
        up = jnp.maximum(_dot(hf, w_up_ref[:, cs]), 0.0)
        part = _dot((up * up).astype(BF16), w_down_ref[cs, :])
        ffn = part if ffn is None else ffn + part
    y_ref[...] = x1 + ffn


def _merge_ffn_call(x, attn, convn, weights, rows):
    n = x.shape[0]
    row = lambda w: pl.BlockSpec((rows, w), lambda i: (i, 0))
    return pl.pallas_call(
        _merge_ffn_kernel,
        grid=(n // rows,),
        in_specs=[row(D_MODEL), row(ATTN_WIDTH), row(CONV_CH)] + [_const_spec(w.shape) for w in weights],
        out_specs=row(D_MODEL),
        out_shape=jax.ShapeDtypeStruct((n, D_MODEL), F32),
        compiler_params=pltpu.CompilerParams(dimension_semantics=("arbitrary",),
                                             vmem_limit_bytes=VMEM_LIMIT),
        name="merge_ffn",
    )(x, attn, convn, *weights)


def _slots(w, width):
    k = w.shape[0]
    w = w.reshape(k, N_HEADS, width)
    return jnp.pad(w, ((0, 0), (0, 0), (0, SLOT - width))).reshape(k, HEADS_W)


def _rot_partner(w):
    return jnp.concatenate([-w[..., HALF_ROPE:], w[..., :HALF_ROPE]], axis=-1)


def _rope_slot(w):
    return jnp.pad(w, ((0, 0), (QK_NOPE, SLOT - QK_HEAD)))


def _layer_weights(w_in, q_lora_g, kv_lora_g, w_uq, w_ukv, q_norm_g, k_norm_g, conv_w, conv_b,
                   conv_out_g, norm_mix_g):
    o1 = Q_LORA
    o2 = o1 + KV_LORA
    o3 = o2 + QK_ROPE
    w_kpe = w_in[:, o2:o3]
    w_in_p = jnp.concatenate(
        [w_in[:, :o2], w_in[:, o3:], _rope_slot(w_kpe), _rope_slot(_rot_partner(w_kpe))],
        axis=1).astype(BF16)
    uq = w_uq.reshape(Q_LORA, N_HEADS, QK_HEAD)
    w_uq_p = _slots(w_uq, QK_HEAD).astype(BF16)
    uq_rot = jnp.pad(_rot_partner(uq[..., QK_NOPE:]), ((0, 0), (0, 0), (QK_NOPE, SLOT - QK_HEAD)))
    w_uq_r = uq_rot.reshape(Q_LORA, HEADS_W).astype(BF16)
    ukv = w_ukv.reshape(KV_LORA, N_HEADS, QK_NOPE + V_HEAD)
    w_uk_p = _slots(ukv[..., :QK_NOPE].reshape(KV_LORA, N_HEADS * QK_NOPE), QK_NOPE).astype(BF16)
    uv = ukv[..., QK_NOPE:]
    w_uv = uv.reshape(KV_LORA, ATTN_WIDTH).astype(BF16)
    odd_head = (np.arange(N_HEADS) % 2 == 1)[None, :, None]
    w_uv_slots = jnp.where(odd_head, jnp.concatenate([jnp.zeros_like(uv), uv], axis=-1),
                           jnp.concatenate([uv, jnp.zeros_like(uv)], axis=-1))
    w_uv_slots = w_uv_slots.reshape(KV_LORA, HEADS_W).astype(BF16)
    w_ukt = jnp.transpose(ukv[..., :QK_NOPE], (2, 1, 0)).reshape(N_HEADS * QK_NOPE, KV_LORA).astype(BF16)
    pad_g = lambda g: jnp.pad(g, (0, SLOT - QK_HEAD))[None, :]
    gq = pad_g(q_norm_g) * (ATTN_SCALE * LOG2_E)
    gk = pad_g(k_norm_g)
    mixer = (norm_mix_g[None, :], w_in_p, q_lora_g[None, :], w_uq_p, w_uq_r, kv_lora_g[None, :],
             w_uk_p, w_uv_slots, gq, gk, conv_w, conv_b[None, :], conv_out_g[None, :])
    return mixer, w_ukt, w_uv


def _rope_slot_tables(pos):
    inv_freq = ROPE_THETA ** (-(np.arange(0, QK_ROPE, 2, dtype=np.float64) / QK_ROPE))
    ang = pos.astype(np.float64)[:, None] * inv_freq[None, :]
    n = pos.shape[0]
    cos2 = np.concatenate([np.cos(ang)] * 2, axis=1)
    sin2 = np.concatenate([np.sin(ang)] * 2, axis=1)
    cos_t = np.concatenate([np.ones((n, QK_NOPE)), cos2, np.zeros((n, SLOT - QK_HEAD))], axis=1)
    sin_t = np.pad(sin2, ((0, 0), (QK_NOPE, SLOT - QK_HEAD)))
    return jnp.asarray(cos_t, F32), jnp.asarray(sin_t, F32)


def kernel(x_prompt, x_sample, cache_kv_latent, cache_k_rope, state_conv, page_table, meta_tokens,
           norm_mix_g, w_in, q_lora_g, kv_lora_g, w_uq, w_ukv, q_norm_g, k_norm_g, conv_w, conv_b,
           attn_out_g, conv_out_g, w_o, norm_ffn_g, w_up, w_down):
    depth = w_in.shape[0]
    assert depth == 1, "the prompt and sample streams are chained for a single layer"
    nb, seq, _ = x_prompt.shape
    nseq, dec, _ = x_sample.shape
    past = page_table.shape[1] * PAGE_SIZE
    l = 0

    mixer_w, w_ukt, w_uv = _layer_weights(w_in[l], q_lora_g[l], kv_lora_g[l], w_uq[l], w_ukv[l],
                                          q_norm_g[l], k_norm_g[l], conv_w[l], conv_b[l],
                                          conv_out_g[l], norm_mix_g[l])
    ffn_w = (attn_out_g[l][None, :], w_o[l].astype(BF16), norm_ffn_g[l][None, :],
             w_up[l].astype(BF16), w_down[l].astype(BF16))

    ns = nseq * dec
    xs_rows = jnp.concatenate([x_sample.reshape(ns, D_MODEL), meta_tokens.astype(F32)], axis=0)
    t_in_seq = np.concatenate([np.tile(np.arange(dec), nseq), np.arange(N_META)])
    pos_s = np.concatenate([np.tile(past + np.arange(dec), nseq), np.arange(N_META)])
    cos_s, sin_s = _rope_slot_tables(pos_s)
    m1 = jnp.asarray((t_in_seq >= 1)[:, None], F32)
    m2 = jnp.asarray((t_in_seq >= 2)[:, None], F32)
    st = state_conv[l].astype(F32)
    zrow = jnp.zeros((nseq, 1, CONV_CH), F32)
    s1 = jnp.concatenate([st[:, 1:2], zrow, zrow, zrow], axis=1)[:, :dec]
    s2 = jnp.concatenate([st[:, 0:1], st[:, 1:2], zrow, zrow], axis=1)[:, :dec]
    zmeta = jnp.zeros((N_META, CONV_CH), F32)
    s1 = jnp.concatenate([s1.reshape(ns, CONV_CH), zmeta], axis=0)
    s2 = jnp.concatenate([s2.reshape(ns, CONV_CH), zmeta], axis=0)
    (qk_s, qabs_s, k_s, v_s, lat_s, kpe_s, convn_s, u_s) = _mixer_call(
        xs_rows, cos_s, sin_s, (m1, m2, s1, s2), mixer_w, sample_mode=True)

    pad_meta = lambda a: jnp.pad(a[ns:], ((0, LANES - N_META), (0, 0)))
    k_meta, v_meta = pad_meta(k_s), pad_meta(v_s)
    lat_meta, kpe_meta, u_meta = lat_s[ns:], kpe_s[ns:], u_s[ns:]

    cos_p, sin_p = _rope_slot_tables(N_META + np.arange(seq))
    q_p, k_p, v_p, lat_p, kpe_p, convn_p, utail_p = _mixer_call(
        x_prompt, cos_p, sin_p, (u_meta[N_META - SUBLANES:], lat_meta), mixer_w, sample_mode=False)
    attn_p = _prompt_attn_call(q_p, k_p, v_p, k_meta, v_meta)
    y_prompt = _merge_ffn_call(x_prompt.reshape(nb * seq, D_MODEL), attn_p.reshape(nb * seq, ATTN_WIDTH),
                               convn_p.reshape(nb * seq, CONV_CH), ffn_w, ROW_TILE)
    y_prompt = y_prompt.reshape(nb, seq, D_MODEL)
    bcast = lambda a: jnp.broadcast_to(a[None], (nb,) + a.shape)
    new_lat_prompt = lat_p[None]
    new_kpe_prompt = jnp.concatenate([bcast(kpe_meta), kpe_p], axis=1)[None]
    new_conv_prompt = utail_p[:, -1, SUBLANES - (CONV_W - 1):][None]

    nq = dec * N_HEADS
    qabs = qabs_s[:ns].reshape(nseq, nq, KV_LORA)
    qpe = qk_s[:ns].reshape(ns, N_HEADS, SLOT)[:, :, QK_NOPE:QK_HEAD].reshape(nseq, nq, QK_ROPE)
    assert dec <= TAIL_KEYS
    pad_tail = lambda a: jnp.pad(a[:ns].reshape(nseq, dec, -1), ((0, 0), (0, TAIL_KEYS - dec), (0, 0)))
    attn_s = _sample_attn_call(page_table, qabs, qpe, w_ukt, w_uv, pad_tail(lat_s),
                               jnp.swapaxes(pad_tail(kpe_s), 1, 2),
                               cache_kv_latent[l], jnp.swapaxes(cache_k_rope[l], 1, 2))
    y_sample = _merge_ffn_call(x_sample.reshape(ns, D_MODEL), attn_s.reshape(ns, ATTN_WIDTH),
                               convn_s[:ns], ffn_w, ns)
    y_sample = y_sample.reshape(nseq, dec, D_MODEL)
    new_lat_sample = lat_s[:ns].reshape(nseq, dec, KV_LORA)[None]
    new_kpe_sample = kpe_s[:ns].reshape(nseq, dec, QK_ROPE)[None]
    us = jnp.concatenate([st, u_s[:ns].reshape(nseq, dec, CONV_CH)], axis=1)
    new_conv_sample = us[:, -(CONV_W - 1):][None]

    return (y_prompt, y_sample, new_lat_prompt, new_kpe_prompt, new_conv_prompt,
            new_lat_sample, new_kpe_sample, new_conv_sample)
```

```python
import functools

import jax
import jax.numpy as jnp
import numpy as np
from jax import lax
from jax.experimental import pallas as pl
from jax.experimental.pallas import tpu as pltpu

D_MODEL = 1024
N_META = 16
N_HEADS = 8
QK_NOPE = 64
QK_ROPE = 32
V_HEAD = 64
QK_HEAD = QK_NOPE + QK_ROPE
Q_LORA = 384
KV_LORA = 256
ATTN_WIDTH = N_HEADS * V_HEAD
CONV_CH = D_MODEL - ATTN_WIDTH
CONV_W = 3
D_FF = 4 * D_MODEL
ROPE_THETA = 10000.0
EPS = 1e-6
PAGE_SIZE = 128
ATTN_SCALE = QK_HEAD ** -0.5
LOG2_E = 1.4426950408889634

LANES = 128
SUBLANES = 8
SLOT = LANES
HEADS_W = N_HEADS * SLOT
HALF_ROPE = QK_ROPE // 2

C_CQ = 0
C_CKV = C_CQ + Q_LORA
C_GB = C_CKV + KV_LORA
C_GC = C_GB + CONV_CH
C_HC = C_GC + CONV_CH
C_KPE = C_HC + CONV_CH
C_KPR = C_KPE + SLOT
IN_W = C_KPR + SLOT

ROW_TILE = 512
ATT_TQ = 256
ATT_TK = 256
ATT_TK_WIDE = 2048
ATT_LOOKAHEAD = 4
FF_CHUNK = 1024
CHUNK_PAGES = 8
CHUNK_LOOKAHEAD = 2
DMA_GROUP = 8
DMA_ISSUE_PAGES = 2
TAIL_KEYS = PAGE_SIZE
VMEM_LIMIT = 52 * 1024 * 1024

BF16 = jnp.bfloat16
F32 = jnp.float32

_NT = (((1,), (1,)), ((), ()))


def _dot(a, b):
    return jnp.dot(a, b, preferred_element_type=F32)


def _dot_nt(a, b):
    return lax.dot_general(a, b, _NT, preferred_element_type=F32)


def _rms(x, g):
    return x * lax.rsqrt(jnp.mean(x * x, axis=-1, keepdims=True) + EPS) * g


def _const_spec(shape):
    nd = len(shape)
    return pl.BlockSpec(shape, lambda *_: (0,) * nd, pipeline_mode=pl.Buffered(1))


def _mixer_kernel(*refs, rows, sample_mode):
    it = iter(refs)
    x_ref, cos_ref, sin_ref = next(it), next(it), next(it)
    if sample_mode:
        m1_ref, m2_ref, s1_ref, s2_ref = next(it), next(it), next(it), next(it)
    else:
        tail_in_ref, lat_meta_ref = next(it), next(it)
    (g_mix_ref, w_in_ref, g_ql_ref, w_uq_ref, w_uqr_ref, g_kvl_ref, w_uk_ref, w_uv_ref,
     gq_ref, gk_ref, cw_ref, cb_ref, g_conv_ref) = (next(it) for _ in range(13))
    if sample_mode:
        (qk_ref, qabs_ref, k_ref, v_ref, lat_ref, kpe_ref, convn_ref, u_ref) = (next(it) for _ in range(8))
    else:
        (q_ref, k_ref, v_ref, lat_hbm, kpe_ref, convn_ref, utail_ref) = (next(it) for _ in range(7))
    ubuf = next(it)

    if sample_mode:
        ubuf[0:SUBLANES, :] = jnp.zeros((SUBLANES, CONV_CH), F32)
    else:
        lat_stage, lat_sems = next(it), next(it)
        b = pl.program_id(0)
        t = pl.program_id(1)
        lat_copy = pltpu.make_async_copy(
            lat_stage, lat_hbm.at[b, pl.ds(N_META + t * rows, rows)], lat_sems.at[0])
        meta_copy = pltpu.make_async_copy(lat_meta_ref, lat_hbm.at[b, pl.ds(0, N_META)], lat_sems.at[1])
        first_step = jnp.logical_and(b == 0, t == 0)
        last_step = jnp.logical_and(b == pl.num_programs(0) - 1, t == pl.num_programs(1) - 1)

        @pl.when(jnp.logical_not(first_step))
        def _():
            lat_copy.wait()

        @pl.when(t == 0)
        def _():
            ubuf[0:SUBLANES, :] = tail_in_ref[...]
            meta_copy.start()

        @pl.when(t != 0)
        def _():
            ubuf[0:SUBLANES, :] = ubuf[rows:rows + SUBLANES, :]

    hn = _rms(x_ref[...], g_mix_ref[...]).astype(BF16)
    cos = cos_ref[...]
    sin = sin_ref[...]

    cq = _dot(hn, w_in_ref[:, C_CQ:C_CKV])
    ckv = _dot(hn, w_in_ref[:, C_CKV:C_GB])
    zk = _dot(hn, w_in_ref[:, C_KPE:IN_W])
    gc = _dot(hn, w_in_ref[:, C_GC:C_HC])
    cqn = _rms(cq, g_ql_ref[...]).astype(BF16)
    q_pairs = []
    for hp in range(N_HEADS // 2):
        sl2 = slice(2 * hp * SLOT, (2 * hp + 2) * SLOT)
        q_pairs.append((_dot(cqn, w_uq_ref[:, sl2]), _dot(cqn, w_uqr_ref[:, sl2])))
    lat = _rms(ckv, g_kvl_ref[...])
    lat_b = lat.astype(BF16)
    k_pairs = [_dot(lat_b, w_uk_ref[:, 2 * hp * SLOT:(2 * hp + 2) * SLOT]) for hp in range(N_HEADS // 2)]
    v_slots = _dot(lat_b, w_uv_ref[...])
    hc = _dot(hn, w_in_ref[:, C_HC:C_KPE])
    gb = _dot(hn, w_in_ref[:, C_GB:C_GC])

    gq = gq_ref[...]
    gk = gk_ref[...]
    for h in range(N_HEADS):
        sl = slice(h * SLOT, (h + 1) * SLOT)
        q2, qr2 = q_pairs[h // 2]
        own = slice((h % 2) * SLOT, (h % 2 + 1) * SLOT)
        qh = q2[:, own] * cos + qr2[:, own] * sin
        ss = jnp.sum(qh * qh, axis=-1, keepdims=True)
        qh = qh * lax.rsqrt(ss * (1.0 / QK_HEAD) + EPS) * gq
        if sample_mode:
            qkh = (qh * gk).astype(BF16)
            qk_ref[:, sl] = qkh
            qabs_ref[:, h * KV_LORA:(h + 1) * KV_LORA] = _dot_nt(qkh, w_uk_ref[:, sl]).astype(BF16)
        else:
            q_ref[:, sl] = qh.astype(BF16)

    if sample_mode:
        lat_ref[...] = lat
    else:
        lat_stage[...] = lat
    krot = zk[:, :SLOT] * cos + zk[:, SLOT:] * sin
    kpe_ref[...] = krot[:, QK_NOPE:QK_HEAD]
    ss_rot = jnp.sum(krot * krot, axis=-1, keepdims=True)
    for h in range(N_HEADS):
        sl = slice(h * SLOT, (h + 1) * SLOT)
        kn = k_pairs[h // 2][:, (h % 2) * SLOT:(h % 2 + 1) * SLOT]
        ss = jnp.sum(kn * kn, axis=-1, keepdims=True) + ss_rot
        kh = (kn + krot) * lax.rsqrt(ss * (1.0 / QK_HEAD) + EPS) * gk
        k_ref[:, sl] = kh.astype(BF16)
    vlane = lax.broadcasted_iota(jnp.int32, (1, HEADS_W), 1)
    ones_half = ((vlane // V_HEAD) % 2 != (vlane // SLOT) % 2).astype(F32)
    v_ref[...] = (v_slots + ones_half).astype(BF16)

    u = gc * hc
    ubuf[SUBLANES:SUBLANES + rows, :] = u
    u1 = ubuf[SUBLANES - 1:SUBLANES - 1 + rows, :]
    u2 = ubuf[SUBLANES - 2:SUBLANES - 2 + rows, :]
    if sample_mode:
        u1 = u1 * m1_ref[...] + s1_ref[...]
        u2 = u2 * m2_ref[...] + s2_ref[...]
        u_ref[...] = u
    else:
        utail_ref[...] = u[rows - SUBLANES:, :]
    y = cb_ref[...] + u2 * cw_ref[0:1, :] + u1 * cw_ref[1:2, :] + u * cw_ref[2:3, :]
    convn_ref[...] = _rms(gb * y, g_conv_ref[...]).astype(BF16)
    if not sample_mode:
        lat_copy.start()

        @pl.when(t == 0)
        def _():
            meta_copy.wait()

        @pl.when(last_step)
        def _():
            lat_copy.wait()


def _mixer_call(x, cos_t, sin_t, conv_in, weights, *, sample_mode):
    wspecs = [_const_spec(w.shape) for w in weights]
    if sample_mode:
        rows = x.shape[0]
        grid = (1,)
        row = lambda w: pl.BlockSpec((rows, w), lambda i: (0, 0))
        m1, m2, s1, s2 = conv_in
        in_specs = [row(D_MODEL), row(SLOT), row(SLOT), row(1), row(1), row(CONV_CH), row(CONV_CH)]
        args = [x, cos_t, sin_t, m1, m2, s1, s2]
        widths = [(HEADS_W, BF16), (N_HEADS * KV_LORA, BF16), (HEADS_W, BF16), (HEADS_W, BF16),
                  (KV_LORA, F32), (QK_ROPE, F32), (CONV_CH, BF16), (CONV_CH, F32)]
        out_shape = [jax.ShapeDtypeStruct((rows, w), d) for w, d in widths]
        out_specs = [row(w) for w, _ in widths]
        sem = ("arbitrary",)
        scratch = []
    else:
        nb, seq, _ = x.shape
        rows = ROW_TILE
        nt = seq // rows
        grid = (nb, nt)
        row3 = lambda w: pl.BlockSpec((None, rows, w), lambda b, t: (b, t, 0))
        tab = pl.BlockSpec((rows, SLOT), lambda b, t: (t, 0))
        u_tail, lat_meta = conv_in
        in_specs = [row3(D_MODEL), tab, tab, _const_spec(u_tail.shape), _const_spec(lat_meta.shape)]
        args = [x, cos_t, sin_t, u_tail, lat_meta]
        widths = [(HEADS_W, BF16), (HEADS_W, BF16), (HEADS_W, BF16), (KV_LORA, F32),
                  (QK_ROPE, F32), (CONV_CH, BF16)]
        out_shape = [jax.ShapeDtypeStruct((nb, seq, w), d) for w, d in widths]
        out_shape.append(jax.ShapeDtypeStruct((nb, nt, SUBLANES, CONV_CH), F32))
        out_specs = [row3(w) for w, _ in widths]
        out_specs.append(pl.BlockSpec((None, None, SUBLANES, CONV_CH), lambda b, t: (b, t, 0, 0)))
        out_shape[3] = jax.ShapeDtypeStruct((nb, N_META + seq, KV_LORA), F32)
        out_specs[3] = pl.BlockSpec(memory_space=pl.ANY)
        sem = ("arbitrary", "arbitrary")
        scratch = [pltpu.VMEM((rows, KV_LORA), F32), pltpu.SemaphoreType.DMA((2,))]
    return pl.pallas_call(
        functools.partial(_mixer_kernel, rows=rows, sample_mode=sample_mode),
        grid=grid,
        in_specs=in_specs + wspecs,
        out_specs=out_specs,
        out_shape=out_shape,
        scratch_shapes=[pltpu.VMEM((rows + 2 * SUBLANES, CONV_CH), F32)] + scratch,
        compiler_params=pltpu.CompilerParams(dimension_semantics=sem, vmem_limit_bytes=VMEM_LIMIT),
        name="mixer_sample" if sample_mode else "mixer_prompt",
    )(*args, *weights)


def _softmax_step(s, state, pv):
    m_old, l_old, acc_old = state
    n = s.shape[1] // LANES
    m_new = jnp.maximum(m_old, jnp.max(s, axis=-1, keepdims=True))
    alpha = jnp.exp2(m_old - m_new)
    ps = [jnp.exp2(s[:, c * LANES:(c + 1) * LANES] - m_new) for c in range(n)]
    l_new = None if l_old is None else alpha * l_old + functools.reduce(lambda a, b: a + b, ps)
    p = ps[0] if n == 1 else jnp.concatenate(ps, axis=1)
    w = acc_old.shape[-1] // LANES
    alpha_w = alpha if w == 1 else jnp.concatenate([alpha] * w, axis=1)
    return m_new, l_new, alpha_w * acc_old + pv(p.astype(BF16))


def _prompt_attn_kernel(q_ref, k_ref, v_ref, km_ref, vm_ref, o_ref, m_sc, acc_sc):
    seq = q_ref.shape[0]
    half = ATT_TQ // 2
    lane = lax.broadcasted_iota(jnp.int32, (ATT_TQ, SLOT), 1)
    m_sc[...] = jnp.full(m_sc.shape, -jnp.inf, F32)
    acc_sc[...] = jnp.zeros(acc_sc.shape, F32)

    meta = lane < N_META
    col_h = lax.broadcasted_iota(jnp.int32, (half, half), 1)
    row_h = lax.broadcasted_iota(jnp.int32, (half, half), 0)
    col_f = lax.broadcasted_iota(jnp.int32, (half, ATT_TK), 1)
    row_f = lax.broadcasted_iota(jnp.int32, (half, ATT_TK), 0)

    steps = []
    for qi in range(seq // ATT_TQ):
        for hh in range(2):
            steps.append((qi, hh, 0, ATT_TQ, None, meta))
        d0 = qi * ATT_TK
        for k0 in range(0, d0, ATT_TK_WIDE):
            for hh in range(2):
                steps.append((qi, hh, 0, ATT_TQ, slice(k0, min(k0 + ATT_TK_WIDE, d0)), None))
        for hh in range(2):
            steps.append((qi, hh, 0, half, slice(d0, d0 + half), col_h <= row_h))
            steps.append((qi, hh, half, half, slice(d0, d0 + ATT_TK), col_f <= row_f + half))
        steps.append((qi,))

    def scores(qi, hh, r0, nr, ks, mask):
        sl = slice(hh * SLOT, (hh + 1) * SLOT)
        k_blk = km_ref[:, sl] if ks is None else k_ref[ks, sl]
        s = _dot_nt(q_ref[pl.ds(qi * ATT_TQ + r0, nr), sl], k_blk)
        return s if mask is None else jnp.where(mask, s, -jnp.inf)

    def update(s, qi, hh, r0, nr, ks, mask):
        rows = pl.ds(r0, nr)
        sl = slice(hh * SLOT, (hh + 1) * SLOT)
        v_blk = vm_ref[:, sl] if ks is None else v_ref[ks, sl]
        m_ref, acc_ref = m_sc.at[qi, hh, rows], acc_sc.at[qi, hh, rows]
        m_ref[...], _, acc_ref[...] = _softmax_step(s, (m_ref[...], None, acc_ref[...]),
                                                    lambda p: _dot(p, v_blk))

    def finalize(qi):
        a, b = acc_sc[qi, 0], acc_sc[qi, 1]
        sums = pltpu.roll(jnp.where(lane < V_HEAD, b, a), V_HEAD, 1)
        o_ref[qi * ATT_TQ:(qi + 1) * ATT_TQ, :] = (
            jnp.where(lane < V_HEAD, a, b) / sums).astype(o_ref.dtype)

    pending = []
    for st in steps:
        pending.append((st, scores(*st) if len(st) > 1 else None))
        if len(pending) > ATT_LOOKAHEAD:
            d, s = pending.pop(0)
            finalize(*d) if s is None else update(s, *d)
    for d, s in pending:
        finalize(*d) if s is None else update(s, *d)


def _prompt_attn_call(q, k, v, k_meta, v_meta):
    nb, seq, _ = q.shape
    assert ATT_TQ == ATT_TK and seq % ATT_TQ == 0
    grid = (nb, N_HEADS // 2)
    return pl.pallas_call(
        _prompt_attn_kernel,
        grid=grid,
        in_specs=[
            pl.BlockSpec((None, seq, 2 * SLOT), lambda b, p: (b, 0, p)),
            pl.BlockSpec((None, seq, 2 * SLOT), lambda b, p: (b, 0, p)),
            pl.BlockSpec((None, seq, 2 * SLOT), lambda b, p: (b, 0, p)),
            pl.BlockSpec((LANES, 2 * SLOT), lambda b, p: (0, p)),
            pl.BlockSpec((LANES, 2 * SLOT), lambda b, p: (0, p)),
        ],
        out_specs=pl.BlockSpec((None, seq, SLOT), lambda b, p: (b, 0, p)),
        out_shape=jax.ShapeDtypeStruct((nb, seq, ATTN_WIDTH), BF16),
        scratch_shapes=[pltpu.VMEM((seq // ATT_TQ, 2, ATT_TQ, LANES), F32)] * 2,
        compiler_params=pltpu.CompilerParams(
            dimension_semantics=("arbitrary", "arbitrary"),
            vmem_limit_bytes=VMEM_LIMIT),
        name="prompt_attn",
    )(q, k, v, k_meta, v_meta)


def _sample_attn_kernel(pt_ref, qabs_ref, qpe_ref, wukt_ref, wuv_ref, tlat_ref, tkpe_ref,
                        lat_hbm, kpe_hbm, o_ref,
                        latb_sc, lhs_sc, lat_buf, kpe_buf, sems):
    n_pages = lat_buf.shape[1]
    chunk_pages = (CHUNK_PAGES,) * (n_pages // CHUNK_PAGES)
    b = pl.program_id(0)
    last = pl.num_programs(0) - 1
    slot = b % 2
    nq = qabs_ref.shape[0]
    reps = nq // N_HEADS
    nkn = N_HEADS * QK_NOPE
    qpe = qpe_ref[...]

    ngroup = lat_buf.shape[1] // DMA_GROUP

    def group_copies(bb, sl, gg):
        copies = []
        for j in range(gg * DMA_GROUP, (gg + 1) * DMA_GROUP):
            page = pt_ref[bb, j]
            copies.append(pltpu.make_async_copy(lat_hbm.at[page], lat_buf.at[sl, j], sems.at[0, sl]))
            copies.append(pltpu.make_async_copy(kpe_hbm.at[page], kpe_buf.at[sl, j], sems.at[1, sl]))
        return copies

    @pl.when(b == 0)
    def _():
        lhs_sc[0:nkn, :] = wukt_ref[...]
        for gg in range(ngroup):
            for cp in group_copies(0, 0, gg):
                cp.start()

    nxt = jnp.minimum(b + 1, last)

    lhs_sc[nkn:nkn + nq, :] = qabs_ref[...]
    state = (jnp.full((nq, LANES), -jnp.inf, F32), jnp.zeros((nq, LANES), F32),
             jnp.zeros((nq, KV_LORA), F32))

    def scores(lat_b, kpe_t):
        nk = lat_b.shape[0]
        full = _dot_nt(lhs_sc[...], lat_b)
        knt = full[:nkn]
        nsum = jnp.sum((knt * knt).reshape(QK_NOPE, N_HEADS, nk), axis=0)
        rsum = jnp.sum(kpe_t * kpe_t, axis=0, keepdims=True)
        r = lax.rsqrt((nsum + rsum) * (1.0 / QK_HEAD) + EPS)
        s = full[nkn:] + _dot(qpe, kpe_t.astype(BF16))
        return s * jnp.concatenate([r] * reps, axis=0)

    for gg in range(ngroup):
        for cp in group_copies(b, slot, gg):
            cp.wait()

    tail_b = tlat_ref[...].astype(BF16)
    s = scores(tail_b, tkpe_ref[...])
    key = lax.broadcasted_iota(jnp.int32, s.shape, 1)
    qt = lax.broadcasted_iota(jnp.int32, s.shape, 0) // N_HEADS
    s_tail = jnp.where(key <= qt, s, -jnp.inf)

    def chunk_scores(p0, npg):
        parts = [s_tail] if p0 == 0 else []
        for j in range(p0, p0 + npg, 2):
            for jj in (j, j + 1):
                latb_sc[jj * PAGE_SIZE:(jj + 1) * PAGE_SIZE, :] = lat_buf[slot, jj].astype(BF16)
            kpe_t = jnp.concatenate([kpe_buf[slot, j], kpe_buf[slot, j + 1]], axis=1)
            parts.append(scores(latb_sc[j * PAGE_SIZE:(j + 2) * PAGE_SIZE, :], kpe_t))
            if (j + 2) % DMA_ISSUE_PAGES == 0 and (j + 2) // DMA_ISSUE_PAGES <= ngroup:
                for cp in group_copies(nxt, 1 - slot, (j + 2) // DMA_ISSUE_PAGES - 1):
                    cp.start()
        return jnp.concatenate(parts, axis=1)

    def chunk_update(state, s, p0, npg):
        rows = slice(p0 * PAGE_SIZE, (p0 + npg) * PAGE_SIZE)
        if p0 == 0:
            pv = lambda p: _dot(p[:, :TAIL_KEYS], tail_b) + _dot(p[:, TAIL_KEYS:], latb_sc[rows, :])
        else:
            pv = lambda p: _dot(p, latb_sc[rows, :])
        return _softmax_step(s, state, pv)

    starts = [sum(chunk_pages[:i]) for i in range(len(chunk_pages))]
    pending = []
    for p0, npg in zip(starts, chunk_pages):
        pending.append((chunk_scores(p0, npg), p0, npg))
        if len(pending) > CHUNK_LOOKAHEAD:
            state = chunk_update(state, *pending.pop(0))
    for item in pending:
        state = chunk_update(state, *item)

    _, l_fin, acc_fin = state
    o_lat = (acc_fin / jnp.sum(l_fin, axis=-1, keepdims=True)).astype(BF16)
    full = _dot(o_lat, wuv_ref[...])
    row_h = lax.broadcasted_iota(jnp.int32, full.shape, 0) % N_HEADS
    col_h = lax.broadcasted_iota(jnp.int32, full.shape, 1) // V_HEAD
    own = jnp.where(row_h == col_h, full, 0.0)
    o_ref[...] = jnp.sum(own.reshape(reps, N_HEADS, ATTN_WIDTH), axis=1)

    @pl.when(b == last)
    def _():
        for gg in range(ngroup):
            for cp in group_copies(last, 1 - slot, gg):
                cp.wait()


def _sample_attn_call(page_table, qabs, qpe, w_ukt, w_uv, tail_lat, tail_kpe_t, cache_lat, cache_kpe_t):
    nseq, n_pages = page_table.shape
    nq = qabs.shape[1]
    assert n_pages % DMA_GROUP == 0 and DMA_GROUP % 2 == 0 and DMA_ISSUE_PAGES % 2 == 0
    assert DMA_ISSUE_PAGES <= DMA_GROUP
    assert n_pages % CHUNK_PAGES == 0 and CHUNK_PAGES % 2 == 0
    grid = (nseq,)

    per_seq = lambda rows, width: pl.BlockSpec((None, rows, width), lambda b, pt: (b, 0, 0))
    whole = lambda shape: pl.BlockSpec(shape, lambda b, pt: (0,) * len(shape),
                                       pipeline_mode=pl.Buffered(1))
    in_specs = [per_seq(nq, KV_LORA), per_seq(nq, QK_ROPE), whole(w_ukt.shape), whole(w_uv.shape),
                per_seq(TAIL_KEYS, KV_LORA), per_seq(QK_ROPE, TAIL_KEYS),
                pl.BlockSpec(memory_space=pl.ANY), pl.BlockSpec(memory_space=pl.ANY)]
    reps = nq // N_HEADS
    return pl.pallas_call(
        _sample_attn_kernel,
        grid_spec=pltpu.PrefetchScalarGridSpec(
            num_scalar_prefetch=1,
            grid=grid,
            in_specs=in_specs,
            out_specs=pl.BlockSpec((None, reps, ATTN_WIDTH), lambda b, pt: (b, 0, 0)),
            scratch_shapes=[pltpu.VMEM((n_pages * PAGE_SIZE, KV_LORA), BF16),
                            pltpu.VMEM((N_HEADS * QK_NOPE + nq, KV_LORA), BF16),
                            pltpu.VMEM((2, n_pages, PAGE_SIZE, KV_LORA), F32),
                            pltpu.VMEM((2, n_pages, QK_ROPE, PAGE_SIZE), F32),
                            pltpu.SemaphoreType.DMA((2, 2))],
        ),
        out_shape=jax.ShapeDtypeStruct((nseq, reps, ATTN_WIDTH), F32),
        compiler_params=pltpu.CompilerParams(dimension_semantics=("arbitrary",),
                                             vmem_limit_bytes=VMEM_LIMIT),
        name="sample_attn",
    )(page_table, qabs, qpe, w_ukt, w_uv, tail_lat, tail_kpe_t, cache_lat, cache_kpe_t)


def _merge_ffn_kernel(x_ref, attn_ref, convn_ref, g_attn_ref, w_o_ref, g_ffn_ref, w_up_ref,
                      w_down_ref, y_ref):
    an = _rms(attn_ref[...].astype(F32), g_attn_ref[...]).astype(BF16)
    x1 = x_ref[...] + (_dot(an, w_o_ref[0:ATTN_WIDTH, :]) + _dot(convn_ref[...], w_o_ref[ATTN_WIDTH:, :]))
    hf = _rms(x1, g_ffn_ref[...]).astype(BF16)
    ffn = None
    for c in range(D_FF // FF_CHUNK):
        cs = slice(c * FF_CHUNK, (c + 1) * FF_CHUNK)
        up = jnp.maximum(_dot(hf, w_up_ref[:, cs]), 0.0)
        part = _dot((up * up).astype(BF16), w_down_ref[cs, :])
        ffn = part if ffn is None else ffn + part
    y_ref[...] = x1 + ffn


def _merge_ffn_call(x, attn, convn, weights, rows):
    n = x.shape[0]
    row = lambda w: pl.BlockSpec((rows, w), lambda i: (i, 0))
    return pl.pallas_call(
        _merge_ffn_kernel,
        grid=(n // rows,),
        in_specs=[row(D_MODEL), row(ATTN_WIDTH), row(CONV_CH)] + [_const_spec(w.shape) for w in weights],
        out_specs=row(D_MODEL),
        out_shape=jax.ShapeDtypeStruct((n, D_MODEL), F32),
        compiler_params=pltpu.CompilerParams(dimension_semantics=("arbitrary",),
                                             vmem_limit_bytes=VMEM_LIMIT),
        name="merge_ffn",
    )(x, attn, convn, *weights)


def _slots(w, width):
    k = w.shape[0]
    w = w.reshape(k, N_HEADS, width)
    return jnp.pad(w, ((0, 0), (0, 0), (0, SLOT - width))).reshape(k, HEADS_W)


def _rot_partner(w):
    return jnp.concatenate([-w[..., HALF_ROPE:], w[..., :HALF_ROPE]], axis=-1)


def _rope_slot(w):
    return jnp.pad(w, ((0, 0), (QK_NOPE, SLOT - QK_HEAD)))


def _layer_weights(w_in, q_lora_g, kv_lora_g, w_uq, w_ukv, q_norm_g, k_norm_g, conv_w, conv_b,
                   conv_out_g, norm_mix_g):
    o1 = Q_LORA
    o2 = o1 + KV_LORA
    o3 = o2 + QK_ROPE
    w_kpe = w_in[:, o2:o3]
    w_in_p = jnp.concatenate(
        [w_in[:, :o2], w_in[:, o3:], _rope_slot(w_kpe), _rope_slot(_rot_partner(w_kpe))],
        axis=1).astype(BF16)
    uq = w_uq.reshape(Q_LORA, N_HEADS, QK_HEAD)
    w_uq_p = _slots(w_uq, QK_HEAD).astype(BF16)
    uq_rot = jnp.pad(_rot_partner(uq[..., QK_NOPE:]), ((0, 0), (0, 0), (QK_NOPE, SLOT - QK_HEAD)))
    w_uq_r = uq_rot.reshape(Q_LORA, HEADS_W).astype(BF16)
    ukv = w_ukv.reshape(KV_LORA, N_HEADS, QK_NOPE + V_HEAD)
    w_uk_p = _slots(ukv[..., :QK_NOPE].reshape(KV_LORA, N_HEADS * QK_NOPE), QK_NOPE).astype(BF16)
    uv = ukv[..., QK_NOPE:]
    w_uv = uv.reshape(KV_LORA, ATTN_WIDTH).astype(BF16)
    odd_head = (np.arange(N_HEADS) % 2 == 1)[None, :, None]
    w_uv_slots = jnp.where(odd_head, jnp.concatenate([jnp.zeros_like(uv), uv], axis=-1),
                           jnp.concatenate([uv, jnp.zeros_like(uv)], axis=-1))
    w_uv_slots = w_uv_slots.reshape(KV_LORA, HEADS_W).astype(BF16)
    w_ukt = jnp.transpose(ukv[..., :QK_NOPE], (2, 1, 0)).reshape(N_HEADS * QK_NOPE, KV_LORA).astype(BF16)
    pad_g = lambda g: jnp.pad(g, (0, SLOT - QK_HEAD))[None, :]
    gq = pad_g(q_norm_g) * (ATTN_SCALE * LOG2_E)
    gk = pad_g(k_norm_g)
    mixer = (norm_mix_g[None, :], w_in_p, q_lora_g[None, :], w_uq_p, w_uq_r, kv_lora_g[None, :],
             w_uk_p, w_uv_slots, gq, gk, conv_w, conv_b[None, :], conv_out_g[None, :])
    return mixer, w_ukt, w_uv


def _rope_slot_tables(pos):
    inv_freq = ROPE_THETA ** (-(np.arange(0, QK_ROPE, 2, dtype=np.float64) / QK_ROPE))
    ang = pos.astype(np.float64)[:, None] * inv_freq[None, :]
    n = pos.shape[0]
    cos2 = np.concatenate([np.cos(ang)] * 2, axis=1)
    sin2 = np.concatenate([np.sin(ang)] * 2, axis=1)
    cos_t = np.concatenate([np.ones((n, QK_NOPE)), cos2, np.zeros((n, SLOT - QK_HEAD))], axis=1)
    sin_t = np.pad(sin2, ((0, 0), (QK_NOPE, SLOT - QK_HEAD)))
    return jnp.asarray(cos_t, F32), jnp.asarray(sin_t, F32)


def kernel(x_prompt, x_sample, cache_kv_latent, cache_k_rope, state_conv, page_table, meta_tokens,
           norm_mix_g, w_in, q_lora_g, kv_lora_g, w_uq, w_ukv, q_norm_g, k_norm_g, conv_w, conv_b,
           attn_out_g, conv_out_g, w_o, norm_ffn_g, w_up, w_down):
    depth = w_in.shape[0]
    assert depth == 1, "the prompt and sample streams are chained for a single layer"
    nb, seq, _ = x_prompt.shape
    nseq, dec, _ = x_sample.shape
    past = page_table.shape[1] * PAGE_SIZE
    l = 0

    mixer_w, w_ukt, w_uv = _layer_weights(w_in[l], q_lora_g[l], kv_lora_g[l], w_uq[l], w_ukv[l],
                                          q_norm_g[l], k_norm_g[l], conv_w[l], conv_b[l],
                                          conv_out_g[l], norm_mix_g[l])
    ffn_w = (attn_out_g[l][None, :], w_o[l].astype(BF16), norm_ffn_g[l][None, :],
             w_up[l].astype(BF16), w_down[l].astype(BF16))

    ns = nseq * dec
    xs_rows = jnp.concatenate([x_sample.reshape(ns, D_MODEL), meta_tokens.astype(F32)], axis=0)
    t_in_seq = np.concatenate([np.tile(np.arange(dec), nseq), np.arange(N_META)])
    pos_s = np.concatenate([np.tile(past + np.arange(dec), nseq), np.arange(N_META)])
    cos_s, sin_s = _rope_slot_tables(pos_s)
    m1 = jnp.asarray((t_in_seq >= 1)[:, None], F32)
    m2 = jnp.asarray((t_in_seq >= 2)[:, None], F32)
    st = state_conv[l].astype(F32)
    zrow = jnp.zeros((nseq, 1, CONV_CH), F32)
    s1 = jnp.concatenate([st[:, 1:2], zrow, zrow, zrow], axis=1)[:, :dec]
    s2 = jnp.concatenate([st[:, 0:1], st[:, 1:2], zrow, zrow], axis=1)[:, :dec]
    zmeta = jnp.zeros((N_META, CONV_CH), F32)
    s1 = jnp.concatenate([s1.reshape(ns, CONV_CH), zmeta], axis=0)
    s2 = jnp.concatenate([s2.reshape(ns, CONV_CH), zmeta], axis=0)
    (qk_s, qabs_s, k_s, v_s, lat_s, kpe_s, convn_s, u_s) = _mixer_call(
        xs_rows, cos_s, sin_s, (m1, m2, s1, s2), mixer_w, sample_mode=True)

    pad_meta = lambda a: jnp.pad(a[ns:], ((0, LANES - N_META), (0, 0)))
    k_meta, v_meta = pad_meta(k_s), pad_meta(v_s)
    lat_meta, kpe_meta, u_meta = lat_s[ns:], kpe_s[ns:], u_s[ns:]

    cos_p, sin_p = _rope_slot_tables(N_META + np.arange(seq))
    q_p, k_p, v_p, lat_p, kpe_p, convn_p, utail_p = _mixer_call(
        x_prompt, cos_p, sin_p, (u_meta[N_META - SUBLANES:], lat_meta), mixer_w, sample_mode=False)
    attn_p = _prompt_attn_call(q_p, k_p, v_p, k_meta, v_meta)
    y_prompt = _merge_ffn_call(x_prompt.reshape(nb * seq, D_MODEL), attn_p.reshape(nb * seq, ATTN_WIDTH),
                               convn_p.reshape(nb * seq, CONV_CH), ffn_w, ROW_TILE)
    y_prompt = y_prompt.reshape(nb, seq, D_MODEL)
    bcast = lambda a: jnp.broadcast_to(a[None], (nb,) + a.shape)
    new_lat_prompt = lat_p[None]
    new_kpe_prompt = jnp.concatenate([bcast(kpe_meta), kpe_p], axis=1)[None]
    new_conv_prompt = utail_p[:, -1, SUBLANES - (CONV_W - 1):][None]

    nq = dec * N_HEADS
    qabs = qabs_s[:ns].reshape(nseq, nq, KV_LORA)
    qpe = qk_s[:ns].reshape(ns, N_HEADS, SLOT)[:, :, QK_NOPE:QK_HEAD].reshape(nseq, nq, QK_ROPE)
    assert dec <= TAIL_KEYS
    pad_tail = lambda a: jnp.pad(a[:ns].reshape(nseq, dec, -1), ((0, 0), (0, TAIL_KEYS - dec), (0, 0)))
    attn_s = _sample_attn_call(page_table, qabs, qpe, w_ukt, w_uv, pad_tail(lat_s),
                               jnp.swapaxes(pad_tail(kpe_s), 1, 2),
                               cache_kv_latent[l], jnp.swapaxes(cache_k_rope[l], 1, 2))
    y_sample = _merge_ffn_call(x_sample.reshape(ns, D_MODEL), attn_s.reshape(ns, ATTN_WIDTH),
                               convn_s[:ns], ffn_w, ns)
    y_sample = y_sample.reshape(nseq, dec, D_MODEL)
    new_lat_sample = lat_s[:ns].reshape(nseq, dec, KV_LORA)[None]
    new_kpe_sample = kpe_s[:ns].reshape(nseq, dec, QK_ROPE)[None]
    us = jnp.concatenate([st, u_s[:ns].reshape(nseq, dec, CONV_CH)], axis=1)
    new_conv_sample = us[:, -(CONV_W - 1):][None]

    return (y_prompt, y_sample, new_lat_prompt, new_kpe_prompt, new_conv_prompt,
            new_lat_sample, new_kpe_sample, new_conv_sample)
```

```python
import functools

import jax
import jax.numpy as jnp
import numpy as np
from jax import lax
from jax.experimental import pallas as pl
from jax.experimental.pallas import tpu as pltpu

D_MODEL = 1024
N_META = 16
N_HEADS = 8
QK_NOPE = 64
QK_ROPE = 32
V_HEAD = 64
QK_HEAD = QK_NOPE + QK_ROPE
Q_LORA = 384
KV_LORA = 256
ATTN_WIDTH = N_HEADS * V_HEAD
CONV_CH = D_MODEL - ATTN_WIDTH
CONV_W = 3
D_FF = 4 * D_MODEL
ROPE_THETA = 10000.0
EPS = 1e-6
PAGE_SIZE = 128
ATTN_SCALE = QK_HEAD ** -0.5
LOG2_E = 1.4426950408889634

LANES = 128
SUBLANES = 8
SLOT = LANES
HEADS_W = N_HEADS * SLOT
HALF_ROPE = QK_ROPE // 2

C_CQ = 0
C_CKV = C_CQ + Q_LORA
C_GB = C_CKV + KV_LORA
C_GC = C_GB + CONV_CH
C_HC = C_GC + CONV_CH
C_KPE = C_HC + CONV_CH
C_KPR = C_KPE + SLOT
IN_W = C_KPR + SLOT

ROW_TILE = 512
ATT_TQ = 256
ATT_TK = 256
ATT_TK_WIDE = 2048
ATT_LOOKAHEAD = 4
FF_CHUNK = 1024
CHUNK_PAGES = 8
CHUNK_LOOKAHEAD = 3
DMA_GROUP = 8
DMA_ISSUE_PAGES = 2
TAIL_KEYS = PAGE_SIZE
VMEM_LIMIT = 52 * 1024 * 1024

BF16 = jnp.bfloat16
F32 = jnp.float32

_NT = (((1,), (1,)), ((), ()))


def _dot(a, b):
    return jnp.dot(a, b, preferred_element_type=F32)


def _dot_nt(a, b):
    return lax.dot_general(a, b, _NT, preferred_element_type=F32)


def _rms(x, g):
    return x * lax.rsqrt(jnp.mean(x * x, axis=-1, keepdims=True) + EPS) * g


def _const_spec(shape):
    nd = len(shape)
    return pl.BlockSpec(shape, lambda *_: (0,) * nd, pipeline_mode=pl.Buffered(1))


def _mixer_kernel(*refs, rows, sample_mode):
    it = iter(refs)
    x_ref, cos_ref, sin_ref = next(it), next(it), next(it)
    if sample_mode:
        m1_ref, m2_ref, s1_ref, s2_ref = next(it), next(it), next(it), next(it)
    else:
        tail_in_ref, lat_meta_ref = next(it), next(it)
    (g_mix_ref, w_in_ref, g_ql_ref, w_uq_ref, w_uqr_ref, g_kvl_ref, w_uk_ref, w_uv_ref,
     gq_ref, gk_ref, cw_ref, cb_ref, g_conv_ref) = (next(it) for _ in range(13))
    if sample_mode:
        (qk_ref, qabs_ref, k_ref, v_ref, lat_ref, kpe_ref, convn_ref, u_ref) = (next(it) for _ in range(8))
    else:
        (q_ref, k_ref, v_ref, lat_hbm, kpe_ref, convn_ref, utail_ref) = (next(it) for _ in range(7))
    ubuf = next(it)

    if sample_mode:
        ubuf[0:SUBLANES, :] = jnp.zeros((SUBLANES, CONV_CH), F32)
    else:
        lat_stage, lat_sems = next(it), next(it)
        b = pl.program_id(0)
        t = pl.program_id(1)
        lat_copy = pltpu.make_async_copy(
            lat_stage, lat_hbm.at[b, pl.ds(N_META + t * rows, rows)], lat_sems.at[0])
        meta_copy = pltpu.make_async_copy(lat_meta_ref, lat_hbm.at[b, pl.ds(0, N_META)], lat_sems.at[1])
        first_step = jnp.logical_and(b == 0, t == 0)
        last_step = jnp.logical_and(b == pl.num_programs(0) - 1, t == pl.num_programs(1) - 1)

        @pl.when(jnp.logical_not(first_step))
        def _():
            lat_copy.wait()

        @pl.when(t == 0)
        def _():
            ubuf[0:SUBLANES, :] = tail_in_ref[...]
            meta_copy.start()

        @pl.when(t != 0)
        def _():
            ubuf[0:SUBLANES, :] = ubuf[rows:rows + SUBLANES, :]

    hn = _rms(x_ref[...], g_mix_ref[...]).astype(BF16)
    cos = cos_ref[...]
    sin = sin_ref[...]

    cq = _dot(hn, w_in_ref[:, C_CQ:C_CKV])
    ckv = _dot(hn, w_in_ref[:, C_CKV:C_GB])
    zk = _dot(hn, w_in_ref[:, C_KPE:IN_W])
    gc = _dot(hn, w_in_ref[:, C_GC:C_HC])
    cqn = _rms(cq, g_ql_ref[...]).astype(BF16)
    q_pairs = []
    for hp in range(N_HEADS // 2):
        sl2 = slice(2 * hp * SLOT, (2 * hp + 2) * SLOT)
        q_pairs.append((_dot(cqn, w_uq_ref[:, sl2]), _dot(cqn, w_uqr_ref[:, sl2])))
    lat = _rms(ckv, g_kvl_ref[...])
    lat_b = lat.astype(BF16)
    k_pairs = [_dot(lat_b, w_uk_ref[:, 2 * hp * SLOT:(2 * hp + 2) * SLOT]) for hp in range(N_HEADS // 2)]
    v_slots = _dot(lat_b, w_uv_ref[...])
    hc = _dot(hn, w_in_ref[:, C_HC:C_KPE])
    gb = _dot(hn, w_in_ref[:, C_GB:C_GC])

    gq = gq_ref[...]
    gk = gk_ref[...]
    for h in range(N_HEADS):
        sl = slice(h * SLOT, (h + 1) * SLOT)
        q2, qr2 = q_pairs[h // 2]
        own = slice((h % 2) * SLOT, (h % 2 + 1) * SLOT)
        qh = q2[:, own] * cos + qr2[:, own] * sin
        ss = jnp.sum(qh * qh, axis=-1, keepdims=True)
        qh = qh * lax.rsqrt(ss * (1.0 / QK_HEAD) + EPS) * gq
        if sample_mode:
            qkh = (qh * gk).astype(BF16)
            qk_ref[:, sl] = qkh
            qabs_ref[:, h * KV_LORA:(h + 1) * KV_LORA] = _dot_nt(qkh, w_uk_ref[:, sl]).astype(BF16)
        else:
            q_ref[:, sl] = qh.astype(BF16)

    if sample_mode:
        lat_ref[...] = lat
    else:
        lat_stage[...] = lat
    krot = zk[:, :SLOT] * cos + zk[:, SLOT:] * sin
    kpe_ref[...] = krot[:, QK_NOPE:QK_HEAD]
    ss_rot = jnp.sum(krot * krot, axis=-1, keepdims=True)
    for h in range(N_HEADS):
        sl = slice(h * SLOT, (h + 1) * SLOT)
        kn = k_pairs[h // 2][:, (h % 2) * SLOT:(h % 2 + 1) * SLOT]
        ss = jnp.sum(kn * kn, axis=-1, keepdims=True) + ss_rot
        kh = (kn + krot) * lax.rsqrt(ss * (1.0 / QK_HEAD) + EPS) * gk
        k_ref[:, sl] = kh.astype(BF16)
    vlane = lax.broadcasted_iota(jnp.int32, (1, HEADS_W), 1)
    ones_half = ((vlane // V_HEAD) % 2 != (vlane // SLOT) % 2).astype(F32)
    v_ref[...] = (v_slots + ones_half).astype(BF16)

    u = gc * hc
    ubuf[SUBLANES:SUBLANES + rows, :] = u
    u1 = ubuf[SUBLANES - 1:SUBLANES - 1 + rows, :]
    u2 = ubuf[SUBLANES - 2:SUBLANES - 2 + rows, :]
    if sample_mode:
        u1 = u1 * m1_ref[...] + s1_ref[...]
        u2 = u2 * m2_ref[...] + s2_ref[...]
        u_ref[...] = u
    else:
        utail_ref[...] = u[rows - SUBLANES:, :]
    y = cb_ref[...] + u2 * cw_ref[0:1, :] + u1 * cw_ref[1:2, :] + u * cw_ref[2:3, :]
    convn_ref[...] = _rms(gb * y, g_conv_ref[...]).astype(BF16)
    if not sample_mode:
        lat_copy.start()

        @pl.when(t == 0)
        def _():
            meta_copy.wait()

        @pl.when(last_step)
        def _():
            lat_copy.wait()


def _mixer_call(x, cos_t, sin_t, conv_in, weights, *, sample_mode):
    wspecs = [_const_spec(w.shape) for w in weights]
    if sample_mode:
        rows = x.shape[0]
        grid = (1,)
        row = lambda w: pl.BlockSpec((rows, w), lambda i: (0, 0))
        m1, m2, s1, s2 = conv_in
        in_specs = [row(D_MODEL), row(SLOT), row(SLOT), row(1), row(1), row(CONV_CH), row(CONV_CH)]
        args = [x, cos_t, sin_t, m1, m2, s1, s2]
        widths = [(HEADS_W, BF16), (N_HEADS * KV_LORA, BF16), (HEADS_W, BF16), (HEADS_W, BF16),
                  (KV_LORA, F32), (QK_ROPE, F32), (CONV_CH, BF16), (CONV_CH, F32)]
        out_shape = [jax.ShapeDtypeStruct((rows, w), d) for w, d in widths]
        out_specs = [row(w) for w, _ in widths]
        sem = ("arbitrary",)
        scratch = []
    else:
        nb, seq, _ = x.shape
        rows = ROW_TILE
        nt = seq // rows
        grid = (nb, nt)
        row3 = lambda w: pl.BlockSpec((None, rows, w), lambda b, t: (b, t, 0))
        tab = pl.BlockSpec((rows, SLOT), lambda b, t: (t, 0))
        u_tail, lat_meta = conv_in
        in_specs = [row3(D_MODEL), tab, tab, _const_spec(u_tail.shape), _const_spec(lat_meta.shape)]
        args = [x, cos_t, sin_t, u_tail, lat_meta]
        widths = [(HEADS_W, BF16), (HEADS_W, BF16), (HEADS_W, BF16), (KV_LORA, F32),
                  (QK_ROPE, F32), (CONV_CH, BF16)]
        out_shape = [jax.ShapeDtypeStruct((nb, seq, w), d) for w, d in widths]
        out_shape.append(jax.ShapeDtypeStruct((nb, nt, SUBLANES, CONV_CH), F32))
        out_specs = [row3(w) for w, _ in widths]
        out_specs.append(pl.BlockSpec((None, None, SUBLANES, CONV_CH), lambda b, t: (b, t, 0, 0)))
        out_shape[3] = jax.ShapeDtypeStruct((nb, N_META + seq, KV_LORA), F32)
        out_specs[3] = pl.BlockSpec(memory_space=pl.ANY)
        sem = ("arbitrary", "arbitrary")
        scratch = [pltpu.VMEM((rows, KV_LORA), F32), pltpu.SemaphoreType.DMA((2,))]
    return pl.pallas_call(
        functools.partial(_mixer_kernel, rows=rows, sample_mode=sample_mode),
        grid=grid,
        in_specs=in_specs + wspecs,
        out_specs=out_specs,
        out_shape=out_shape,
        scratch_shapes=[pltpu.VMEM((rows + 2 * SUBLANES, CONV_CH), F32)] + scratch,
        compiler_params=pltpu.CompilerParams(dimension_semantics=sem, vmem_limit_bytes=VMEM_LIMIT),
        name="mixer_sample" if sample_mode else "mixer_prompt",
    )(*args, *weights)


def _softmax_step(s, state, pv):
    m_old, l_old, acc_old = state
    n = s.shape[1] // LANES
    m_new = jnp.maximum(m_old, jnp.max(s, axis=-1, keepdims=True))
    alpha = jnp.exp2(m_old - m_new)
    ps = [jnp.exp2(s[:, c * LANES:(c + 1) * LANES] - m_new) for c in range(n)]
    l_new = None if l_old is None else alpha * l_old + functools.reduce(lambda a, b: a + b, ps)
    p = ps[0] if n == 1 else jnp.concatenate(ps, axis=1)
    w = acc_old.shape[-1] // LANES
    alpha_w = alpha if w == 1 else jnp.concatenate([alpha] * w, axis=1)
    return m_new, l_new, alpha_w * acc_old + pv(p.astype(BF16))


def _prompt_attn_kernel(q_ref, k_ref, v_ref, km_ref, vm_ref, o_ref, m_sc, acc_sc):
    seq = q_ref.shape[0]
    half = ATT_TQ // 2
    lane = lax.broadcasted_iota(jnp.int32, (ATT_TQ, SLOT), 1)
    m_sc[...] = jnp.full(m_sc.shape, -jnp.inf, F32)
    acc_sc[...] = jnp.zeros(acc_sc.shape, F32)

    meta = lane < N_META
    col_h = lax.broadcasted_iota(jnp.int32, (half, half), 1)
    row_h = lax.broadcasted_iota(jnp.int32, (half, half), 0)
    col_f = lax.broadcasted_iota(jnp.int32, (half, ATT_TK), 1)
    row_f = lax.broadcasted_iota(jnp.int32, (half, ATT_TK), 0)

    steps = []
    for qi in range(seq // ATT_TQ):
        for hh in range(2):
            steps.append((qi, hh, 0, ATT_TQ, None, meta))
        d0 = qi * ATT_TK
        for k0 in range(0, d0, ATT_TK_WIDE):
            for hh in range(2):
                steps.append((qi, hh, 0, ATT_TQ, slice(k0, min(k0 + ATT_TK_WIDE, d0)), None))
        for hh in range(2):
            steps.append((qi, hh, 0, half, slice(d0, d0 + half), col_h <= row_h))
            steps.append((qi, hh, half, half, slice(d0, d0 + ATT_TK), col_f <= row_f + half))
        steps.append((qi,))

    def scores(qi, hh, r0, nr, ks, mask):
        sl = slice(hh * SLOT, (hh + 1) * SLOT)
        k_blk = km_ref[:, sl] if ks is None else k_ref[ks, sl]
        s = _dot_nt(q_ref[pl.ds(qi * ATT_TQ + r0, nr), sl], k_blk)
        return s if mask is None else jnp.where(mask, s, -jnp.inf)

    def update(s, qi, hh, r0, nr, ks, mask):
        rows = pl.ds(r0, nr)
        sl = slice(hh * SLOT, (hh + 1) * SLOT)
        v_blk = vm_ref[:, sl] if ks is None else v_ref[ks, sl]
        m_ref, acc_ref = m_sc.at[qi, hh, rows], acc_sc.at[qi, hh, rows]
        m_ref[...], _, acc_ref[...] = _softmax_step(s, (m_ref[...], None, acc_ref[...]),
                                                    lambda p: _dot(p, v_blk))

    def finalize(qi):
        a, b = acc_sc[qi, 0], acc_sc[qi, 1]
        sums = pltpu.roll(jnp.where(lane < V_HEAD, b, a), V_HEAD, 1)
        o_ref[qi * ATT_TQ:(qi + 1) * ATT_TQ, :] = (
            jnp.where(lane < V_HEAD, a, b) / sums).astype(o_ref.dtype)

    pending = []
    for st in steps:
        pending.append((st, scores(*st) if len(st) > 1 else None))
        if len(pending) > ATT_LOOKAHEAD:
            d, s = pending.pop(0)
            finalize(*d) if s is None else update(s, *d)
    for d, s in pending:
        finalize(*d) if s is None else update(s, *d)


def _prompt_attn_call(q, k, v, k_meta, v_meta):
    nb, seq, _ = q.shape
    assert ATT_TQ == ATT_TK and seq % ATT_TQ == 0
    grid = (nb, N_HEADS // 2)
    return pl.pallas_call(
        _prompt_attn_kernel,
        grid=grid,
        in_specs=[
            pl.BlockSpec((None, seq, 2 * SLOT), lambda b, p: (b, 0, p)),
            pl.BlockSpec((None, seq, 2 * SLOT), lambda b, p: (b, 0, p)),
            pl.BlockSpec((None, seq, 2 * SLOT), lambda b, p: (b, 0, p)),
            pl.BlockSpec((LANES, 2 * SLOT), lambda b, p: (0, p)),
            pl.BlockSpec((LANES, 2 * SLOT), lambda b, p: (0, p)),
        ],
        out_specs=pl.BlockSpec((None, seq, SLOT), lambda b, p: (b, 0, p)),
        out_shape=jax.ShapeDtypeStruct((nb, seq, ATTN_WIDTH), BF16),
        scratch_shapes=[pltpu.VMEM((seq // ATT_TQ, 2, ATT_TQ, LANES), F32)] * 2,
        compiler_params=pltpu.CompilerParams(
            dimension_semantics=("arbitrary", "arbitrary"),
            vmem_limit_bytes=VMEM_LIMIT),
        name="prompt_attn",
    )(q, k, v, k_meta, v_meta)


def _sample_attn_kernel(pt_ref, qabs_ref, qpe_ref, wukt_ref, wuv_ref, tlat_ref, tkpe_ref,
                        lat_hbm, kpe_hbm, o_ref,
                        latb_sc, lhs_sc, lat_buf, kpe_buf, sems):
    n_pages = lat_buf.shape[1]
    chunk_pages = (CHUNK_PAGES,) * (n_pages // CHUNK_PAGES)
    b = pl.program_id(0)
    last = pl.num_programs(0) - 1
    slot = b % 2
    nq = qabs_ref.shape[0]
    reps = nq // N_HEADS
    nkn = N_HEADS * QK_NOPE
    qpe = qpe_ref[...]

    ngroup = lat_buf.shape[1] // DMA_GROUP

    def group_copies(bb, sl, gg):
        copies = []
        for j in range(gg * DMA_GROUP, (gg + 1) * DMA_GROUP):
            page = pt_ref[bb, j]
            copies.append(pltpu.make_async_copy(lat_hbm.at[page], lat_buf.at[sl, j], sems.at[0, sl]))
            copies.append(pltpu.make_async_copy(kpe_hbm.at[page], kpe_buf.at[sl, j], sems.at[1, sl]))
        return copies

    @pl.when(b == 0)
    def _():
        lhs_sc[0:nkn, :] = wukt_ref[...]
        for gg in range(ngroup):
            for cp in group_copies(0, 0, gg):
                cp.start()

    nxt = jnp.minimum(b + 1, last)

    lhs_sc[nkn:nkn + nq, :] = qabs_ref[...]
    state = (jnp.full((nq, LANES), -jnp.inf, F32), jnp.zeros((nq, LANES), F32),
             jnp.zeros((nq, KV_LORA), F32))

    def scores(lat_b, kpe_t):
        nk = lat_b.shape[0]
        full = _dot_nt(lhs_sc[...], lat_b)
        knt = full[:nkn]
        nsum = jnp.sum((knt * knt).reshape(QK_NOPE, N_HEADS, nk), axis=0)
        rsum = jnp.sum(kpe_t * kpe_t, axis=0, keepdims=True)
        r = lax.rsqrt((nsum + rsum) * (1.0 / QK_HEAD) + EPS)
        s = full[nkn:] + _dot(qpe, kpe_t.astype(BF16))
        return s * jnp.concatenate([r] * reps, axis=0)

    for gg in range(ngroup):
        for cp in group_copies(b, slot, gg):
            cp.wait()

    tail_b = tlat_ref[...].astype(BF16)
    s = scores(tail_b, tkpe_ref[...])
    key = lax.broadcasted_iota(jnp.int32, s.shape, 1)
    qt = lax.broadcasted_iota(jnp.int32, s.shape, 0) // N_HEADS
    s_tail = jnp.where(key <= qt, s, -jnp.inf)

    def chunk_scores(p0, npg):
        parts = [s_tail] if p0 == 0 else []
        for j in range(p0, p0 + npg, 2):
            for jj in (j, j + 1):
                latb_sc[jj * PAGE_SIZE:(jj + 1) * PAGE_SIZE, :] = lat_buf[slot, jj].astype(BF16)
            kpe_t = jnp.concatenate([kpe_buf[slot, j], kpe_buf[slot, j + 1]], axis=1)
            parts.append(scores(latb_sc[j * PAGE_SIZE:(j + 2) * PAGE_SIZE, :], kpe_t))
            if (j + 2) % DMA_ISSUE_PAGES == 0 and (j + 2) // DMA_ISSUE_PAGES <= ngroup:
                for cp in group_copies(nxt, 1 - slot, (j + 2) // DMA_ISSUE_PAGES - 1):
                    cp.start()
        return jnp.concatenate(parts, axis=1)

    def chunk_update(state, s, p0, npg):
        rows = slice(p0 * PAGE_SIZE, (p0 + npg) * PAGE_SIZE)
        if p0 == 0:
            pv = lambda p: _dot(p[:, :TAIL_KEYS], tail_b) + _dot(p[:, TAIL_KEYS:], latb_sc[rows, :])
        else:
            pv = lambda p: _dot(p, latb_sc[rows, :])
        return _softmax_step(s, state, pv)

    starts = [sum(chunk_pages[:i]) for i in range(len(chunk_pages))]
    pending = []
    for p0, npg in zip(starts, chunk_pages):
        pending.append((chunk_scores(p0, npg), p0, npg))
        if len(pending) > CHUNK_LOOKAHEAD:
            state = chunk_update(state, *pending.pop(0))
    for item in pending:
        state = chunk_update(state, *item)

    _, l_fin, acc_fin = state
    o_lat = (acc_fin / jnp.sum(l_fin, axis=-1, keepdims=True)).astype(BF16)
    full = _dot(o_lat, wuv_ref[...])
    row_h = lax.broadcasted_iota(jnp.int32, full.shape, 0) % N_HEADS
    col_h = lax.broadcasted_iota(jnp.int32, full.shape, 1) // V_HEAD
    own = jnp.where(row_h == col_h, full, 0.0)
    o_ref[...] = jnp.sum(own.reshape(reps, N_HEADS, ATTN_WIDTH), axis=1)

    @pl.when(b == last)
    def _():
        for gg in range(ngroup):
            for cp in group_copies(last, 1 - slot, gg):
                cp.wait()


def _sample_attn_call(page_table, qabs, qpe, w_ukt, w_uv, tail_lat, tail_kpe_t, cache_lat, cache_kpe_t):
    nseq, n_pages = page_table.shape
    nq = qabs.shape[1]
    assert n_pages % DMA_GROUP == 0 and DMA_GROUP % 2 == 0 and DMA_ISSUE_PAGES % 2 == 0
    assert DMA_ISSUE_PAGES <= DMA_GROUP
    assert n_pages % CHUNK_PAGES == 0 and CHUNK_PAGES % 2 == 0
    grid = (nseq,)

    per_seq = lambda rows, width: pl.BlockSpec((None, rows, width), lambda b, pt: (b, 0, 0))
    whole = lambda shape: pl.BlockSpec(shape, lambda b, pt: (0,) * len(shape),
                                       pipeline_mode=pl.Buffered(1))
    in_specs = [per_seq(nq, KV_LORA), per_seq(nq, QK_ROPE), whole(w_ukt.shape), whole(w_uv.shape),
                per_seq(TAIL_KEYS, KV_LORA), per_seq(QK_ROPE, TAIL_KEYS),
                pl.BlockSpec(memory_space=pl.ANY), pl.BlockSpec(memory_space=pl.ANY)]
    reps = nq // N_HEADS
    return pl.pallas_call(
        _sample_attn_kernel,
        grid_spec=pltpu.PrefetchScalarGridSpec(
            num_scalar_prefetch=1,
            grid=grid,
            in_specs=in_specs,
            out_specs=pl.BlockSpec((None, reps, ATTN_WIDTH), lambda b, pt: (b, 0, 0)),
            scratch_shapes=[pltpu.VMEM((n_pages * PAGE_SIZE, KV_LORA), BF16),
                            pltpu.VMEM((N_HEADS * QK_NOPE + nq, KV_LORA), BF16),
                            pltpu.VMEM((2, n_pages, PAGE_SIZE, KV_LORA), F32),
                            pltpu.VMEM((2, n_pages, QK_ROPE, PAGE_SIZE), F32),
                            pltpu.SemaphoreType.DMA((2, 2))],
        ),
        out_shape=jax.ShapeDtypeStruct((nseq, reps, ATTN_WIDTH), F32),
        compiler_params=pltpu.CompilerParams(dimension_semantics=("arbitrary",),
                                             vmem_limit_bytes=VMEM_LIMIT),
        name="sample_attn",
    )(page_table, qabs, qpe, w_ukt, w_uv, tail_lat, tail_kpe_t, cache_lat, cache_kpe_t)


def _merge_ffn_kernel(x_ref, attn_ref, convn_ref, g_attn_ref, w_o_ref, g_ffn_ref, w_up_ref,
                      w_down_ref, y_ref):
    an = _rms(attn_ref[...].astype(F32), g_attn_ref[...]).astype(BF16)
    x1 = x_ref[...] + (_dot(an, w_o_ref[0:ATTN_WIDTH, :]) + _dot(convn_ref[...], w_o_ref[ATTN_WIDTH:, :]))
    hf = _rms(x1, g_ffn_ref[...]).astype(BF16)
    ffn = None
    for c in range(D_FF // FF_CHUNK):
        cs = slice(c * FF_CHUNK, (c + 1) * FF_CHUNK)
        up = jnp.maximum(_dot(hf, w_up_ref[:, cs]), 0.0)
        part = _dot((up * up).astype(BF16), w_down_ref[cs, :])
        ffn = part if ffn is None else ffn + part
    y_ref[...] = x1 + ffn


def _merge_ffn_call(x, attn, convn, weights, rows):
    n = x.shape[0]
    row = lambda w: pl.BlockSpec((rows, w), lambda i: (i, 0))
    return pl.pallas_call(
        _merge_ffn_kernel,
        grid=(n // rows,),
        in_specs=[row(D_MODEL), row(ATTN_WIDTH), row(CONV_CH)] + [_const_spec(w.shape) for w in weights],
        out_specs=row(D_MODEL),
        out_shape=jax.ShapeDtypeStruct((n, D_MODEL), F32),
        compiler_params=pltpu.CompilerParams(dimension_semantics=("arbitrary",),
                                             vmem_limit_bytes=VMEM_LIMIT),
        name="merge_ffn",
    )(x, attn, convn, *weights)


def _slots(w, width):
    k = w.shape[0]
    w = w.reshape(k, N_HEADS, width)
    return jnp.pad(w, ((0, 0), (0, 0), (0, SLOT - width))).reshape(k, HEADS_W)


def _rot_partner(w):
    return jnp.concatenate([-w[..., HALF_ROPE:], w[..., :HALF_ROPE]], axis=-1)


def _rope_slot(w):
    return jnp.pad(w, ((0, 0), (QK_NOPE, SLOT - QK_HEAD)))


def _layer_weights(w_in, q_lora_g, kv_lora_g, w_uq, w_ukv, q_norm_g, k_norm_g, conv_w, conv_b,
                   conv_out_g, norm_mix_g):
    o1 = Q_LORA
    o2 = o1 + KV_LORA
    o3 = o2 + QK_ROPE
    w_kpe = w_in[:, o2:o3]
    w_in_p = jnp.concatenate(
        [w_in[:, :o2], w_in[:, o3:], _rope_slot(w_kpe), _rope_slot(_rot_partner(w_kpe))],
        axis=1).astype(BF16)
    uq = w_uq.reshape(Q_LORA, N_HEADS, QK_HEAD)
    w_uq_p = _slots(w_uq, QK_HEAD).astype(BF16)
    uq_rot = jnp.pad(_rot_partner(uq[..., QK_NOPE:]), ((0, 0), (0, 0), (QK_NOPE, SLOT - QK_HEAD)))
    w_uq_r = uq_rot.reshape(Q_LORA, HEADS_W).astype(BF16)
    ukv = w_ukv.reshape(KV_LORA, N_HEADS, QK_NOPE + V_HEAD)
    w_uk_p = _slots(ukv[..., :QK_NOPE].reshape(KV_LORA, N_HEADS * QK_NOPE), QK_NOPE).astype(BF16)
    uv = ukv[..., QK_NOPE:]
    w_uv = uv.reshape(KV_LORA, ATTN_WIDTH).astype(BF16)
    odd_head = (np.arange(N_HEADS) % 2 == 1)[None, :, None]
    w_uv_slots = jnp.where(odd_head, jnp.concatenate([jnp.zeros_like(uv), uv], axis=-1),
                           jnp.concatenate([uv, jnp.zeros_like(uv)], axis=-1))
    w_uv_slots = w_uv_slots.reshape(KV_LORA, HEADS_W).astype(BF16)
    w_ukt = jnp.transpose(ukv[..., :QK_NOPE], (2, 1, 0)).reshape(N_HEADS * QK_NOPE, KV_LORA).astype(BF16)
    pad_g = lambda g: jnp.pad(g, (0, SLOT - QK_HEAD))[None, :]
    gq = pad_g(q_norm_g) * (ATTN_SCALE * LOG2_E)
    gk = pad_g(k_norm_g)
    mixer = (norm_mix_g[None, :], w_in_p, q_lora_g[None, :], w_uq_p, w_uq_r, kv_lora_g[None, :],
             w_uk_p, w_uv_slots, gq, gk, conv_w, conv_b[None, :], conv_out_g[None, :])
    return mixer, w_ukt, w_uv


def _rope_slot_tables(pos):
    inv_freq = ROPE_THETA ** (-(np.arange(0, QK_ROPE, 2, dtype=np.float64) / QK_ROPE))
    ang = pos.astype(np.float64)[:, None] * inv_freq[None, :]
    n = pos.shape[0]
    cos2 = np.concatenate([np.cos(ang)] * 2, axis=1)
    sin2 = np.concatenate([np.sin(ang)] * 2, axis=1)
    cos_t = np.concatenate([np.ones((n, QK_NOPE)), cos2, np.zeros((n, SLOT - QK_HEAD))], axis=1)
    sin_t = np.pad(sin2, ((0, 0), (QK_NOPE, SLOT - QK_HEAD)))
    return jnp.asarray(cos_t, F32), jnp.asarray(sin_t, F32)


def kernel(x_prompt, x_sample, cache_kv_latent, cache_k_rope, state_conv, page_table, meta_tokens,
           norm_mix_g, w_in, q_lora_g, kv_lora_g, w_uq, w_ukv, q_norm_g, k_norm_g, conv_w, conv_b,
           attn_out_g, conv_out_g, w_o, norm_ffn_g, w_up, w_down):
    depth = w_in.shape[0]
    assert depth == 1, "the prompt and sample streams are chained for a single layer"
    nb, seq, _ = x_prompt.shape
    nseq, dec, _ = x_sample.shape
    past = page_table.shape[1] * PAGE_SIZE
    l = 0

    mixer_w, w_ukt, w_uv = _layer_weights(w_in[l], q_lora_g[l], kv_lora_g[l], w_uq[l], w_ukv[l],
                                          q_norm_g[l], k_norm_g[l], conv_w[l], conv_b[l],
                                          conv_out_g[l], norm_mix_g[l])
    ffn_w = (attn_out_g[l][None, :], w_o[l].astype(BF16), norm_ffn_g[l][None, :],
             w_up[l].astype(BF16), w_down[l].astype(BF16))

    ns = nseq * dec
    xs_rows = jnp.concatenate([x_sample.reshape(ns, D_MODEL), meta_tokens.astype(F32)], axis=0)
    t_in_seq = np.concatenate([np.tile(np.arange(dec), nseq), np.arange(N_META)])
    pos_s = np.concatenate([np.tile(past + np.arange(dec), nseq), np.arange(N_META)])
    cos_s, sin_s = _rope_slot_tables(pos_s)
    m1 = jnp.asarray((t_in_seq >= 1)[:, None], F32)
    m2 = jnp.asarray((t_in_seq >= 2)[:, None], F32)
    st = state_conv[l].astype(F32)
    zrow = jnp.zeros((nseq, 1, CONV_CH), F32)
    s1 = jnp.concatenate([st[:, 1:2], zrow, zrow, zrow], axis=1)[:, :dec]
    s2 = jnp.concatenate([st[:, 0:1], st[:, 1:2], zrow, zrow], axis=1)[:, :dec]
    zmeta = jnp.zeros((N_META, CONV_CH), F32)
    s1 = jnp.concatenate([s1.reshape(ns, CONV_CH), zmeta], axis=0)
    s2 = jnp.concatenate([s2.reshape(ns, CONV_CH), zmeta], axis=0)
    (qk_s, qabs_s, k_s, v_s, lat_s, kpe_s, convn_s, u_s) = _mixer_call(
        xs_rows, cos_s, sin_s, (m1, m2, s1, s2), mixer_w, sample_mode=True)

    pad_meta = lambda a: jnp.pad(a[ns:], ((0, LANES - N_META), (0, 0)))
    k_meta, v_meta = pad_meta(k_s), pad_meta(v_s)
    lat_meta, kpe_meta, u_meta = lat_s[ns:], kpe_s[ns:], u_s[ns:]

    cos_p, sin_p = _rope_slot_tables(N_META + np.arange(seq))
    q_p, k_p, v_p, lat_p, kpe_p, convn_p, utail_p = _mixer_call(
        x_prompt, cos_p, sin_p, (u_meta[N_META - SUBLANES:], lat_meta), mixer_w, sample_mode=False)
    attn_p = _prompt_attn_call(q_p, k_p, v_p, k_meta, v_meta)
    y_prompt = _merge_ffn_call(x_prompt.reshape(nb * seq, D_MODEL), attn_p.reshape(nb * seq, ATTN_WIDTH),
                               convn_p.reshape(nb * seq, CONV_CH), ffn_w, ROW_TILE)
    y_prompt = y_prompt.reshape(nb, seq, D_MODEL)
    bcast = lambda a: jnp.broadcast_to(a[None], (nb,) + a.shape)
    new_lat_prompt = lat_p[None]
    new_kpe_prompt = jnp.concatenate([bcast(kpe_meta), kpe_p], axis=1)[None]
    new_conv_prompt = utail_p[:, -1, SUBLANES - (CONV_W - 1):][None]

    nq = dec * N_HEADS
    qabs = qabs_s[:ns].reshape(nseq, nq, KV_LORA)
    qpe = qk_s[:ns].reshape(ns, N_HEADS, SLOT)[:, :, QK_NOPE:QK_HEAD].reshape(nseq, nq, QK_ROPE)
    assert dec <= TAIL_KEYS
    pad_tail = lambda a: jnp.pad(a[:ns].reshape(nseq, dec, -1), ((0, 0), (0, TAIL_KEYS - dec), (0, 0)))
    attn_s = _sample_attn_call(page_table, qabs, qpe, w_ukt, w_uv, pad_tail(lat_s),
                               jnp.swapaxes(pad_tail(kpe_s), 1, 2),
                               cache_kv_latent[l], jnp.swapaxes(cache_k_rope[l], 1, 2))
    y_sample = _merge_ffn_call(x_sample.reshape(ns, D_MODEL), attn_s.reshape(ns, ATTN_WIDTH),
                               convn_s[:ns], ffn_w, ns)
    y_sample = y_sample.reshape(nseq, dec, D_MODEL)
    new_lat_sample = lat_s[:ns].reshape(nseq, dec, KV_LORA)[None]
    new_kpe_sample = kpe_s[:ns].reshape(nseq, dec, QK_ROPE)[None]
    us = jnp.concatenate([st, u_s[:ns].reshape(nseq, dec, CONV_CH)], axis=1)
    new_conv_sample = us[:, -(CONV_W - 1):][None]

    return (y_prompt, y_sample, new_lat_prompt, new_kpe_prompt, new_conv_prompt,
            new_lat_sample, new_kpe_sample, new_conv_sample)
```
